```python
import jax, jax.numpy as jnp
from jax import lax
import numpy as np

D_MODEL = 1024
BATCH = 2
SEQ = 16384
DEPTH = 1
DEC_BATCH = 16
DEC_SEQ = 32
PAST_LEN = 2048

CHUNK = 64
GLA_HEADS = 4
GLA_DK = 64
GLA_DV = 128
GLA_RANK = 16
GLA_TAU = 16.0
GLA_QK = GLA_HEADS * GLA_DK
GLA_V = GLA_HEADS * GLA_DV
FOX_HEADS = 8
FOX_DH = 64
FOX_W = FOX_HEADS * FOX_DH
Q_BLOCK = 128
D_FF = ((8 * D_MODEL) // 3 + 255) // 256 * 256
DN_ALPHA = (2.0 * DEPTH) ** 0.25
DN_BETA = (8.0 * DEPTH) ** -0.25
LN_EPS = 1e-5
RMS_EPS = 1e-5
SPLIT_SIZES = (GLA_QK, GLA_QK, GLA_V, GLA_RANK, GLA_V, FOX_W, FOX_W, FOX_W, FOX_HEADS, D_MODEL, D_MODEL)
D_IN = sum(SPLIT_SIZES)
VALUE_SPLITS = (2, 7)

kernel_name = "gla_fox_deepnorm_streaming_encoder_step"


def _split_points():
    return [int(i) for i in np.cumsum(SPLIT_SIZES)[:-1]]


def layer_norm(x, g, b):
    xf = x.astype(jnp.float32)
    mu = jnp.mean(xf, axis=-1, keepdims=True)
    var = jnp.mean(jnp.square(xf - mu), axis=-1, keepdims=True)
    return ((xf - mu) * lax.rsqrt(var + LN_EPS)).astype(x.dtype) * g + b


def gla_block(S, xs):
    q, k, v, la = xs
    L = q.shape[1]
    b = jnp.cumsum(la, axis=1)
    causal = jnp.tril(jnp.ones((L, L), dtype=bool))
    rel = b[:, :, None] - b[:, None, :]
    decay = jnp.exp(jnp.where(causal[None, :, :, None, None], rel, -jnp.inf))
    A = jnp.einsum('bthd,bshd,btshd->bhts', q.astype(jnp.float32), k.astype(jnp.float32), decay)
    intra = jnp.einsum('bhts,bshv->bthv', A, v.astype(jnp.float32))
    inter = jnp.einsum('bthd,bhdv->bthv', q * jnp.exp(b), S)
    b_last = b[:, -1]
    k_dec = k * jnp.exp(b_last[:, None] - b)
    S_new = jnp.exp(b_last)[..., None] * S + jnp.einsum('bshd,bshv->bhdv', k_dec, v.astype(jnp.float32))
    return S_new, inter + intra


def gla_scan(S0, q, k, v, la):
    B, L = q.shape[0], q.shape[1]
    c = min(CHUNK, L)
    nc = L // c

    def to_blocks(t):
        return jnp.moveaxis(t.reshape((B, nc, c) + t.shape[2:]), 1, 0)

    S_final, o = lax.scan(gla_block, S0.astype(jnp.float32),
                          (to_blocks(q), to_blocks(k), to_blocks(v), to_blocks(la)))
    o = jnp.moveaxis(o, 0, 1).reshape(B, L, GLA_HEADS, GLA_DV)
    return S_final, o


def fox_attention(q, k, v, c_q, c_k, q_pos, k_pos):
    B, Lq, H, Dh = q.shape
    bs = min(Q_BLOCK, Lq)
    nb = Lq // bs
    scale = Dh ** -0.5
    qb = jnp.moveaxis(q.reshape(B, nb, bs, H, Dh), 1, 0)
    cb = jnp.moveaxis(c_q.reshape(B, nb, bs, H), 1, 0)
    pb = q_pos.reshape(nb, bs)
    ck_h = jnp.swapaxes(c_k, 1, 2)

    def one_block(args):
        qi, ci, pi = args
        s = jnp.einsum('bqhd,bkhd->bhqk', qi, k).astype(jnp.float32) * scale
        s = s + jnp.swapaxes(ci, 1, 2)[..., None] - ck_h[:, :, None, :]
        s = jnp.where((pi[:, None] >= k_pos[None, :])[None, None], s, -jnp.inf)
        p = jax.nn.softmax(s, axis=-1)
        return jnp.einsum('bhqk,bkhd->bqhd', p.astype(v.dtype), v)

    o = lax.map(one_block, (qb, cb, pb))
    return jnp.moveaxis(o, 0, 1).reshape(B, Lq, H, Dh)


def hybrid_layer(x, gla_s0, fox_k_past, fox_v_past, fox_lf_past,
                 w_in, b_in, w_alpha2, b_alpha2, gla_norm_g, w_proj_gla, w_proj_fox, w_out,
                 ln1_g, ln1_b, w_ffn_gate, w_ffn_up, w_ffn_down, ln2_g, ln2_b):
    B, L, _ = x.shape
    h = x @ w_in + b_in
    gq, gk, gv, ga, gr, fq, fk, fv, ff, zg, zf = jnp.split(h, _split_points(), axis=-1)

    gq = gq.reshape(B, L, GLA_HEADS, GLA_DK) * (GLA_DK ** -0.5)
    gk = gk.reshape(B, L, GLA_HEADS, GLA_DK)
    gv = gv.reshape(B, L, GLA_HEADS, GLA_DV)
    la = jax.nn.log_sigmoid((ga @ w_alpha2 + b_alpha2).astype(jnp.float32)) / GLA_TAU
    la = la.reshape(B, L, GLA_HEADS, GLA_DK)
    gla_state, o = gla_scan(gla_s0, gq, gk, gv, la)
    o = o * lax.rsqrt(jnp.mean(jnp.square(o), axis=-1, keepdims=True) + RMS_EPS)
    o = o.reshape(B, L, GLA_V) * gla_norm_g * jax.nn.silu(gr.astype(jnp.float32))
    y_gla = o.astype(x.dtype) @ w_proj_gla

    fq = fq.reshape(B, L, FOX_HEADS, FOX_DH)
    fk = fk.reshape(B, L, FOX_HEADS, FOX_DH)
    fv = fv.reshape(B, L, FOX_HEADS, FOX_DH)
    lf = jax.nn.log_sigmoid(ff.astype(jnp.float32))
    if fox_k_past is None:
        k_all, v_all, lf_all = fk, fv, lf
    else:
        k_all = jnp.concatenate([fox_k_past.astype(fk.dtype), fk], axis=1)
        v_all = jnp.concatenate([fox_v_past.astype(fv.dtype), fv], axis=1)
        lf_all = jnp.concatenate([fox_lf_past.astype(jnp.float32), lf], axis=1)
    Lk = k_all.shape[1]
    past = Lk - L
    c_all = jnp.cumsum(lf_all, axis=1)
    o_f = fox_attention(fq, k_all, v_all, c_all[:, past:], c_all,
                        past + jnp.arange(L), jnp.arange(Lk))
    y_fox = o_f.reshape(B, L, FOX_W) @ w_proj_fox

    m = jax.nn.sigmoid(zg) * y_gla + jax.nn.sigmoid(zf) * y_fox
    x1 = layer_norm(DN_ALPHA * x + m @ w_out, ln1_g, ln1_b)

    f = (jax.nn.silu(x1 @ w_ffn_gate) * (x1 @ w_ffn_up)) @ w_ffn_down
    x2 = layer_norm(DN_ALPHA * x1 + f, ln2_g, ln2_b)
    return x2, gla_state, fk, fv, lf


def setup_inputs(seed: int = 0) -> dict:
    key = jax.random.key(seed)
    ks = jax.random.split(key, 24)
    nrm = jax.random.normal
    f32 = jnp.float32
    col_scale = np.concatenate([np.full((n,), DN_BETA if i in VALUE_SPLITS else 1.0, np.float32)
                                for i, n in enumerate(SPLIT_SIZES)])
    return {
        "x_prompt": nrm(ks[0], (BATCH, SEQ, D_MODEL), f32),
        "x_sample": nrm(ks[1], (DEC_BATCH, DEC_SEQ, D_MODEL), f32),
        "state_gla": 2.0 * nrm(ks[2], (DEPTH, DEC_BATCH, GLA_HEADS, GLA_DK, GLA_DV), f32),
        "cache_fox_k": nrm(ks[3], (DEPTH, DEC_BATCH, PAST_LEN, FOX_HEADS, FOX_DH), f32),
        "cache_fox_v": DN_BETA * nrm(ks[4], (DEPTH, DEC_BATCH, PAST_LEN, FOX_HEADS, FOX_DH), f32),
        "cache_fox_logf": jax.nn.log_sigmoid(nrm(ks[5], (DEPTH, DEC_BATCH, PAST_LEN, FOX_HEADS), f32)),
        "w_in": nrm(ks[6], (DEPTH, D_MODEL, D_IN), f32) * (D_MODEL ** -0.5) * jnp.asarray(col_scale),
        "b_in": 0.02 * nrm(ks[7], (DEPTH, D_IN), f32),
        "w_alpha2": nrm(ks[8], (DEPTH, GLA_RANK, GLA_QK), f32) * (GLA_RANK ** -0.5),
        "b_alpha2": 0.02 * nrm(ks[9], (DEPTH, GLA_QK), f32),
        "gla_norm_g": 1.0 + 0.02 * nrm(ks[10], (DEPTH, GLA_V), f32),
        "w_proj_gla": nrm(ks[11], (DEPTH, GLA_V, D_MODEL), f32) * (GLA_V ** -0.5),
        "w_proj_fox": nrm(ks[12], (DEPTH, FOX_W, D_MODEL), f32) * (FOX_W ** -0.5),
        "w_out": nrm(ks[13], (DEPTH, D_MODEL, D_MODEL), f32) * (D_MODEL ** -0.5) * DN_BETA,
        "ln1_g": 1.0 + 0.02 * nrm(ks[14], (DEPTH, D_MODEL), f32),
        "ln1_b": 0.02 * nrm(ks[15], (DEPTH, D_MODEL), f32),
        "w_ffn_gate": nrm(ks[16], (DEPTH, D_MODEL, D_FF), f32) * (D_MODEL ** -0.5) * DN_BETA,
        "w_ffn_up": nrm(ks[17], (DEPTH, D_MODEL, D_FF), f32) * (D_MODEL ** -0.5) * DN_BETA,
        "w_ffn_down": nrm(ks[18], (DEPTH, D_FF, D_MODEL), f32) * (D_FF ** -0.5) * DN_BETA,
        "ln2_g": 1.0 + 0.02 * nrm(ks[19], (DEPTH, D_MODEL), f32),
        "ln2_b": 0.02 * nrm(ks[20], (DEPTH, D_MODEL), f32),
    }


def reference(x_prompt, x_sample, state_gla, cache_fox_k, cache_fox_v, cache_fox_logf,
              w_in, b_in, w_alpha2, b_alpha2, gla_norm_g, w_proj_gla, w_proj_fox, w_out,
              ln1_g, ln1_b, w_ffn_gate, w_ffn_up, w_ffn_down, ln2_g, ln2_b):
    yp, ys = x_prompt, x_sample
    p_gla, p_k, p_v, p_lf = [], [], [], []
    s_gla, s_k, s_v, s_lf = [], [], [], []
    for l in range(DEPTH):
        w = (w_in[l], b_in[l], w_alpha2[l], b_alpha2[l], gla_norm_g[l], w_proj_gla[l], w_proj_fox[l],
             w_out[l], ln1_g[l], ln1_b[l], w_ffn_gate[l], w_ffn_up[l], w_ffn_down[l], ln2_g[l], ln2_b[l])
        s0 = jnp.zeros((yp.shape[0], GLA_HEADS, GLA_DK, GLA_DV), jnp.float32)
        yp, g1, k1, v1, lf1 = hybrid_layer(yp, s0, None, None, None, *w)
        p_gla.append(g1); p_k.append(k1); p_v.append(v1); p_lf.append(lf1)
        ys, g2, k2, v2, lf2 = hybrid_layer(ys, state_gla[l], cache_fox_k[l], cache_fox_v[l],
                                           cache_fox_logf[l], *w)
        s_gla.append(g2); s_k.append(k2); s_v.append(v2); s_lf.append(lf2)
    return (yp, ys,
            jnp.stack(p_gla), jnp.stack(p_k), jnp.stack(p_v), jnp.stack(p_lf),
            jnp.stack(s_gla), jnp.stack(s_k), jnp.stack(s_v), jnp.stack(s_lf))
```

```python
import functools

import jax
import jax.numpy as jnp
import numpy as np
from jax import lax
from jax.experimental import pallas as pl
from jax.experimental.pallas import tpu as pltpu

f32 = jnp.float32
bf16 = jnp.bfloat16

D_MODEL = 1024
GLA_HEADS = 4
GLA_DK = 64
GLA_DV = 128
GLA_RANK = 16
GLA_TAU = 16.0
GLA_QK = GLA_HEADS * GLA_DK
GLA_V = GLA_HEADS * GLA_DV
GLA_CHUNK = 64
FOX_HEADS = 8
FOX_DH = 64
FOX_W = FOX_HEADS * FOX_DH
D_FF = 2816
LN_EPS = 1e-5
RMS_EPS = 1e-5
SPLIT_SIZES = (GLA_QK, GLA_QK, GLA_V, GLA_RANK, GLA_V, FOX_W, FOX_W, FOX_W, FOX_HEADS, D_MODEL, D_MODEL)
_OFF = np.concatenate([[0], np.cumsum(SPLIT_SIZES)]).astype(int)

LANES = 128
SMALL_W = LANES
MAIN_W = 2 * GLA_QK + 2 * GLA_V + 3 * FOX_W
ATT_BLOCK = 128
SKIP_T = 110.0
NORM_SLACK = 1.02
NEG_BIG = -1e30
VMEM_LIMIT = 56 * 1024 * 1024


def _dot(a, b):
    return jnp.dot(a, b, preferred_element_type=f32)


def _dot_nt(a, b):
    return lax.dot_general(a, b, (((1,), (1,)), ((), ())), preferred_element_type=f32)


def _dot_tn(a, b):
    return lax.dot_general(a, b, (((0,), (0,)), ((), ())), preferred_element_type=f32)


def _dot_hi(a, b):
    return jnp.dot(a, b, preferred_element_type=f32, precision=lax.Precision.HIGHEST)


def _log_sigmoid(x):
    return jnp.minimum(x, 0.0) - jnp.log1p(jnp.exp(-jnp.abs(x)))


def _const_spec(shape):
    return pl.BlockSpec(shape, lambda *_: (0,) * len(shape))


def _inproj_kernel(x_ref, w_ref, b_ref, wa_ref, ba_ref, indq_ref, indk_ref,
                   gq_ref, gk_ref, gv_ref, la_ref, sg_ref, fq_ref, fk_ref, fv_ref,
                   fkb_ref, fvb_ref, aux_ref):
    xb = x_ref[...].astype(bf16)

    def group(lo, width):
        return _dot(xb, w_ref[:, lo:lo + width]) + b_ref[:, lo:lo + width]

    o = 0
    gq_ref[...] = (group(o, GLA_QK) * (GLA_DK ** -0.5)).astype(bf16)
    o += GLA_QK
    gk_ref[...] = group(o, GLA_QK).astype(bf16)
    o += GLA_QK
    gv_ref[...] = group(o, GLA_V).astype(bf16)
    o += GLA_V
    gr = group(o, GLA_V)
    sg_ref[...] = (gr * jax.nn.sigmoid(gr)).astype(bf16)
    o += GLA_V
    fq = (group(o, FOX_W) * (FOX_DH ** -0.5)).astype(bf16)
    fq_ref[...] = fq
    o += FOX_W
    fk = group(o, FOX_W)
    fk_ref[...] = fk
    fkb = fk.astype(bf16)
    fkb_ref[...] = fkb
    o += FOX_W
    fv = group(o, FOX_W)
    fv_ref[...] = fv
    fvb_ref[...] = fv.astype(bf16)
    o += FOX_W
    small = group(o, SMALL_W)
    la_pre = _dot(small.astype(bf16), wa_ref[...]) + ba_ref[...]
    la_ref[...] = _log_sigmoid(la_pre) * (1.0 / GLA_TAU)
    lane = lax.broadcasted_iota(jnp.int32, (1, SMALL_W), 1)
    lf = jnp.where(lane < FOX_HEADS, _log_sigmoid(small), 0.0)
    fq32 = fq.astype(f32)
    fk32 = fkb.astype(f32)
    nq2 = _dot((fq32 * fq32).astype(bf16), indq_ref[...])
    nk2 = _dot((fk32 * fk32).astype(bf16), indk_ref[...])
    aux_ref[...] = lf + nq2 + nk2


def _inproj(x2d, w1, b1, wa, ba, indq, indk):
    n = x2d.shape[0]
    tm = min(512, n)
    assert n % tm == 0
    row = lambda w: pl.BlockSpec((tm, w), lambda i: (i, 0))
    out_shapes = (
        jax.ShapeDtypeStruct((n, GLA_QK), bf16),
        jax.ShapeDtypeStruct((n, GLA_QK), bf16),
        jax.ShapeDtypeStruct((n, GLA_V), bf16),
        jax.ShapeDtypeStruct((n, GLA_QK), f32),
        jax.ShapeDtypeStruct((n, GLA_V), bf16),
        jax.ShapeDtypeStruct((n, FOX_W), bf16),
        jax.ShapeDtypeStruct((n, FOX_W), f32),
        jax.ShapeDtypeStruct((n, FOX_W), f32),
        jax.ShapeDtypeStruct((n, FOX_W), bf16),
        jax.ShapeDtypeStruct((n, FOX_W), bf16),
        jax.ShapeDtypeStruct((n, SMALL_W), f32),
    )
    return pl.pallas_call(
        _inproj_kernel,
        grid=(n // tm,),
        in_specs=[row(D_MODEL), _const_spec(w1.shape), _const_spec(b1.shape), _const_spec(wa.shape),
                  _const_spec(ba.shape), _const_spec(indq.shape), _const_spec(indk.shape)],
        out_specs=[row(s.shape[1]) for s in out_shapes],
        out_shape=out_shapes,
        compiler_params=pltpu.CompilerParams(dimension_semantics=("arbitrary",), vmem_limit_bytes=VMEM_LIMIT),
        name="inproj",
    )(x2d, w1, b1, wa, ba, indq, indk)


def _gla_kernel(q_ref, k_ref, v_ref, la_ref, sg_ref, g_ref, s0_ref, o_ref, sfin_ref, st_ref, *, chunk, nsub):
    t = pl.program_id(1)

    @pl.when(t == 0)
    def _():
        st_ref[...] = s0_ref[0]

    C = chunk
    r_i = lax.broadcasted_iota(jnp.int32, (C, C), 0)
    c_i = lax.broadcasted_iota(jnp.int32, (C, C), 1)
    causal = c_i <= r_i
    tril = causal.astype(bf16)
    lane = lax.broadcasted_iota(jnp.int32, (1, LANES), 1)
    first_half = lane < GLA_DK
    mid = C // 2

    for c in range(nsub):
        rows = slice(c * C, (c + 1) * C)
        la = la_ref[0, rows, :]
        la_hi = la.astype(bf16)
        la_lo = (la - la_hi.astype(f32)).astype(bf16)
        bcum = _dot(tril, la_hi) + _dot(tril, la_lo)
        b_last = bcum[C - 1:C, :]
        ref = bcum[mid:mid + 1, :]
        q = q_ref[0, rows, :].astype(f32)
        k = k_ref[0, rows, :].astype(f32)
        q_in = (q * jnp.exp(bcum)).astype(bf16)
        q_a = (q * jnp.exp(jnp.clip(bcum - ref, -80.0, 80.0))).astype(bf16)
        k_a = (k * jnp.exp(jnp.clip(ref - bcum, -80.0, 80.0))).astype(bf16)
        k_d = (k * jnp.exp(b_last - bcum)).astype(bf16)
        dec = jnp.exp(b_last)
        for p in range(GLA_HEADS // 2):
            ls = slice(p * LANES, (p + 1) * LANES)
            st = st_ref[p]
            stb = st.astype(bf16)
            upd = []
            for hh in range(2):
                h = 2 * p + hh
                hm = first_half if hh == 0 else jnp.logical_not(first_half)
                vs = slice(h * GLA_DV, (h + 1) * GLA_DV)
                v_h = v_ref[0, rows, vs]
                a = _dot_nt(jnp.where(hm, q_a[:, ls], jnp.zeros_like(q_a[:, ls])), k_a[:, ls])
                a = jnp.where(causal, a, 0.0).astype(bf16)
                o = _dot(a, v_h) + _dot_nt(jnp.where(hm, q_in[:, ls], jnp.zeros_like(q_in[:, ls])), stb)
                o = o * lax.rsqrt(jnp.mean(o * o, axis=-1, keepdims=True) + RMS_EPS)
                o = o * g_ref[:, vs] * sg_ref[0, rows, vs].astype(f32)
                o_ref[0, rows, vs] = o.astype(bf16)
                upd.append(_dot_tn(v_h, k_d[:, ls]))
            st_ref[p] = dec[:, ls] * st + jnp.where(first_half, upd[0], upd[1])

    @pl.when(t == pl.num_programs(1) - 1)
    def _():
        sfin_ref[0] = st_ref[...]


def _gla(gq, gk, gv, la, sg, g, s0t):
    B, L, _ = gq.shape
    C = min(GLA_CHUNK, L)
    tl = min(256, L)
    assert L % tl == 0 and tl % C == 0
    tok = lambda w: pl.BlockSpec((1, tl, w), lambda b, t: (b, t, 0))
    st_spec = pl.BlockSpec((1, 2, GLA_DV, LANES), lambda b, t: (b, 0, 0, 0))
    return pl.pallas_call(
        functools.partial(_gla_kernel, chunk=C, nsub=tl // C),
        grid=(B, L // tl),
        in_specs=[tok(GLA_QK), tok(GLA_QK), tok(GLA_V), tok(GLA_QK), tok(GLA_V),
                  pl.BlockSpec((1, GLA_V), lambda b, t: (0, 0)), st_spec],
        out_specs=[tok(GLA_V), st_spec],
        out_shape=(jax.ShapeDtypeStruct((B, L, GLA_V), bf16),
                   jax.ShapeDtypeStruct((B, 2, GLA_DV, LANES), f32)),
        scratch_shapes=[pltpu.VMEM((2, GLA_DV, LANES), f32)],
        compiler_params=pltpu.CompilerParams(dimension_semantics=("arbitrary", "arbitrary"),
                                             vmem_limit_bytes=VMEM_LIMIT),
        name="gla_scan",
    )(gq, gk, gv, la, sg, g, s0t)


def _state_to_pairs(s):
    B = s.shape[0]
    s = s.reshape(B, 2, 2, GLA_DK, GLA_DV)
    return jnp.transpose(s, (0, 1, 4, 2, 3)).reshape(B, 2, GLA_DV, 2 * GLA_DK)


def _pairs_to_state(st):
    B = st.shape[0]
    st = st.reshape(B, 2, GLA_DV, 2, GLA_DK)
    return jnp.transpose(st, (0, 1, 3, 4, 2)).reshape(B, GLA_HEADS, GLA_DK, GLA_DV)


def _meta_kernel(a_ref, ct_ref, start_ref, *, nc):
    H = FOX_HEADS
    r_i = lax.broadcasted_iota(jnp.int32, (LANES, LANES), 0)
    c_i = lax.broadcasted_iota(jnp.int32, (LANES, LANES), 1)
    upper = (r_i <= c_i).astype(f32)
    rj = lax.broadcasted_iota(jnp.int32, (nc, nc), 0)
    cj = lax.broadcasted_iota(jnp.int32, (nc, nc), 1)
    strict_lower = (cj < rj).astype(f32)
    eye = rj == cj

    def to_row(col):
        return jnp.sum(jnp.where(eye, jnp.broadcast_to(col, (nc, nc)), 0.0), axis=0, keepdims=True)

    start = None
    for h in range(H):
        y = _dot_hi(a_ref[0, h], upper)
        tot = jnp.broadcast_to(y[:, LANES - 1:LANES], (nc, LANES))
        c = y + _dot_hi(strict_lower, tot)
        ct_ref[0, h] = c
        cmax = jnp.max(c, axis=1, keepdims=True)
        cmin = jnp.min(c, axis=1, keepdims=True)
        qmax = jnp.sqrt(jnp.max(a_ref[0, H + h], axis=1, keepdims=True))
        kmax = jnp.sqrt(jnp.max(jnp.max(a_ref[0, 2 * H + h], axis=1, keepdims=True), axis=0, keepdims=True))
        thr = cmax + (2.0 * NORM_SLACK) * qmax * kmax + SKIP_T
        needed = (to_row(cmin) <= thr) & (cj <= rj)
        first = jnp.min(jnp.where(needed, cj, nc), axis=1, keepdims=True)
        start = first if start is None else jnp.minimum(start, first)
    start_ref[0] = to_row(start.astype(f32)).astype(jnp.int32)


def _fox_meta(aux_t):
    B, _, nc, _ = aux_t.shape
    return pl.pallas_call(
        functools.partial(_meta_kernel, nc=nc),
        grid=(B,),
        in_specs=[pl.BlockSpec((1, 3 * FOX_HEADS, nc, LANES), lambda b: (b, 0, 0, 0))],
        out_specs=[pl.BlockSpec((1, FOX_HEADS, nc, LANES), lambda b: (b, 0, 0, 0)),
                   pl.BlockSpec((1, 1, nc), lambda b: (b, 0, 0))],
        out_shape=(jax.ShapeDtypeStruct((B, FOX_HEADS, nc, LANES), f32),
                   jax.ShapeDtypeStruct((B, 1, nc), jnp.int32)),
        compiler_params=pltpu.CompilerParams(dimension_semantics=("arbitrary",), vmem_limit_bytes=VMEM_LIMIT),
        name="fox_meta",
    )(aux_t)


def _fox_prompt_kernel(start_ref, q_ref, cq_ref, k0_ref, k1_ref, k2_ref, v0_ref, v1_ref, v2_ref,
                       ct_ref, kh_ref, vh_ref, o_ref, m_s, l_s, acc_s, kbuf, vbuf, sem):
    T = ATT_BLOCK
    H = FOX_HEADS
    b = pl.program_id(0)
    i = pl.program_id(1)
    start = start_ref[b, i]
    r_i = lax.broadcasted_iota(jnp.int32, (T, T), 0)
    c_i = lax.broadcasted_iota(jnp.int32, (T, T), 1)
    causal = c_i <= r_i
    lane = lax.broadcasted_iota(jnp.int32, (1, LANES), 1)
    first_half = lane < FOX_DH
    i1 = jnp.maximum(i - 1, 0)
    i2 = jnp.maximum(i - 2, 0)

    def head_q(h):
        ls = slice((h // 2) * LANES, (h // 2 + 1) * LANES)
        hm = first_half if h % 2 == 0 else jnp.logical_not(first_half)
        qp = q_ref[0, :, ls]
        return ls, jnp.where(hm, qp, jnp.zeros_like(qp))

    def head_c(h):
        base = ct_ref[0, h, pl.ds(i, 1), 0:1]
        return base, cq_ref[0, :, h:h + 1] - base

    for h in range(H):
        ls, qm = head_q(h)
        base, cq = head_c(h)
        ck0 = ct_ref[0, h, pl.ds(i, 1), :] - base
        ck1 = jnp.where(i >= 1, ct_ref[0, h, pl.ds(i1, 1), :] - base, -NEG_BIG)
        ck2 = jnp.where(i >= 2, ct_ref[0, h, pl.ds(i2, 1), :] - base, -NEG_BIG)
        s0 = jnp.where(causal, _dot_nt(qm, k0_ref[0, :, ls]) + cq - ck0, NEG_BIG)
        s1 = _dot_nt(qm, k1_ref[0, :, ls]) + cq - ck1
        s2 = _dot_nt(qm, k2_ref[0, :, ls]) + cq - ck2
        m = jnp.maximum(jnp.maximum(jnp.max(s0, axis=1, keepdims=True), jnp.max(s1, axis=1, keepdims=True)),
                        jnp.max(s2, axis=1, keepdims=True))
        p0 = jnp.exp(s0 - m)
        p1 = jnp.exp(s1 - m)
        p2 = jnp.exp(s2 - m)
        l = (jnp.sum(p0, axis=1, keepdims=True) + jnp.sum(p1, axis=1, keepdims=True)
             + jnp.sum(p2, axis=1, keepdims=True))
        acc = (_dot(p0.astype(bf16), v0_ref[0, :, ls]) + _dot(p1.astype(bf16), v1_ref[0, :, ls])
               + _dot(p2.astype(bf16), v2_ref[0, :, ls]))
        m_s[h] = m
        l_s[h] = l
        acc_s[h] = acc

    n_far = jnp.maximum(i - 2 - start, 0)

    def copies(j, slot):
        rows = pl.ds(pl.multiple_of(j * T, T), T)
        return (pltpu.make_async_copy(kh_ref.at[b, rows, :], kbuf.at[slot], sem.at[0, slot]),
                pltpu.make_async_copy(vh_ref.at[b, rows, :], vbuf.at[slot], sem.at[1, slot]))

    @pl.when(n_far > 0)
    def _():
        for cp in copies(start, 0):
            cp.start()

        def body(t, carry):
            j = start + t
            slot = lax.rem(t, 2)
            for cp in copies(j, slot):
                cp.wait()

            @pl.when(t + 1 < n_far)
            def _():
                for cp in copies(j + 1, 1 - slot):
                    cp.start()

            for h in range(H):
                ls, qm = head_q(h)
                base, cq = head_c(h)
                ck = ct_ref[0, h, pl.ds(j, 1), :] - base
                s = _dot_nt(qm, kbuf[slot, :, ls]) + cq - ck
                m_prev = m_s[h]
                m_new = jnp.maximum(m_prev, jnp.max(s, axis=1, keepdims=True))
                alpha = jnp.exp(m_prev - m_new)
                p = jnp.exp(s - m_new)
                l_s[h] = alpha * l_s[h] + jnp.sum(p, axis=1, keepdims=True)
                acc_s[h] = alpha * acc_s[h] + _dot(p.astype(bf16), vbuf[slot, :, ls])
                m_s[h] = m_new
            return carry

        lax.fori_loop(0, n_far, body, 0)

    for p in range(H // 2):
        a0 = acc_s[2 * p] / l_s[2 * p]
        a1 = acc_s[2 * p + 1] / l_s[2 * p + 1]
        o_ref[0, :, p * LANES:(p + 1) * LANES] = jnp.where(first_half, a0, a1).astype(bf16)


def _fox_prompt(fq, fkb, fvb, c_rows, ct, start):
    B, L, _ = fq.shape
    T = ATT_BLOCK
    nq = L // T
    blk = lambda d: pl.BlockSpec((1, T, FOX_W), lambda b, i, s: (b, jnp.maximum(i - d, 0), 0))
    grid_spec = pltpu.PrefetchScalarGridSpec(
        num_scalar_prefetch=1,
        grid=(B, nq),
        in_specs=[blk(0),
                  pl.BlockSpec((1, T, FOX_HEADS), lambda b, i, s: (b, i, 0)),
                  blk(0), blk(1), blk(2), blk(0), blk(1), blk(2),
                  pl.BlockSpec((1, FOX_HEADS, nq, LANES), lambda b, i, s: (b, 0, 0, 0)),
                  pl.BlockSpec(memory_space=pl.ANY), pl.BlockSpec(memory_space=pl.ANY)],
        out_specs=pl.BlockSpec((1, T, FOX_W), lambda b, i, s: (b, i, 0)),
        scratch_shapes=[pltpu.VMEM((FOX_HEADS, T, 1), f32), pltpu.VMEM((FOX_HEADS, T, 1), f32),
                        pltpu.VMEM((FOX_HEADS, T, LANES), f32),
                        pltpu.VMEM((2, T, FOX_W), bf16), pltpu.VMEM((2, T, FOX_W), bf16),
                        pltpu.SemaphoreType.DMA((2, 2))],
    )
    return pl.pallas_call(
        _fox_prompt_kernel,
        grid_spec=grid_spec,
        out_shape=jax.ShapeDtypeStruct((B, L, FOX_W), bf16),
        compiler_params=pltpu.CompilerParams(dimension_semantics=("arbitrary", "arbitrary"),
                                             vmem_limit_bytes=VMEM_LIMIT),
        name="fox_prompt",
    )(start, fq, c_rows, fkb, fkb, fkb, fvb, fvb, fvb, ct, fkb, fvb)


def _fox_sample_kernel(q_ref, cq_ref, kn_ref, vn_ref, kp_ref, vp_ref, ct_ref, o_ref, *, past, lq):
    H = FOX_HEADS
    r_i = lax.broadcasted_iota(jnp.int32, (lq, lq), 0)
    c_i = lax.broadcasted_iota(jnp.int32, (lq, lq), 1)
    causal = c_i <= r_i
    lane = lax.broadcasted_iota(jnp.int32, (1, LANES), 1)
    first_half = lane < FOX_DH
    kp = kp_ref[0].astype(bf16)
    vp = vp_ref[0].astype(bf16)
    outs = []
    for h in range(H):
        ls = slice((h // 2) * LANES, (h // 2 + 1) * LANES)
        hm = first_half if h % 2 == 0 else jnp.logical_not(first_half)
        qp = q_ref[0, :, ls]
        qm = jnp.where(hm, qp, jnp.zeros_like(qp))
        base = ct_ref[0, h:h + 1, past:past + 1]
        cq = cq_ref[0, :, h:h + 1] - base
        s_p = _dot_nt(qm, kp[:, ls]) + cq - (ct_ref[0, h:h + 1, 0:past] - base)
        s_n = _dot_nt(qm, kn_ref[0, :, ls]) + cq - (ct_ref[0, h:h + 1, past:past + lq] - base)
        s_n = jnp.where(causal, s_n, NEG_BIG)
        m = jnp.maximum(jnp.max(s_p, axis=1, keepdims=True), jnp.max(s_n, axis=1, keepdims=True))
        p_p = jnp.exp(s_p - m)
        p_n = jnp.exp(s_n - m)
        l = jnp.sum(p_p, axis=1, keepdims=True) + jnp.sum(p_n, axis=1, keepdims=True)
        acc = _dot(p_p.astype(bf16), vp[:, ls]) + _dot(p_n.astype(bf16), vn_ref[0, :, ls])
        outs.append(acc / l)
    for p in range(H // 2):
        o_ref[0, :, p * LANES:(p + 1) * LANES] = jnp.where(first_half, outs[2 * p], outs[2 * p + 1]).astype(bf16)


def _fox_sample(fq, fkb, fvb, k_past, v_past, c_rows, ct_flat):
    B, lq, _ = fq.shape
    past = k_past.shape[1]
    lpad = ct_flat.shape[2]
    new = lambda w: pl.BlockSpec((1, lq, w), lambda b: (b, 0, 0))
    old = pl.BlockSpec((1, past, FOX_W), lambda b: (b, 0, 0))
    return pl.pallas_call(
        functools.partial(_fox_sample_kernel, past=past, lq=lq),
        grid=(B,),
        in_specs=[new(FOX_W), new(FOX_HEADS), new(FOX_W), new(FOX_W), old, old,
                  pl.BlockSpec((1, FOX_HEADS, lpad), lambda b: (b, 0, 0))],
        out_specs=new(FOX_W),
        out_shape=jax.ShapeDtypeStruct((B, lq, FOX_W), bf16),
        compiler_params=pltpu.CompilerParams(dimension_semantics=("arbitrary",), vmem_limit_bytes=VMEM_LIMIT),
        name="fox_sample",
    )(fq, c_rows, fkb, fvb, k_past, v_past, ct_flat)


def _layer_norm(t, g, b):
    mu = jnp.mean(t, axis=-1, keepdims=True)
    d = t - mu
    var = jnp.mean(d * d, axis=-1, keepdims=True)
    return d * lax.rsqrt(var + LN_EPS) * g + b


def _mix_kernel(x_ref, og_ref, of_ref, wz_ref, bz_ref, wpg_ref, wpf_ref, wo_ref, g1_ref, b1_ref, x1_ref, *, alpha):
    x = x_ref[...]
    xb = x.astype(bf16)
    zg = _dot(xb, wz_ref[:, :D_MODEL]) + bz_ref[:, :D_MODEL]
    zf = _dot(xb, wz_ref[:, D_MODEL:]) + bz_ref[:, D_MODEL:]
    m = jax.nn.sigmoid(zg) * _dot(og_ref[...], wpg_ref[...]) + jax.nn.sigmoid(zf) * _dot(of_ref[...], wpf_ref[...])
    t = alpha * x + _dot(m.astype(bf16), wo_ref[...])
    x1_ref[...] = _layer_norm(t, g1_ref[...], b1_ref[...])


def _ffn_kernel(x1_ref, wg_ref, wu_ref, wd_ref, g2_ref, b2_ref, y_ref, *, alpha, nchunk):
    x1 = x1_ref[...]
    xb = x1.astype(bf16)
    w = D_FF // nchunk
    acc = None
    for c in range(nchunk):
        cs = slice(c * w, (c + 1) * w)
        g = _dot(xb, wg_ref[:, cs])
        u = _dot(xb, wu_ref[:, cs])
        hid = (g * jax.nn.sigmoid(g) * u).astype(bf16)
        part = _dot(hid, wd_ref[cs, :])
        acc = part if acc is None else acc + part
    y_ref[...] = _layer_norm(alpha * x1 + acc, g2_ref[...], b2_ref[...])


def _mix(x2d, og, of, wz, bz, wpg, wpf, wo, g1, b1, alpha):
    n = x2d.shape[0]
    tm = min(512, n)
    row = lambda w: pl.BlockSpec((tm, w), lambda i: (i, 0))
    return pl.pallas_call(
        functools.partial(_mix_kernel, alpha=alpha),
        grid=(n // tm,),
        in_specs=[row(D_MODEL), row(GLA_V), row(FOX_W)] + [_const_spec(a.shape) for a in (wz, bz, wpg, wpf, wo, g1, b1)],
        out_specs=row(D_MODEL),
        out_shape=jax.ShapeDtypeStruct((n, D_MODEL), f32),
        compiler_params=pltpu.CompilerParams(dimension_semantics=("arbitrary",), vmem_limit_bytes=VMEM_LIMIT),
        name="mix",
    )(x2d, og, of, wz, bz, wpg, wpf, wo, g1, b1)


def _ffn(x1, wg, wu, wd, g2, b2, alpha):
    n = x1.shape[0]
    tm = min(512, n)
    row = lambda w: pl.BlockSpec((tm, w), lambda i: (i, 0))
    return pl.pallas_call(
        functools.partial(_ffn_kernel, alpha=alpha, nchunk=2),
        grid=(n // tm,),
        in_specs=[row(D_MODEL)] + [_const_spec(a.shape) for a in (wg, wu, wd, g2, b2)],
        out_specs=row(D_MODEL),
        out_shape=jax.ShapeDtypeStruct((n, D_MODEL), f32),
        compiler_params=pltpu.CompilerParams(dimension_semantics=("arbitrary",), vmem_limit_bytes=VMEM_LIMIT),
        name="ffn",
    )(x1, wg, wu, wd, g2, b2)


def _prep_weights(w_in, b_in, w_alpha2, b_alpha2, gla_norm_g, w_proj_gla, w_proj_fox, w_out,
                  ln1_g, ln1_b, w_ffn_gate, w_ffn_up, w_ffn_down, ln2_g, ln2_b):
    def cols(a, idx):
        return a[..., _OFF[idx]:_OFF[idx + 1]]

    order = (0, 1, 2, 4, 5, 6, 7, 8, 3)
    pad = SMALL_W - FOX_HEADS - GLA_RANK
    w1 = jnp.concatenate([cols(w_in, i) for i in order] + [jnp.zeros((D_MODEL, pad), f32)], axis=1).astype(bf16)
    b1 = jnp.concatenate([cols(b_in, i) for i in order] + [jnp.zeros((pad,), f32)])[None, :]
    wa = jnp.zeros((SMALL_W, GLA_QK), f32).at[FOX_HEADS:FOX_HEADS + GLA_RANK].set(w_alpha2).astype(bf16)
    head_of = np.arange(FOX_W) // FOX_DH
    indq = np.zeros((FOX_W, SMALL_W), np.float32)
    indq[np.arange(FOX_W), FOX_HEADS + head_of] = 1.0
    indk = np.zeros((FOX_W, SMALL_W), np.float32)
    indk[np.arange(FOX_W), 2 * FOX_HEADS + head_of] = 1.0
    row = lambda a: a[None, :].astype(f32)
    return dict(
        w1=w1, b1=b1, wa=wa, ba=row(b_alpha2), indq=jnp.asarray(indq, bf16), indk=jnp.asarray(indk, bf16),
        g=row(gla_norm_g),
        wz=w_in[:, _OFF[9]:].astype(bf16), bz=row(b_in[_OFF[9]:]),
        wpg=w_proj_gla.astype(bf16), wpf=w_proj_fox.astype(bf16), wo=w_out.astype(bf16),
        g1=row(ln1_g), b1n=row(ln1_b),
        wg=w_ffn_gate.astype(bf16), wu=w_ffn_up.astype(bf16), wd=w_ffn_down.astype(bf16),
        g2=row(ln2_g), b2n=row(ln2_b),
    )


def _time_on_lanes(a, lpad):
    B, L, W = a.shape
    a = jnp.transpose(a, (0, 2, 1))
    return jnp.pad(a, ((0, 0), (0, 0), (0, lpad - L)))


def _layer(x, s0, k_past, v_past, lf_past, p, alpha):
    B, L, _ = x.shape
    n = B * L
    x2d = x.reshape(n, D_MODEL)
    gq, gk, gv, la, sg, fq, fk, fv, fkb, fvb, aux = _inproj(x2d, p["w1"], p["b1"], p["wa"], p["ba"], p["indq"], p["indk"])
    r3 = lambda a: a.reshape(B, L, a.shape[-1])

    o_gla, st = _gla(r3(gq), r3(gk), r3(gv), r3(la), r3(sg), p["g"], _state_to_pairs(s0))
    gla_state = _pairs_to_state(st)

    aux3 = r3(aux)
    lf = aux3[:, :, :FOX_HEADS]
    if k_past is None:
        nc = L // LANES
        aux_t = _time_on_lanes(aux3[:, :, :3 * FOX_HEADS], L).reshape(B, 3 * FOX_HEADS, nc, LANES)
        ct, start = _fox_meta(aux_t)
        c_rows = jnp.transpose(ct.reshape(B, FOX_HEADS, L), (0, 2, 1))
        o_fox = _fox_prompt(r3(fq), r3(fkb), r3(fvb), c_rows, ct, start.reshape(B, nc))
    else:
        past = k_past.shape[1]
        lk = past + L
        lpad = -(-lk // (8 * LANES)) * (8 * LANES)
        lf_all = jnp.concatenate([lf_past.astype(f32), lf], axis=1)
        stats = jnp.concatenate([lf_all, jnp.zeros((B, lk, 2 * FOX_HEADS), f32)], axis=2)
        aux_t = _time_on_lanes(stats, lpad).reshape(B, 3 * FOX_HEADS, lpad // LANES, LANES)
        ct, _ = _fox_meta(aux_t)
        ct_flat = ct.reshape(B, FOX_HEADS, lpad)
        c_rows = jnp.transpose(ct_flat[:, :, past:lk], (0, 2, 1))
        o_fox = _fox_sample(r3(fq), r3(fkb), r3(fvb), k_past.reshape(B, past, FOX_W), v_past.reshape(B, past, FOX_W),
                            c_rows, ct_flat)

    x1 = _mix(x2d, o_gla.reshape(n, GLA_V), o_fox.reshape(n, FOX_W), p["wz"], p["bz"], p["wpg"], p["wpf"], p["wo"],
              p["g1"], p["b1n"], alpha)
    y = _ffn(x1, p["wg"], p["wu"], p["wd"], p["g2"], p["b2n"], alpha)
    return (y.reshape(B, L, D_MODEL), gla_state,
            fk.reshape(B, L, FOX_HEADS, FOX_DH), fv.reshape(B, L, FOX_HEADS, FOX_DH), lf)


def kernel(x_prompt, x_sample, state_gla, cache_fox_k, cache_fox_v, cache_fox_logf, w_in, b_in, w_alpha2, b_alpha2,
           gla_norm_g, w_proj_gla, w_proj_fox, w_out, ln1_g, ln1_b, w_ffn_gate, w_ffn_up, w_ffn_down, ln2_g, ln2_b):
    depth = w_in.shape[0]
    alpha = (2.0 * depth) ** 0.25
    yp, ys = x_prompt, x_sample
    outs_p, outs_s = [], []
    for l in range(depth):
        p = _prep_weights(w_in[l], b_in[l], w_alpha2[l], b_alpha2[l], gla_norm_g[l], w_proj_gla[l], w_proj_fox[l],
                          w_out[l], ln1_g[l], ln1_b[l], w_ffn_gate[l], w_ffn_up[l], w_ffn_down[l], ln2_g[l], ln2_b[l])
        s0 = jnp.zeros((yp.shape[0], GLA_HEADS, GLA_DK, GLA_DV), f32)
        yp, *rest_p = _layer(yp, s0, None, None, None, p, alpha)
        outs_p.append(rest_p)
        ys, *rest_s = _layer(ys, state_gla[l], cache_fox_k[l], cache_fox_v[l], cache_fox_logf[l], p, alpha)
        outs_s.append(rest_s)
    stack = lambda outs, i: jnp.stack([o[i] for o in outs])
    return (yp, ys,
            stack(outs_p, 0), stack(outs_p, 1), stack(outs_p, 2), stack(outs_p, 3),
            stack(outs_s, 0), stack(outs_s, 1), stack(outs_s, 2), stack(outs_s, 3))
```

```python
import functools

import jax
import jax.numpy as jnp
import numpy as np
from jax import lax
from jax.experimental import pallas as pl
from jax.experimental.pallas import tpu as pltpu

f32 = jnp.float32
bf16 = jnp.bfloat16

D_MODEL = 1024
GLA_HEADS = 4
GLA_DK = 64
GLA_DV = 128
GLA_RANK = 16
GLA_TAU = 16.0
GLA_QK = GLA_HEADS * GLA_DK
GLA_V = GLA_HEADS * GLA_DV
GLA_CHUNK = 64
FOX_HEADS = 8
FOX_DH = 64
FOX_W = FOX_HEADS * FOX_DH
D_FF = 2816
LN_EPS = 1e-5
RMS_EPS = 1e-5
SPLIT_SIZES = (GLA_QK, GLA_QK, GLA_V, GLA_RANK, GLA_V, FOX_W, FOX_W, FOX_W, FOX_HEADS, D_MODEL, D_MODEL)
_OFF = np.concatenate([[0], np.cumsum(SPLIT_SIZES)]).astype(int)

LANES = 128
SMALL_W = LANES
MAIN_W = 2 * GLA_QK + 2 * GLA_V + 3 * FOX_W
ATT_BLOCK = 128
LOG2E = 1.4426950408889634
SKIP_T2 = 110.0 * LOG2E
NORM_SLACK = 1.02
NEG_BIG = -1e30
VMEM_LIMIT = 56 * 1024 * 1024


def _dot(a, b):
    return jnp.dot(a, b, preferred_element_type=f32)


def _dot_nt(a, b):
    return lax.dot_general(a, b, (((1,), (1,)), ((), ())), preferred_element_type=f32)


def _dot_tn(a, b):
    return lax.dot_general(a, b, (((0,), (0,)), ((), ())), preferred_element_type=f32)


def _dot_hi(a, b):
    return jnp.dot(a, b, preferred_element_type=f32, precision=lax.Precision.HIGHEST)


def _log_sigmoid(x):
    return jnp.minimum(x, 0.0) - jnp.log1p(jnp.exp(-jnp.abs(x)))


def _const_spec(shape):
    return pl.BlockSpec(shape, lambda *_: (0,) * len(shape))


def _inproj_kernel(x_ref, w_ref, b_ref, wa_ref, ba_ref, indq_ref, indk_ref,
                   gq_ref, gk_ref, gv_ref, la_ref, sg_ref, fq_ref, fk_ref, fv_ref,
                   fkb_ref, fvb_ref, aux_ref):
    xb = x_ref[...].astype(bf16)

    def group(lo, width):
        return _dot(xb, w_ref[:, lo:lo + width]) + b_ref[:, lo:lo + width]

    o = 0
    gq_ref[...] = (group(o, GLA_QK) * (GLA_DK ** -0.5)).astype(bf16)
    o += GLA_QK
    gk_ref[...] = group(o, GLA_QK).astype(bf16)
    o += GLA_QK
    gv_ref[...] = group(o, GLA_V).astype(bf16)
    o += GLA_V
    gr = group(o, GLA_V)
    sg_ref[...] = (gr * jax.nn.sigmoid(gr)).astype(bf16)
    o += GLA_V
    fq = (group(o, FOX_W) * (FOX_DH ** -0.5 * LOG2E)).astype(bf16)
    fq_ref[...] = fq
    o += FOX_W
    tm = x_ref.shape[0]

    def store_heads(ref, val):
        for h in range(FOX_HEADS):
            ref[pl.ds(h, tm, stride=FOX_HEADS), :] = val[:, h * FOX_DH:(h + 1) * FOX_DH]

    fk = group(o, FOX_W)
    store_heads(fk_ref, fk)
    fkb = fk.astype(bf16)
    fkb_ref[...] = fkb
    o += FOX_W
    fv = group(o, FOX_W)
    store_heads(fv_ref, fv)
    fvb_ref[...] = fv.astype(bf16)
    o += FOX_W
    small = group(o, SMALL_W)
    la_pre = _dot(small.astype(bf16), wa_ref[...]) + ba_ref[...]
    la_ref[...] = _log_sigmoid(la_pre) * (1.0 / GLA_TAU)
    lane = lax.broadcasted_iota(jnp.int32, (1, SMALL_W), 1)
    lf = jnp.where(lane < FOX_HEADS, _log_sigmoid(small), 0.0)
    fq32 = fq.astype(f32)
    fk32 = fkb.astype(f32)
    nq2 = _dot((fq32 * fq32).astype(bf16), indq_ref[...])
    nk2 = _dot((fk32 * fk32).astype(bf16), indk_ref[...])
    aux_ref[...] = lf + nq2 + nk2


def _inproj(x2d, w1, b1, wa, ba, indq, indk):
    n = x2d.shape[0]
    tm = min(512, n)
    assert n % tm == 0
    row = lambda w: pl.BlockSpec((tm, w), lambda i: (i, 0))
    out_shapes = (
        jax.ShapeDtypeStruct((n, GLA_QK), bf16),
        jax.ShapeDtypeStruct((n, GLA_QK), bf16),
        jax.ShapeDtypeStruct((n, GLA_V), bf16),
        jax.ShapeDtypeStruct((n, GLA_QK), f32),
        jax.ShapeDtypeStruct((n, GLA_V), bf16),
        jax.ShapeDtypeStruct((n, FOX_W), bf16),
        jax.ShapeDtypeStruct((n * FOX_HEADS, FOX_DH), f32),
        jax.ShapeDtypeStruct((n * FOX_HEADS, FOX_DH), f32),
        jax.ShapeDtypeStruct((n, FOX_W), bf16),
        jax.ShapeDtypeStruct((n, FOX_W), bf16),
        jax.ShapeDtypeStruct((n, SMALL_W), f32),
    )
    return pl.pallas_call(
        _inproj_kernel,
        grid=(n // tm,),
        in_specs=[row(D_MODEL), _const_spec(w1.shape), _const_spec(b1.shape), _const_spec(wa.shape),
                  _const_spec(ba.shape), _const_spec(indq.shape), _const_spec(indk.shape)],
        out_specs=[pl.BlockSpec((tm * s.shape[0] // n, s.shape[1]), lambda i: (i, 0)) for s in out_shapes],
        out_shape=out_shapes,
        compiler_params=pltpu.CompilerParams(dimension_semantics=("arbitrary",), vmem_limit_bytes=VMEM_LIMIT),
        name="inproj",
    )(x2d, w1, b1, wa, ba, indq, indk)


def _gla_kernel(q_ref, k_ref, v_ref, la_ref, sg_ref, g_ref, s0_ref, o_ref, sfin_ref, st_ref, *, chunk, nsub):
    t = pl.program_id(1)

    @pl.when(t == 0)
    def _():
        st_ref[...] = s0_ref[0]

    C = chunk
    tl = C * nsub
    r_i = lax.broadcasted_iota(jnp.int32, (C, C), 0)
    c_i = lax.broadcasted_iota(jnp.int32, (C, C), 1)
    causal = c_i <= r_i
    rt = lax.broadcasted_iota(jnp.int32, (tl, tl), 0)
    ct = lax.broadcasted_iota(jnp.int32, (tl, tl), 1)
    chunk_tril = ((ct <= rt) & (ct >= (rt // C) * C)).astype(bf16)
    lane = lax.broadcasted_iota(jnp.int32, (1, LANES), 1)
    first_half = lane < GLA_DK
    mid = C // 2
    pairs = range(GLA_HEADS // 2)

    def lanes(p):
        return slice(p * LANES, (p + 1) * LANES)

    def one_head(x, hh):
        return jnp.where(first_half if hh == 0 else jnp.logical_not(first_half), x, jnp.zeros_like(x))

    la = la_ref[0]
    la_hi = la.astype(bf16)
    la_lo = (la - la_hi.astype(f32)).astype(bf16)
    bcum_all = _dot(chunk_tril, la_hi) + _dot(chunk_tril, la_lo)

    q_in, dec, amat, upd = [], [], {}, {}
    for c in range(nsub):
        rows = slice(c * C, (c + 1) * C)
        bcum = bcum_all[rows]
        b_last = bcum[C - 1:C, :]
        ref = bcum[mid:mid + 1, :]
        q = q_ref[0, rows, :].astype(f32)
        k = k_ref[0, rows, :].astype(f32)
        q_in.append((q * jnp.exp(bcum)).astype(bf16))
        q_a = (q * jnp.exp(jnp.clip(bcum - ref, -80.0, 80.0))).astype(bf16)
        k_a = (k * jnp.exp(jnp.clip(ref - bcum, -80.0, 80.0))).astype(bf16)
        k_d = (k * jnp.exp(b_last - bcum)).astype(bf16)
        dec.append(jnp.exp(b_last))
        for p in pairs:
            for hh in range(2):
                h = 2 * p + hh
                v_h = v_ref[0, rows, h * GLA_DV:(h + 1) * GLA_DV]
                a = _dot_nt(one_head(q_a[:, lanes(p)], hh), k_a[:, lanes(p)])
                amat[c, h] = jnp.where(causal, a, 0.0).astype(bf16)
                upd[c, h] = _dot_tn(v_h, k_d[:, lanes(p)])

    states = []
    st = [st_ref[p] for p in pairs]
    for c in range(nsub):
        states.append([s.astype(bf16) for s in st])
        st = [dec[c][:, lanes(p)] * st[p] + jnp.where(first_half, upd[c, 2 * p], upd[c, 2 * p + 1]) for p in pairs]
    for p in pairs:
        st_ref[p] = st[p]

    for c in range(nsub):
        rows = slice(c * C, (c + 1) * C)
        for h in range(GLA_HEADS):
            p, hh = divmod(h, 2)
            vs = slice(h * GLA_DV, (h + 1) * GLA_DV)
            o = _dot(amat[c, h], v_ref[0, rows, vs]) + _dot_nt(one_head(q_in[c][:, lanes(p)], hh), states[c][p])
            o = o * lax.rsqrt(jnp.mean(o * o, axis=-1, keepdims=True) + RMS_EPS)
            o = o * g_ref[:, vs] * sg_ref[0, rows, vs].astype(f32)
            o_ref[0, rows, vs] = o.astype(bf16)

    @pl.when(t == pl.num_programs(1) - 1)
    def _():
        sfin_ref[0] = st_ref[...]


def _gla(gq, gk, gv, la, sg, g, s0t):
    B, L, _ = gq.shape
    C = min(GLA_CHUNK, L)
    tl = min(256, L)
    assert L % tl == 0 and tl % C == 0
    tok = lambda w: pl.BlockSpec((1, tl, w), lambda b, t: (b, t, 0))
    st_spec = pl.BlockSpec((1, 2, GLA_DV, LANES), lambda b, t: (b, 0, 0, 0))
    return pl.pallas_call(
        functools.partial(_gla_kernel, chunk=C, nsub=tl // C),
        grid=(B, L // tl),
        in_specs=[tok(GLA_QK), tok(GLA_QK), tok(GLA_V), tok(GLA_QK), tok(GLA_V),
                  pl.BlockSpec((1, GLA_V), lambda b, t: (0, 0)), st_spec],
        out_specs=[tok(GLA_V), st_spec],
        out_shape=(jax.ShapeDtypeStruct((B, L, GLA_V), bf16),
                   jax.ShapeDtypeStruct((B, 2, GLA_DV, LANES), f32)),
        scratch_shapes=[pltpu.VMEM((2, GLA_DV, LANES), f32)],
        compiler_params=pltpu.CompilerParams(dimension_semantics=("arbitrary", "arbitrary"),
                                             vmem_limit_bytes=VMEM_LIMIT),
        name="gla_scan",
    )(gq, gk, gv, la, sg, g, s0t)


def _state_to_pairs(s):
    B = s.shape[0]
    s = s.reshape(B, 2, 2, GLA_DK, GLA_DV)
    return jnp.transpose(s, (0, 1, 4, 2, 3)).reshape(B, 2, GLA_DV, 2 * GLA_DK)


def _pairs_to_state(st):
    B = st.shape[0]
    st = st.reshape(B, 2, GLA_DV, 2, GLA_DK)
    return jnp.transpose(st, (0, 1, 3, 4, 2)).reshape(B, GLA_HEADS, GLA_DK, GLA_DV)


def _meta_kernel(a_ref, ct_ref, start_ref, *, nc):
    H = FOX_HEADS
    r_i = lax.broadcasted_iota(jnp.int32, (LANES, LANES), 0)
    c_i = lax.broadcasted_iota(jnp.int32, (LANES, LANES), 1)
    upper = (r_i <= c_i).astype(f32)
    rj = lax.broadcasted_iota(jnp.int32, (nc, nc), 0)
    cj = lax.broadcasted_iota(jnp.int32, (nc, nc), 1)
    strict_lower = (cj < rj).astype(f32)
    eye = rj == cj

    def to_row(col):
        return jnp.sum(jnp.where(eye, jnp.broadcast_to(col, (nc, nc)), 0.0), axis=0, keepdims=True)

    start = None
    for h in range(H):
        y = _dot_hi(a_ref[0, h], upper)
        tot = jnp.broadcast_to(y[:, LANES - 1:LANES], (nc, LANES))
        c = (y + _dot_hi(strict_lower, tot)) * LOG2E
        ct_ref[0, h] = c
        cmax = jnp.max(c, axis=1, keepdims=True)
        cmin = jnp.min(c, axis=1, keepdims=True)
        qmax = jnp.sqrt(jnp.max(a_ref[0, H + h], axis=1, keepdims=True))
        kmax = jnp.sqrt(jnp.max(jnp.max(a_ref[0, 2 * H + h], axis=1, keepdims=True), axis=0, keepdims=True))
        thr = cmax + (2.0 * NORM_SLACK) * qmax * kmax + SKIP_T2
        needed = (to_row(cmin) <= thr) & (cj <= rj)
        first = jnp.min(jnp.where(needed, cj, nc), axis=1, keepdims=True)
        start = first if start is None else jnp.minimum(start, first)
    start_ref[0] = to_row(start.astype(f32)).astype(jnp.int32)


def _fox_meta(aux_t):
    B, _, nc, _ = aux_t.shape
    return pl.pallas_call(
        functools.partial(_meta_kernel, nc=nc),
        grid=(B,),
        in_specs=[pl.BlockSpec((1, 3 * FOX_HEADS, nc, LANES), lambda b: (b, 0, 0, 0))],
        out_specs=[pl.BlockSpec((1, FOX_HEADS, nc, LANES), lambda b: (b, 0, 0, 0)),
                   pl.BlockSpec((1, 1, nc), lambda b: (b, 0, 0))],
        out_shape=(jax.ShapeDtypeStruct((B, FOX_HEADS, nc, LANES), f32),
                   jax.ShapeDtypeStruct((B, 1, nc), jnp.int32)),
        compiler_params=pltpu.CompilerParams(dimension_semantics=("arbitrary",), vmem_limit_bytes=VMEM_LIMIT),
        name="fox_meta",
    )(aux_t)


def _fox_prompt_kernel(start_ref, q_ref, k0_ref, k1_ref, k2_ref, v0_ref, v1_ref, v2_ref,
                       ct_ref, kh_ref, vh_ref, o_ref, m_s, acc_s, kbuf, vbuf, sem):
    T = ATT_BLOCK
    H = FOX_HEADS
    b = pl.program_id(0)
    i = pl.program_id(1)
    start = start_ref[b, i]
    r_i = lax.broadcasted_iota(jnp.int32, (T, T), 0)
    c_i = lax.broadcasted_iota(jnp.int32, (T, T), 1)
    causal = c_i <= r_i
    lane = lax.broadcasted_iota(jnp.int32, (1, LANES), 1)
    first_half = lane < FOX_DH
    i1 = jnp.maximum(i - 1, 0)
    i2 = jnp.maximum(i - 2, 0)

    def head_q(h):
        ls = slice((h // 2) * LANES, (h // 2 + 1) * LANES)
        hm = first_half if h % 2 == 0 else jnp.logical_not(first_half)
        qp = q_ref[0, :, ls]
        return ls, jnp.where(hm, qp, jnp.zeros_like(qp))

    def v_aug(vp, h):
        one = jnp.ones_like(vp)
        return jnp.where(first_half, vp, one) if h % 2 == 0 else jnp.where(first_half, one, vp)

    def head_base(h):
        return ct_ref[0, h, pl.ds(i, 1), 0:1]

    scores = []
    for h in range(H):
        ls, qm = head_q(h)
        base = head_base(h)
        ck0 = ct_ref[0, h, pl.ds(i, 1), :] - base
        ck1 = jnp.where(i >= 1, ct_ref[0, h, pl.ds(i1, 1), :] - base, -NEG_BIG)
        ck2 = jnp.where(i >= 2, ct_ref[0, h, pl.ds(i2, 1), :] - base, -NEG_BIG)
        s0 = jnp.where(causal, _dot_nt(qm, k0_ref[0, :, ls]) - ck0, NEG_BIG)
        s1 = _dot_nt(qm, k1_ref[0, :, ls]) - ck1
        s2 = _dot_nt(qm, k2_ref[0, :, ls]) - ck2
        m = jnp.max(jnp.maximum(jnp.maximum(s0, s1), s2), axis=1, keepdims=True)
        m_s[h] = m
        scores.append((s0, s1, s2, m))
    for h in range(H):
        ls = slice((h // 2) * LANES, (h // 2 + 1) * LANES)
        s0, s1, s2, m = scores[h]
        acc_s[h] = (_dot(jnp.exp2(s0 - m).astype(bf16), v_aug(v0_ref[0, :, ls], h))
                    + _dot(jnp.exp2(s1 - m).astype(bf16), v_aug(v1_ref[0, :, ls], h))
                    + _dot(jnp.exp2(s2 - m).astype(bf16), v_aug(v2_ref[0, :, ls], h)))

    n_far = jnp.maximum(i - 2 - start, 0)

    def copies(j, slot):
        rows = pl.ds(pl.multiple_of(j * T, T), T)
        return (pltpu.make_async_copy(kh_ref.at[b, rows, :], kbuf.at[slot], sem.at[0, slot]),
                pltpu.make_async_copy(vh_ref.at[b, rows, :], vbuf.at[slot], sem.at[1, slot]))

    @pl.when(n_far > 0)
    def _():
        for cp in copies(start, 0):
            cp.start()

        def body(t, carry):
            j = start + t
            slot = lax.rem(t, 2)
            for cp in copies(j, slot):
                cp.wait()

            @pl.when(t + 1 < n_far)
            def _():
                for cp in copies(j + 1, 1 - slot):
                    cp.start()

            for h in range(H):
                ls, qm = head_q(h)
                ck = ct_ref[0, h, pl.ds(j, 1), :] - head_base(h)
                s = _dot_nt(qm, kbuf[slot, :, ls]) - ck
                m_prev = m_s[h]
                m_new = jnp.maximum(m_prev, jnp.max(s, axis=1, keepdims=True))
                alpha = jnp.exp2(m_prev - m_new)
                p = jnp.exp2(s - m_new)
                acc_s[h] = alpha * acc_s[h] + _dot(p.astype(bf16), v_aug(vbuf[slot, :, ls], h))
                m_s[h] = m_new
            return carry

        lax.fori_loop(0, n_far, body, 0)

    for p in range(H // 2):
        even = acc_s[2 * p]
        odd = acc_s[2 * p + 1]
        numer = jnp.where(first_half, even, odd)
        denom = pltpu.roll(jnp.where(first_half, odd, even), FOX_DH, 1)
        o_ref[0, :, p * LANES:(p + 1) * LANES] = (numer / denom).astype(bf16)


def _fox_prompt(fq, fkb, fvb, ct, start):
    B, L, _ = fq.shape
    T = ATT_BLOCK
    nq = L // T
    blk = lambda d: pl.BlockSpec((1, T, FOX_W), lambda b, i, s: (b, jnp.maximum(i - d, 0), 0))
    grid_spec = pltpu.PrefetchScalarGridSpec(
        num_scalar_prefetch=1,
        grid=(B, nq),
        in_specs=[blk(0),
                  blk(0), blk(1), blk(2), blk(0), blk(1), blk(2),
                  pl.BlockSpec((1, FOX_HEADS, nq, LANES), lambda b, i, s: (b, 0, 0, 0)),
                  pl.BlockSpec(memory_space=pl.ANY), pl.BlockSpec(memory_space=pl.ANY)],
        out_specs=pl.BlockSpec((1, T, FOX_W), lambda b, i, s: (b, i, 0)),
        scratch_shapes=[pltpu.VMEM((FOX_HEADS, T, 1), f32),
                        pltpu.VMEM((FOX_HEADS, T, LANES), f32),
                        pltpu.VMEM((2, T, FOX_W), bf16), pltpu.VMEM((2, T, FOX_W), bf16),
                        pltpu.SemaphoreType.DMA((2, 2))],
    )
    return pl.pallas_call(
        _fox_prompt_kernel,
        grid_spec=grid_spec,
        out_shape=jax.ShapeDtypeStruct((B, L, FOX_W), bf16),
        compiler_params=pltpu.CompilerParams(dimension_semantics=("arbitrary", "arbitrary"),
                                             vmem_limit_bytes=VMEM_LIMIT),
        name="fox_prompt",
    )(start, fq, fkb, fkb, fkb, fvb, fvb, fvb, ct, fkb, fvb)


def _fox_sample_kernel(q_ref, kn_ref, vn_ref, kp_ref, vp_ref, ct_ref, o_ref, *, past, lq):
    H = FOX_HEADS
    r_i = lax.broadcasted_iota(jnp.int32, (lq, lq), 0)
    c_i = lax.broadcasted_iota(jnp.int32, (lq, lq), 1)
    causal = c_i <= r_i
    heads = lambda h: slice(h * FOX_DH, (h + 1) * FOX_DH)
    probs = []
    for h in range(H):
        q_h = q_ref[0, :, heads(h)]
        base = ct_ref[0, h:h + 1, past:past + 1]
        kp = kp_ref[0, pl.ds(h, past, stride=H), :].astype(bf16)
        s_p = _dot_nt(q_h, kp) - (ct_ref[0, h:h + 1, 0:past] - base)
        s_n = _dot_nt(q_h, kn_ref[0, :, heads(h)]) - (ct_ref[0, h:h + 1, past:past + lq] - base)
        s_n = jnp.where(causal, s_n, NEG_BIG)
        m = jnp.maximum(jnp.max(s_p, axis=1, keepdims=True), jnp.max(s_n, axis=1, keepdims=True))
        p_p = jnp.exp2(s_p - m)
        p_n = jnp.exp2(s_n - m)
        l = jnp.sum(p_p, axis=1, keepdims=True) + jnp.sum(p_n, axis=1, keepdims=True)
        probs.append((p_p.astype(bf16), p_n.astype(bf16), l))
    outs = []
    for h in range(H):
        p_p, p_n, l = probs[h]
        vp = vp_ref[0, pl.ds(h, past, stride=H), :].astype(bf16)
        outs.append((_dot(p_p, vp) + _dot(p_n, vn_ref[0, :, heads(h)])) / l)
    o_ref[0] = jnp.concatenate(outs, axis=1).astype(bf16)


def _fox_sample(fq, fkb, fvb, k_past, v_past, ct_flat):
    B, lq, _ = fq.shape
    past = k_past.shape[1] // FOX_HEADS
    lpad = ct_flat.shape[2]
    new = lambda w: pl.BlockSpec((1, lq, w), lambda b: (b, 0, 0))
    old = pl.BlockSpec((1, past * FOX_HEADS, FOX_DH), lambda b: (b, 0, 0))
    return pl.pallas_call(
        functools.partial(_fox_sample_kernel, past=past, lq=lq),
        grid=(B,),
        in_specs=[new(FOX_W), new(FOX_W), new(FOX_W), old, old,
                  pl.BlockSpec((1, FOX_HEADS, lpad), lambda b: (b, 0, 0))],
        out_specs=new(FOX_W),
        out_shape=jax.ShapeDtypeStruct((B, lq, FOX_W), bf16),
        compiler_params=pltpu.CompilerParams(dimension_semantics=("arbitrary",), vmem_limit_bytes=VMEM_LIMIT),
        name="fox_sample",
    )(fq, fkb, fvb, k_past, v_past, ct_flat)


def _layer_norm(t, g, b):
    mu = jnp.mean(t, axis=-1, keepdims=True)
    d = t - mu
    var = jnp.mean(d * d, axis=-1, keepdims=True)
    return d * lax.rsqrt(var + LN_EPS) * g + b


def _mix_kernel(x_ref, og_ref, of_ref, wz_ref, bz_ref, wpg_ref, wpf_ref, wo_ref, g1_ref, b1_ref, x1_ref, *, alpha):
    x = x_ref[...]
    xb = x.astype(bf16)
    zg = _dot(xb, wz_ref[:, :D_MODEL]) + bz_ref[:, :D_MODEL]
    zf = _dot(xb, wz_ref[:, D_MODEL:]) + bz_ref[:, D_MODEL:]
    m = jax.nn.sigmoid(zg) * _dot(og_ref[...], wpg_ref[...]) + jax.nn.sigmoid(zf) * _dot(of_ref[...], wpf_ref[...])
    t = alpha * x + _dot(m.astype(bf16), wo_ref[...])
    x1_ref[...] = _layer_norm(t, g1_ref[...], b1_ref[...])


def _ffn_kernel(x1_ref, wg_ref, wu_ref, wd_ref, g2_ref, b2_ref, y_ref, *, alpha, nchunk):
    x1 = x1_ref[...]
    xb = x1.astype(bf16)
    w = D_FF // nchunk
    acc = None
    for c in range(nchunk):
        cs = slice(c * w, (c + 1) * w)
        g = _dot(xb, wg_ref[:, cs])
        u = _dot(xb, wu_ref[:, cs])
        hid = (g * jax.nn.sigmoid(g) * u).astype(bf16)
        part = _dot(hid, wd_ref[cs, :])
        acc = part if acc is None else acc + part
    y_ref[...] = _layer_norm(alpha * x1 + acc, g2_ref[...], b2_ref[...])


def _mix(x2d, og, of, wz, bz, wpg, wpf, wo, g1, b1, alpha):
    n = x2d.shape[0]
    tm = min(512, n)
    row = lambda w: pl.BlockSpec((tm, w), lambda i: (i, 0))
    return pl.pallas_call(
        functools.partial(_mix_kernel, alpha=alpha),
        grid=(n // tm,),
        in_specs=[row(D_MODEL), row(GLA_V), row(FOX_W)] + [_const_spec(a.shape) for a in (wz, bz, wpg, wpf, wo, g1, b1)],
        out_specs=row(D_MODEL),
        out_shape=jax.ShapeDtypeStruct((n, D_MODEL), f32),
        compiler_params=pltpu.CompilerParams(dimension_semantics=("arbitrary",), vmem_limit_bytes=VMEM_LIMIT),
        name="mix",
    )(x2d, og, of, wz, bz, wpg, wpf, wo, g1, b1)


def _ffn(x1, wg, wu, wd, g2, b2, alpha):
    n = x1.shape[0]
    tm = min(512, n)
    row = lambda w: pl.BlockSpec((tm, w), lambda i: (i, 0))
    return pl.pallas_call(
        functools.partial(_ffn_kernel, alpha=alpha, nchunk=2),
        grid=(n // tm,),
        in_specs=[row(D_MODEL)] + [_const_spec(a.shape) for a in (wg, wu, wd, g2, b2)],
        out_specs=row(D_MODEL),
        out_shape=jax.ShapeDtypeStruct((n, D_MODEL), f32),
        compiler_params=pltpu.CompilerParams(dimension_semantics=("arbitrary",), vmem_limit_bytes=VMEM_LIMIT),
        name="ffn",
    )(x1, wg, wu, wd, g2, b2)


def _prep_weights(w_in, b_in, w_alpha2, b_alpha2, gla_norm_g, w_proj_gla, w_proj_fox, w_out,
                  ln1_g, ln1_b, w_ffn_gate, w_ffn_up, w_ffn_down, ln2_g, ln2_b):
    def cols(a, idx):
        return a[..., _OFF[idx]:_OFF[idx + 1]]

    order = (0, 1, 2, 4, 5, 6, 7, 8, 3)
    pad = SMALL_W - FOX_HEADS - GLA_RANK
    w1 = jnp.concatenate([cols(w_in, i) for i in order] + [jnp.zeros((D_MODEL, pad), f32)], axis=1).astype(bf16)
    b1 = jnp.concatenate([cols(b_in, i) for i in order] + [jnp.zeros((pad,), f32)])[None, :]
    wa = jnp.zeros((SMALL_W, GLA_QK), f32).at[FOX_HEADS:FOX_HEADS + GLA_RANK].set(w_alpha2).astype(bf16)
    head_of = np.arange(FOX_W) // FOX_DH
    indq = np.zeros((FOX_W, SMALL_W), np.float32)
    indq[np.arange(FOX_W), FOX_HEADS + head_of] = 1.0
    indk = np.zeros((FOX_W, SMALL_W), np.float32)
    indk[np.arange(FOX_W), 2 * FOX_HEADS + head_of] = 1.0
    row = lambda a: a[None, :].astype(f32)
    return dict(
        w1=w1, b1=b1, wa=wa, ba=row(b_alpha2), indq=jnp.asarray(indq, bf16), indk=jnp.asarray(indk, bf16),
        g=row(gla_norm_g),
        wz=w_in[:, _OFF[9]:].astype(bf16), bz=row(b_in[_OFF[9]:]),
        wpg=w_proj_gla.astype(bf16), wpf=w_proj_fox.astype(bf16), wo=w_out.astype(bf16),
        g1=row(ln1_g), b1n=row(ln1_b),
        wg=w_ffn_gate.astype(bf16), wu=w_ffn_up.astype(bf16), wd=w_ffn_down.astype(bf16),
        g2=row(ln2_g), b2n=row(ln2_b),
    )


def _time_on_lanes(a, lpad):
    B, L, W = a.shape
    a = jnp.transpose(a, (0, 2, 1))
    return jnp.pad(a, ((0, 0), (0, 0), (0, lpad - L)))


def _layer(x, s0, k_past, v_past, lf_past, p, alpha):
    B, L, _ = x.shape
    n = B * L
    x2d = x.reshape(n, D_MODEL)
    gq, gk, gv, la, sg, fq, fk, fv, fkb, fvb, aux = _inproj(x2d, p["w1"], p["b1"], p["wa"], p["ba"], p["indq"], p["indk"])
    r3 = lambda a: a.reshape(B, L, a.shape[-1])

    o_gla, st = _gla(r3(gq), r3(gk), r3(gv), r3(la), r3(sg), p["g"], _state_to_pairs(s0))
    gla_state = _pairs_to_state(st)

    aux3 = r3(aux)
    lf = aux3[:, :, :FOX_HEADS]
    if k_past is None:
        nc = L // LANES
        aux_t = _time_on_lanes(aux3[:, :, :3 * FOX_HEADS], L).reshape(B, 3 * FOX_HEADS, nc, LANES)
        ct, start = _fox_meta(aux_t)
        o_fox = _fox_prompt(r3(fq), r3(fkb), r3(fvb), ct, start.reshape(B, nc))
    else:
        past = k_past.shape[1]
        lk = past + L
        lpad = -(-lk // (8 * LANES)) * (8 * LANES)
        lf_all = jnp.concatenate([lf_past.astype(f32), lf], axis=1)
        stats = jnp.concatenate([lf_all, jnp.zeros((B, lk, 2 * FOX_HEADS), f32)], axis=2)
        aux_t = _time_on_lanes(stats, lpad).reshape(B, 3 * FOX_HEADS, lpad // LANES, LANES)
        ct, _ = _fox_meta(aux_t)
        ct_flat = ct.reshape(B, FOX_HEADS, lpad)
        o_fox = _fox_sample(r3(fq), r3(fkb), r3(fvb), k_past.reshape(B, past * FOX_HEADS, FOX_DH),
                            v_past.reshape(B, past * FOX_HEADS, FOX_DH), ct_flat)

    x1 = _mix(x2d, o_gla.reshape(n, GLA_V), o_fox.reshape(n, FOX_W), p["wz"], p["bz"], p["wpg"], p["wpf"], p["wo"],
              p["g1"], p["b1n"], alpha)
    y = _ffn(x1, p["wg"], p["wu"], p["wd"], p["g2"], p["b2n"], alpha)
    return (y.reshape(B, L, D_MODEL), gla_state,
            fk.reshape(B, L, FOX_HEADS, FOX_DH), fv.reshape(B, L, FOX_HEADS, FOX_DH), lf)


def kernel(x_prompt, x_sample, state_gla, cache_fox_k, cache_fox_v, cache_fox_logf, w_in, b_in, w_alpha2, b_alpha2,
           gla_norm_g, w_proj_gla, w_proj_fox, w_out, ln1_g, ln1_b, w_ffn_gate, w_ffn_up, w_ffn_down, ln2_g, ln2_b):
    depth = w_in.shape[0]
    alpha = (2.0 * depth) ** 0.25
    yp, ys = x_prompt, x_sample
    outs_p, outs_s = [], []
    for l in range(depth):
        p = _prep_weights(w_in[l], b_in[l], w_alpha2[l], b_alpha2[l], gla_norm_g[l], w_proj_gla[l], w_proj_fox[l],
                          w_out[l], ln1_g[l], ln1_b[l], w_ffn_gate[l], w_ffn_up[l], w_ffn_down[l], ln2_g[l], ln2_b[l])
        s0 = jnp.zeros((yp.shape[0], GLA_HEADS, GLA_DK, GLA_DV), f32)
        yp, *rest_p = _layer(yp, s0, None, None, None, p, alpha)
        outs_p.append(rest_p)
        ys, *rest_s = _layer(ys, state_gla[l], cache_fox_k[l], cache_fox_v[l], cache_fox_logf[l], p, alpha)
        outs_s.append(rest_s)
    stack = lambda outs, i: jnp.stack([o[i] for o in outs])
    return (yp, ys,
            stack(outs_p, 0), stack(outs_p, 1), stack(outs_p, 2), stack(outs_p, 3),
            stack(outs_s, 0), stack(outs_s, 1), stack(outs_s, 2), stack(outs_s, 3))
```

```python
import functools

import jax
import jax.numpy as jnp
import numpy as np
from jax import lax
from jax.experimental import pallas as pl
from jax.experimental.pallas import tpu as pltpu

f32 = jnp.float32
bf16 = jnp.bfloat16

D_MODEL = 1024
GLA_HEADS = 4
GLA_DK = 64
GLA_DV = 128
GLA_RANK = 16
GLA_TAU = 16.0
GLA_QK = GLA_HEADS * GLA_DK
GLA_V = GLA_HEADS * GLA_DV
GLA_CHUNK = 64
FOX_HEADS = 8
FOX_DH = 64
FOX_W = FOX_HEADS * FOX_DH
D_FF = 2816
LN_EPS = 1e-5
RMS_EPS = 1e-5
SPLIT_SIZES = (GLA_QK, GLA_QK, GLA_V, GLA_RANK, GLA_V, FOX_W, FOX_W, FOX_W, FOX_HEADS, D_MODEL, D_MODEL)
_OFF = np.concatenate([[0], np.cumsum(SPLIT_SIZES)]).astype(int)

LANES = 128
SMALL_W = LANES
ATT_BLOCK = 128
LOG2E = 1.4426950408889634
SKIP_T2 = 110.0 * LOG2E
NORM_SLACK = 1.02
NEG_BIG = -1e30
VMEM_LIMIT = 56 * 1024 * 1024


def _dot(a, b):
    return jnp.dot(a, b, preferred_element_type=f32)


def _dot_nt(a, b):
    return lax.dot_general(a, b, (((1,), (1,)), ((), ())), preferred_element_type=f32)


def _dot_tn(a, b):
    return lax.dot_general(a, b, (((0,), (0,)), ((), ())), preferred_element_type=f32)


def _dot_hi(a, b):
    return jnp.dot(a, b, preferred_element_type=f32, precision=lax.Precision.HIGHEST)


def _log_sigmoid(x):
    return jnp.minimum(x, 0.0) - jnp.log1p(jnp.exp(-jnp.abs(x)))


def _const_spec(shape):
    return pl.BlockSpec(shape, lambda *_: (0,) * len(shape))


def _inproj_kernel(x_ref, w_ref, b_ref, wkv_ref, bkv_ref, wa_ref, ba_ref, indq_ref,
                   gq_ref, gk_ref, gv_ref, la_ref, sg_ref, fq_ref, fkt_ref, fvt_ref,
                   fkbt_ref, fvbt_ref, aux_ref, nk2t_ref):
    xb = x_ref[...].astype(bf16)
    tm = x_ref.shape[0]

    def group(lo, width):
        return _dot(xb, w_ref[:, lo:lo + width]) + b_ref[:, lo:lo + width]

    o = 0
    gq_ref[...] = (group(o, GLA_QK) * (GLA_DK ** -0.5)).astype(bf16)
    o += GLA_QK
    gk_ref[...] = group(o, GLA_QK).astype(bf16)
    o += GLA_QK
    gv_ref[...] = group(o, GLA_V).astype(bf16)
    o += GLA_V
    gr = group(o, GLA_V)
    sg_ref[...] = (gr * jax.nn.sigmoid(gr)).astype(bf16)
    o += GLA_V
    fq = (group(o, FOX_W) * (FOX_DH ** -0.5 * LOG2E)).astype(bf16)
    fq_ref[...] = fq
    o += FOX_W
    small = group(o, SMALL_W)
    la_pre = _dot(small.astype(bf16), wa_ref[...]) + ba_ref[...]
    la_ref[...] = _log_sigmoid(la_pre) * (1.0 / GLA_TAU)
    lane = lax.broadcasted_iota(jnp.int32, (1, SMALL_W), 1)
    lf = jnp.where(lane < FOX_HEADS, _log_sigmoid(small), 0.0)
    fq32 = fq.astype(f32)
    nq2 = _dot((fq32 * fq32).astype(bf16), indq_ref[...])
    aux_ref[...] = lf + nq2

    bias_t = jnp.concatenate([bkv_ref[...]] * (tm // LANES), axis=1)
    for g, (t_ref, tb_ref) in enumerate(((fkt_ref, fkbt_ref), (fvt_ref, fvbt_ref))):
        rows = slice(g * FOX_W, (g + 1) * FOX_W)
        kv_t = _dot_nt(wkv_ref[rows, :], xb) + bias_t[rows]
        t_ref[0] = kv_t
        kvb = kv_t.astype(bf16)
        tb_ref[0] = kvb
        if g == 0:
            k32 = kvb.astype(f32)
            k2 = k32 * k32
            nk2t_ref[0] = jnp.concatenate(
                [jnp.sum(k2[h * FOX_DH:(h + 1) * FOX_DH], axis=0, keepdims=True) for h in range(FOX_HEADS)], axis=0)


def _inproj(x2d, lo, w1, b1, wkv, bkv, wa, ba, indq):
    n = x2d.shape[0]
    bo = n // lo
    tm = min(512, lo)
    assert lo % tm == 0 and tm % LANES == 0
    tpb = lo // tm
    row = lambda w: pl.BlockSpec((tm, w), lambda i: (i, 0))
    col = lambda w: pl.BlockSpec((1, w, tm), lambda i: (i // tpb, 0, i % tpb))
    rows_out = lambda w, dt: (jax.ShapeDtypeStruct((n, w), dt), row(w))
    cols_out = lambda w, dt: (jax.ShapeDtypeStruct((bo, w, lo), dt), col(w))
    outs = (
        rows_out(GLA_QK, bf16),
        rows_out(GLA_QK, bf16),
        rows_out(GLA_V, bf16),
        rows_out(GLA_QK, f32),
        rows_out(GLA_V, bf16),
        rows_out(FOX_W, bf16),
        cols_out(FOX_W, f32),
        cols_out(FOX_W, f32),
        cols_out(FOX_W, bf16),
        cols_out(FOX_W, bf16),
        rows_out(SMALL_W, f32),
        cols_out(FOX_HEADS, f32),
    )
    consts = (w1, b1, wkv, bkv, wa, ba, indq)
    return pl.pallas_call(
        _inproj_kernel,
        grid=(n // tm,),
        in_specs=[row(D_MODEL)] + [_const_spec(a.shape) for a in consts],
        out_specs=[o[1] for o in outs],
        out_shape=[o[0] for o in outs],
        compiler_params=pltpu.CompilerParams(dimension_semantics=("arbitrary",), vmem_limit_bytes=VMEM_LIMIT),
        name="inproj",
    )(x2d, *consts)


def _gla_kernel(q_ref, k_ref, v_ref, la_ref, sg_ref, g_ref, s0_ref, o_ref, sfin_ref, st_ref, *, chunk, nsub):
    t = pl.program_id(1)

    @pl.when(t == 0)
    def _():
        st_ref[...] = s0_ref[0]

    C = chunk
    tl = C * nsub
    r_i = lax.broadcasted_iota(jnp.int32, (C, C), 0)
    c_i = lax.broadcasted_iota(jnp.int32, (C, C), 1)
    causal = c_i <= r_i
    rt = lax.broadcasted_iota(jnp.int32, (tl, tl), 0)
    ct = lax.broadcasted_iota(jnp.int32, (tl, tl), 1)
    chunk_tril = ((ct <= rt) & (ct >= (rt // C) * C)).astype(bf16)
    lane = lax.broadcasted_iota(jnp.int32, (1, LANES), 1)
    first_half = lane < GLA_DK
    mid = C // 2
    pairs = range(GLA_HEADS // 2)

    def lanes(p):
        return slice(p * LANES, (p + 1) * LANES)

    def one_head(x, hh):
        return jnp.where(first_half if hh == 0 else jnp.logical_not(first_half), x, jnp.zeros_like(x))

    la = la_ref[0]
    la_hi = la.astype(bf16)
    la_lo = (la - la_hi.astype(f32)).astype(bf16)
    bcum_all = _dot(chunk_tril, la_hi) + _dot(chunk_tril, la_lo)

    q_in, dec, amat, upd = [], [], {}, {}
    for c in range(nsub):
        rows = slice(c * C, (c + 1) * C)
        bcum = bcum_all[rows]
        b_last = bcum[C - 1:C, :]
        ref = bcum[mid:mid + 1, :]
        q = q_ref[0, rows, :].astype(f32)
        k = k_ref[0, rows, :].astype(f32)
        q_in.append((q * jnp.exp(bcum)).astype(bf16))
        q_a = (q * jnp.exp(jnp.clip(bcum - ref, -80.0, 80.0))).astype(bf16)
        k_a = (k * jnp.exp(jnp.clip(ref - bcum, -80.0, 80.0))).astype(bf16)
        k_d = (k * jnp.exp(b_last - bcum)).astype(bf16)
        dec.append(jnp.exp(b_last))
        for p in pairs:
            for hh in range(2):
                h = 2 * p + hh
                v_h = v_ref[0, rows, h * GLA_DV:(h + 1) * GLA_DV]
                a = _dot_nt(one_head(q_a[:, lanes(p)], hh), k_a[:, lanes(p)])
                amat[c, h] = jnp.where(causal, a, 0.0).astype(bf16)
                upd[c, h] = _dot_tn(v_h, k_d[:, lanes(p)])

    states = []
    st = [st_ref[p] for p in pairs]
    for c in range(nsub):
        states.append([s.astype(bf16) for s in st])
        st = [dec[c][:, lanes(p)] * st[p] + jnp.where(first_half, upd[c, 2 * p], upd[c, 2 * p + 1]) for p in pairs]
    for p in pairs:
        st_ref[p] = st[p]

    for c in range(nsub):
        rows = slice(c * C, (c + 1) * C)
        for h in range(GLA_HEADS):
            p, hh = divmod(h, 2)
            vs = slice(h * GLA_DV, (h + 1) * GLA_DV)
            o = _dot(amat[c, h], v_ref[0, rows, vs]) + _dot_nt(one_head(q_in[c][:, lanes(p)], hh), states[c][p])
            o = o * lax.rsqrt(jnp.mean(o * o, axis=-1, keepdims=True) + RMS_EPS)
            o = o * g_ref[:, vs] * sg_ref[0, rows, vs].astype(f32)
            o_ref[0, rows, vs] = o.astype(bf16)

    @pl.when(t == pl.num_programs(1) - 1)
    def _():
        sfin_ref[0] = st_ref[...]


def _gla(gq, gk, gv, la, sg, g, s0t):
    B, L, _ = gq.shape
    C = min(GLA_CHUNK, L)
    tl = min(256, L)
    assert L % tl == 0 and tl % C == 0
    tok = lambda w: pl.BlockSpec((1, tl, w), lambda b, t: (b, t, 0))
    st_spec = pl.BlockSpec((1, 2, GLA_DV, LANES), lambda b, t: (b, 0, 0, 0))
    return pl.pallas_call(
        functools.partial(_gla_kernel, chunk=C, nsub=tl // C),
        grid=(B, L // tl),
        in_specs=[tok(GLA_QK), tok(GLA_QK), tok(GLA_V), tok(GLA_QK), tok(GLA_V),
                  pl.BlockSpec((1, GLA_V), lambda b, t: (0, 0)), st_spec],
        out_specs=[tok(GLA_V), st_spec],
        out_shape=(jax.ShapeDtypeStruct((B, L, GLA_V), bf16),
                   jax.ShapeDtypeStruct((B, 2, GLA_DV, LANES), f32)),
        scratch_shapes=[pltpu.VMEM((2, GLA_DV, LANES), f32)],
        compiler_params=pltpu.CompilerParams(dimension_semantics=("arbitrary", "arbitrary"),
                                             vmem_limit_bytes=VMEM_LIMIT),
        name="gla_scan",
    )(gq, gk, gv, la, sg, g, s0t)


def _state_to_pairs(s):
    B = s.shape[0]
    s = s.reshape(B, 2, 2, GLA_DK, GLA_DV)
    return jnp.transpose(s, (0, 1, 4, 2, 3)).reshape(B, 2, GLA_DV, 2 * GLA_DK)


def _pairs_to_state(st):
    B = st.shape[0]
    st = st.reshape(B, 2, GLA_DV, 2, GLA_DK)
    return jnp.transpose(st, (0, 1, 3, 4, 2)).reshape(B, GLA_HEADS, GLA_DK, GLA_DV)


def _meta_kernel(a_ref, ct_ref, start_ref, *, nc):
    H = FOX_HEADS
    r_i = lax.broadcasted_iota(jnp.int32, (LANES, LANES), 0)
    c_i = lax.broadcasted_iota(jnp.int32, (LANES, LANES), 1)
    upper = (r_i <= c_i).astype(f32)
    rj = lax.broadcasted_iota(jnp.int32, (nc, nc), 0)
    cj = lax.broadcasted_iota(jnp.int32, (nc, nc), 1)
    strict_lower = (cj < rj).astype(f32)
    eye = rj == cj

    def to_row(col):
        return jnp.sum(jnp.where(eye, jnp.broadcast_to(col, (nc, nc)), 0.0), axis=0, keepdims=True)

    start = None
    for h in range(H):
        y = _dot_hi(a_ref[0, h], upper)
        tot = jnp.broadcast_to(y[:, LANES - 1:LANES], (nc, LANES))
        c = (y + _dot_hi(strict_lower, tot)) * LOG2E
        ct_ref[0, h] = c
        cmax = jnp.max(c, axis=1, keepdims=True)
        cmin = jnp.min(c, axis=1, keepdims=True)
        qmax = jnp.sqrt(jnp.max(a_ref[0, H + h], axis=1, keepdims=True))
        kmax = jnp.sqrt(jnp.max(jnp.max(a_ref[0, 2 * H + h], axis=1, keepdims=True), axis=0, keepdims=True))
        thr = cmax + (2.0 * NORM_SLACK) * qmax * kmax + SKIP_T2
        needed = (to_row(cmin) <= thr) & (cj <= rj)
        first = jnp.min(jnp.where(needed, cj, nc), axis=1, keepdims=True)
        start = first if start is None else jnp.minimum(start, first)
    start_ref[0] = to_row(start.astype(f32)).astype(jnp.int32)


def _fox_meta(aux_t):
    B, _, nc, _ = aux_t.shape
    return pl.pallas_call(
        functools.partial(_meta_kernel, nc=nc),
        grid=(B,),
        in_specs=[pl.BlockSpec((1, 3 * FOX_HEADS, nc, LANES), lambda b: (b, 0, 0, 0))],
        out_specs=[pl.BlockSpec((1, FOX_HEADS, nc, LANES), lambda b: (b, 0, 0, 0)),
                   pl.BlockSpec((1, 1, nc), lambda b: (b, 0, 0))],
        out_shape=(jax.ShapeDtypeStruct((B, FOX_HEADS, nc, LANES), f32),
                   jax.ShapeDtypeStruct((B, 1, nc), jnp.int32)),
        compiler_params=pltpu.CompilerParams(dimension_semantics=("arbitrary",), vmem_limit_bytes=VMEM_LIMIT),
        name="fox_meta",
    )(aux_t)


def _head_rows(h):
    return slice(h * FOX_DH, (h + 1) * FOX_DH)


def _pair_slab(h, width=LANES):
    return slice((h // 2) * width, (h // 2 + 1) * width)


def _v_aug(vt_h, h):
    one = jnp.ones_like(vt_h)
    return jnp.concatenate([vt_h, one] if h % 2 == 0 else [one, vt_h], axis=0)


def _fox_prompt_kernel(start_ref, q_ref, k0_ref, k1_ref, k2_ref, v0_ref, v1_ref, v2_ref,
                       ct_ref, kh_ref, vh_ref, o_ref, m_s, acc_s, kbuf, vbuf, sem):
    T = ATT_BLOCK
    H = FOX_HEADS
    b = pl.program_id(0)
    i = pl.program_id(1)
    start = start_ref[b, i]
    r_i = lax.broadcasted_iota(jnp.int32, (T, T), 0)
    c_i = lax.broadcasted_iota(jnp.int32, (T, T), 1)
    causal = c_i <= r_i
    lane = lax.broadcasted_iota(jnp.int32, (1, LANES), 1)
    first_half = lane < FOX_DH
    i1 = jnp.maximum(i - 1, 0)
    i2 = jnp.maximum(i - 2, 0)

    def head_q(h):
        hm = first_half if h % 2 == 0 else jnp.logical_not(first_half)
        qp = q_ref[0, :, _pair_slab(h)]
        return jnp.where(hm, qp, jnp.zeros_like(qp))

    def head_base(h):
        return ct_ref[0, h, pl.ds(i, 1), 0:1]

    scores = []
    for h in range(H):
        qm = head_q(h)
        base = head_base(h)
        ck0 = ct_ref[0, h, pl.ds(i, 1), :] - base
        ck1 = jnp.where(i >= 1, ct_ref[0, h, pl.ds(i1, 1), :] - base, -NEG_BIG)
        ck2 = jnp.where(i >= 2, ct_ref[0, h, pl.ds(i2, 1), :] - base, -NEG_BIG)
        s0 = jnp.where(causal, _dot(qm, k0_ref[0, _pair_slab(h), :]) - ck0, NEG_BIG)
        s1 = _dot(qm, k1_ref[0, _pair_slab(h), :]) - ck1
        s2 = _dot(qm, k2_ref[0, _pair_slab(h), :]) - ck2
        m = jnp.max(jnp.maximum(jnp.maximum(s0, s1), s2), axis=1, keepdims=True)
        m_s[h] = m
        scores.append((s0, s1, s2, m))
    for h in range(H):
        s0, s1, s2, m = scores[h]
        acc_s[h] = (_dot_nt(jnp.exp2(s0 - m).astype(bf16), _v_aug(v0_ref[0, _head_rows(h), :], h))
                    + _dot_nt(jnp.exp2(s1 - m).astype(bf16), _v_aug(v1_ref[0, _head_rows(h), :], h))
                    + _dot_nt(jnp.exp2(s2 - m).astype(bf16), _v_aug(v2_ref[0, _head_rows(h), :], h)))

    n_far = jnp.maximum(i - 2 - start, 0)

    def copies(j, slot):
        cols = pl.ds(pl.multiple_of(j * T, T), T)
        return (pltpu.make_async_copy(kh_ref.at[b, :, cols], kbuf.at[slot], sem.at[0, slot]),
                pltpu.make_async_copy(vh_ref.at[b, :, cols], vbuf.at[slot], sem.at[1, slot]))

    @pl.when(n_far > 0)
    def _():
        for cp in copies(start, 0):
            cp.start()

        def body(t, carry):
            j = start + t
            slot = lax.rem(t, 2)
            for cp in copies(j, slot):
                cp.wait()

            @pl.when(t + 1 < n_far)
            def _():
                for cp in copies(j + 1, 1 - slot):
                    cp.start()

            for h in range(H):
                ck = ct_ref[0, h, pl.ds(j, 1), :] - head_base(h)
                s = _dot(head_q(h), kbuf[slot, _pair_slab(h), :]) - ck
                m_prev = m_s[h]
                m_new = jnp.maximum(m_prev, jnp.max(s, axis=1, keepdims=True))
                alpha = jnp.exp2(m_prev - m_new)
                p = jnp.exp2(s - m_new)
                acc_s[h] = alpha * acc_s[h] + _dot_nt(p.astype(bf16), _v_aug(vbuf[slot, _head_rows(h), :], h))
                m_s[h] = m_new
            return carry

        lax.fori_loop(0, n_far, body, 0)

    for p in range(H // 2):
        even = acc_s[2 * p]
        odd = acc_s[2 * p + 1]
        numer = jnp.where(first_half, even, odd)
        denom = pltpu.roll(jnp.where(first_half, odd, even), FOX_DH, 1)
        o_ref[0, :, p * LANES:(p + 1) * LANES] = (numer / denom).astype(bf16)


def _fox_prompt(fq, fkbt, fvbt, ct, start):
    B, L, _ = fq.shape
    T = ATT_BLOCK
    nq = L // T
    kv = lambda d: pl.BlockSpec((1, FOX_W, T), lambda b, i, s: (b, 0, jnp.maximum(i - d, 0)))
    grid_spec = pltpu.PrefetchScalarGridSpec(
        num_scalar_prefetch=1,
        grid=(B, nq),
        in_specs=[pl.BlockSpec((1, T, FOX_W), lambda b, i, s: (b, i, 0)),
                  kv(0), kv(1), kv(2), kv(0), kv(1), kv(2),
                  pl.BlockSpec((1, FOX_HEADS, nq, LANES), lambda b, i, s: (b, 0, 0, 0)),
                  pl.BlockSpec(memory_space=pl.ANY), pl.BlockSpec(memory_space=pl.ANY)],
        out_specs=pl.BlockSpec((1, T, FOX_W), lambda b, i, s: (b, i, 0)),
        scratch_shapes=[pltpu.VMEM((FOX_HEADS, T, 1), f32),
                        pltpu.VMEM((FOX_HEADS, T, LANES), f32),
                        pltpu.VMEM((2, FOX_W, T), bf16), pltpu.VMEM((2, FOX_W, T), bf16),
                        pltpu.SemaphoreType.DMA((2, 2))],
    )
    return pl.pallas_call(
        _fox_prompt_kernel,
        grid_spec=grid_spec,
        out_shape=jax.ShapeDtypeStruct((B, L, FOX_W), bf16),
        compiler_params=pltpu.CompilerParams(dimension_semantics=("arbitrary", "arbitrary"),
                                             vmem_limit_bytes=VMEM_LIMIT),
        name="fox_prompt",
    )(start, fq, fkbt, fkbt, fkbt, fvbt, fvbt, fvbt, ct, fkbt, fvbt)


def _fox_sample_kernel(q_ref, kn_ref, vn_ref, kp_ref, vp_ref, ct_ref, o_ref, *, past, lq):
    H = FOX_HEADS
    r_i = lax.broadcasted_iota(jnp.int32, (lq, lq), 0)
    c_i = lax.broadcasted_iota(jnp.int32, (lq, lq), 1)
    causal = c_i <= r_i
    lane = lax.broadcasted_iota(jnp.int32, (1, LANES), 1)
    first_half = lane < FOX_DH
    probs = []
    for h in range(H):
        hm = first_half if h % 2 == 0 else jnp.logical_not(first_half)
        qp = q_ref[0, :, _pair_slab(h)]
        qm = jnp.where(hm, qp, jnp.zeros_like(qp))
        base = ct_ref[0, h:h + 1, past:past + 1]
        s_p = _dot(qm, kp_ref[0, _pair_slab(h), :].astype(bf16)) - (ct_ref[0, h:h + 1, 0:past] - base)
        s_n = _dot(qm, kn_ref[0, _pair_slab(h), :]) - (ct_ref[0, h:h + 1, past:past + lq] - base)
        s_n = jnp.where(causal, s_n, NEG_BIG)
        m = jnp.maximum(jnp.max(s_p, axis=1, keepdims=True), jnp.max(s_n, axis=1, keepdims=True))
        probs.append((jnp.exp2(s_p - m).astype(bf16), jnp.exp2(s_n - m).astype(bf16)))
    accs = []
    for h in range(H):
        p_p, p_n = probs[h]
        accs.append(_dot_nt(p_p, _v_aug(vp_ref[0, _head_rows(h), :].astype(bf16), h))
                    + _dot_nt(p_n, _v_aug(vn_ref[0, _head_rows(h), :], h)))
    for p in range(H // 2):
        numer = jnp.where(first_half, accs[2 * p], accs[2 * p + 1])
        denom = pltpu.roll(jnp.where(first_half, accs[2 * p + 1], accs[2 * p]), FOX_DH, 1)
        o_ref[0, :, p * LANES:(p + 1) * LANES] = (numer / denom).astype(bf16)


def _fox_sample(fq, fkbt, fvbt, kt_past, vt_past, ct_flat):
    B, lq, _ = fq.shape
    past = kt_past.shape[2]
    lpad = ct_flat.shape[2]
    new = pl.BlockSpec((1, FOX_W, lq), lambda b: (b, 0, 0))
    old = pl.BlockSpec((1, FOX_W, past), lambda b: (b, 0, 0))
    qo = pl.BlockSpec((1, lq, FOX_W), lambda b: (b, 0, 0))
    return pl.pallas_call(
        functools.partial(_fox_sample_kernel, past=past, lq=lq),
        grid=(B,),
        in_specs=[qo, new, new, old, old, pl.BlockSpec((1, FOX_HEADS, lpad), lambda b: (b, 0, 0))],
        out_specs=qo,
        out_shape=jax.ShapeDtypeStruct((B, lq, FOX_W), bf16),
        compiler_params=pltpu.CompilerParams(dimension_semantics=("arbitrary",), vmem_limit_bytes=VMEM_LIMIT),
        name="fox_sample",
    )(fq, fkbt, fvbt, kt_past, vt_past, ct_flat)


def _layer_norm(t, g, b):
    mu = jnp.mean(t, axis=-1, keepdims=True)
    d = t - mu
    var = jnp.mean(d * d, axis=-1, keepdims=True)
    return d * lax.rsqrt(var + LN_EPS) * g + b


def _mix_kernel(x_ref, og_ref, of_ref, wz_ref, bz_ref, wpg_ref, wpf_ref, wo_ref, g1_ref, b1_ref, x1_ref, *, alpha):
    x = x_ref[...]
    xb = x.astype(bf16)
    zg = _dot(xb, wz_ref[:, :D_MODEL]) + bz_ref[:, :D_MODEL]
    zf = _dot(xb, wz_ref[:, D_MODEL:]) + bz_ref[:, D_MODEL:]
    m = jax.nn.sigmoid(zg) * _dot(og_ref[...], wpg_ref[...]) + jax.nn.sigmoid(zf) * _dot(of_ref[...], wpf_ref[...])
    t = alpha * x + _dot(m.astype(bf16), wo_ref[...])
    x1_ref[...] = _layer_norm(t, g1_ref[...], b1_ref[...])


def _ffn_kernel(x1_ref, wg_ref, wu_ref, wd_ref, g2_ref, b2_ref, y_ref, *, alpha, nchunk):
    x1 = x1_ref[...]
    xb = x1.astype(bf16)
    w = D_FF // nchunk
    acc = None
    for c in range(nchunk):
        cs = slice(c * w, (c + 1) * w)
        g = _dot(xb, wg_ref[:, cs])
        u = _dot(xb, wu_ref[:, cs])
        hid = (g * jax.nn.sigmoid(g) * u).astype(bf16)
        part = _dot(hid, wd_ref[cs, :])
        acc = part if acc is None else acc + part
    y_ref[...] = _layer_norm(alpha * x1 + acc, g2_ref[...], b2_ref[...])


def _mix(x2d, og, of, wz, bz, wpg, wpf, wo, g1, b1, alpha):
    n = x2d.shape[0]
    tm = min(512, n)
    row = lambda w: pl.BlockSpec((tm, w), lambda i: (i, 0))
    return pl.pallas_call(
        functools.partial(_mix_kernel, alpha=alpha),
        grid=(n // tm,),
        in_specs=[row(D_MODEL), row(GLA_V), row(FOX_W)] + [_const_spec(a.shape) for a in (wz, bz, wpg, wpf, wo, g1, b1)],
        out_specs=row(D_MODEL),
        out_shape=jax.ShapeDtypeStruct((n, D_MODEL), f32),
        compiler_params=pltpu.CompilerParams(dimension_semantics=("arbitrary",), vmem_limit_bytes=VMEM_LIMIT),
        name="mix",
    )(x2d, og, of, wz, bz, wpg, wpf, wo, g1, b1)


def _ffn(x1, wg, wu, wd, g2, b2, alpha):
    n = x1.shape[0]
    tm = min(512, n)
    row = lambda w: pl.BlockSpec((tm, w), lambda i: (i, 0))
    return pl.pallas_call(
        functools.partial(_ffn_kernel, alpha=alpha, nchunk=2),
        grid=(n // tm,),
        in_specs=[row(D_MODEL)] + [_const_spec(a.shape) for a in (wg, wu, wd, g2, b2)],
        out_specs=row(D_MODEL),
        out_shape=jax.ShapeDtypeStruct((n, D_MODEL), f32),
        compiler_params=pltpu.CompilerParams(dimension_semantics=("arbitrary",), vmem_limit_bytes=VMEM_LIMIT),
        name="ffn",
    )(x1, wg, wu, wd, g2, b2)


def _prep_weights(w_in, b_in, w_alpha2, b_alpha2, gla_norm_g, w_proj_gla, w_proj_fox, w_out,
                  ln1_g, ln1_b, w_ffn_gate, w_ffn_up, w_ffn_down, ln2_g, ln2_b):
    def cols(a, idx):
        return a[..., _OFF[idx]:_OFF[idx + 1]]

    order = (0, 1, 2, 4, 5, 8, 3)
    pad = SMALL_W - FOX_HEADS - GLA_RANK
    w1 = jnp.concatenate([cols(w_in, i) for i in order] + [jnp.zeros((D_MODEL, pad), f32)], axis=1).astype(bf16)
    b1 = jnp.concatenate([cols(b_in, i) for i in order] + [jnp.zeros((pad,), f32)])[None, :]
    wkv = jnp.transpose(w_in[:, _OFF[6]:_OFF[8]]).astype(bf16)
    bkv = jnp.broadcast_to(b_in[_OFF[6]:_OFF[8], None], (2 * FOX_W, LANES)).astype(f32)
    wa = jnp.zeros((SMALL_W, GLA_QK), f32).at[FOX_HEADS:FOX_HEADS + GLA_RANK].set(w_alpha2).astype(bf16)
    head_of = np.arange(FOX_W) // FOX_DH
    indq = np.zeros((FOX_W, SMALL_W), np.float32)
    indq[np.arange(FOX_W), FOX_HEADS + head_of] = 1.0
    row = lambda a: a[None, :].astype(f32)
    return dict(
        w1=w1, b1=b1, wkv=wkv, bkv=bkv, wa=wa, ba=row(b_alpha2), indq=jnp.asarray(indq, bf16),
        g=row(gla_norm_g),
        wz=w_in[:, _OFF[9]:].astype(bf16), bz=row(b_in[_OFF[9]:]),
        wpg=w_proj_gla.astype(bf16), wpf=w_proj_fox.astype(bf16), wo=w_out.astype(bf16),
        g1=row(ln1_g), b1n=row(ln1_b),
        wg=w_ffn_gate.astype(bf16), wu=w_ffn_up.astype(bf16), wd=w_ffn_down.astype(bf16),
        g2=row(ln2_g), b2n=row(ln2_b),
    )


def _time_on_lanes(a, lpad):
    B, L, W = a.shape
    a = jnp.transpose(a, (0, 2, 1))
    return jnp.pad(a, ((0, 0), (0, 0), (0, lpad - L)))


def _heads_last(t, B, L):
    return jnp.transpose(t.reshape(B, FOX_HEADS, FOX_DH, L), (0, 3, 1, 2))


def _layer(x, s0, k_past, v_past, lf_past, p, alpha):
    B, L, _ = x.shape
    n = B * L
    x2d = x.reshape(n, D_MODEL)
    prompt = k_past is None
    lo = L if prompt else n
    gq, gk, gv, la, sg, fq, fkt, fvt, fkbt, fvbt, aux, nk2t = _inproj(
        x2d, lo, p["w1"], p["b1"], p["wkv"], p["bkv"], p["wa"], p["ba"], p["indq"])
    r3 = lambda a: a.reshape(B, L, a.shape[-1])

    o_gla, st = _gla(r3(gq), r3(gk), r3(gv), r3(la), r3(sg), p["g"], _state_to_pairs(s0))
    gla_state = _pairs_to_state(st)

    aux3 = r3(aux)
    lf = aux3[:, :, :FOX_HEADS]
    if prompt:
        nc = L // LANES
        rows_t = _time_on_lanes(aux3[:, :, :2 * FOX_HEADS], L)
        aux_t = jnp.concatenate([rows_t, nk2t], axis=1).reshape(B, 3 * FOX_HEADS, nc, LANES)
        ct, start = _fox_meta(aux_t)
        o_fox = _fox_prompt(r3(fq), fkbt, fvbt, ct, start.reshape(B, nc))
        fk, fv = _heads_last(fkt, B, L), _heads_last(fvt, B, L)
    else:
        past = k_past.shape[1]
        lk = past + L
        lpad = -(-lk // (8 * LANES)) * (8 * LANES)
        lf_all = jnp.concatenate([lf_past.astype(f32), lf], axis=1)
        stats = jnp.concatenate([lf_all, jnp.zeros((B, lk, 2 * FOX_HEADS), f32)], axis=2)
        aux_t = _time_on_lanes(stats, lpad).reshape(B, 3 * FOX_HEADS, lpad // LANES, LANES)
        ct, _ = _fox_meta(aux_t)
        per_stream = lambda t: jnp.transpose(t.reshape(FOX_W, B, L), (1, 0, 2))
        cache_t = lambda c: jnp.transpose(c, (0, 2, 3, 1)).reshape(B, FOX_W, past)
        o_fox = _fox_sample(r3(fq), per_stream(fkbt), per_stream(fvbt), cache_t(k_past), cache_t(v_past),
                            ct.reshape(B, FOX_HEADS, lpad))
        fk, fv = _heads_last(per_stream(fkt), B, L), _heads_last(per_stream(fvt), B, L)

    x1 = _mix(x2d, o_gla.reshape(n, GLA_V), o_fox.reshape(n, FOX_W), p["wz"], p["bz"], p["wpg"], p["wpf"], p["wo"],
              p["g1"], p["b1n"], alpha)
    y = _ffn(x1, p["wg"], p["wu"], p["wd"], p["g2"], p["b2n"], alpha)
    return y.reshape(B, L, D_MODEL), gla_state, fk, fv, lf


def kernel(x_prompt, x_sample, state_gla, cache_fox_k, cache_fox_v, cache_fox_logf, w_in, b_in, w_alpha2, b_alpha2,
           gla_norm_g, w_proj_gla, w_proj_fox, w_out, ln1_g, ln1_b, w_ffn_gate, w_ffn_up, w_ffn_down, ln2_g, ln2_b):
    depth = w_in.shape[0]
    alpha = (2.0 * depth) ** 0.25
    yp, ys = x_prompt, x_sample
    outs_p, outs_s = [], []
    for l in range(depth):
        p = _prep_weights(w_in[l], b_in[l], w_alpha2[l], b_alpha2[l], gla_norm_g[l], w_proj_gla[l], w_proj_fox[l],
                          w_out[l], ln1_g[l], ln1_b[l], w_ffn_gate[l], w_ffn_up[l], w_ffn_down[l], ln2_g[l], ln2_b[l])
        s0 = jnp.zeros((yp.shape[0], GLA_HEADS, GLA_DK, GLA_DV), f32)
        yp, *rest_p = _layer(yp, s0, None, None, None, p, alpha)
        outs_p.append(rest_p)
        ys, *rest_s = _layer(ys, state_gla[l], cache_fox_k[l], cache_fox_v[l], cache_fox_logf[l], p, alpha)
        outs_s.append(rest_s)
    stack = lambda outs, i: jnp.stack([o[i] for o in outs])
    return (yp, ys,
            stack(outs_p, 0), stack(outs_p, 1), stack(outs_p, 2), stack(outs_p, 3),
            stack(outs_s, 0), stack(outs_s, 1), stack(outs_s, 2), stack(outs_s, 3))
```

```python
import functools

import jax
import jax.numpy as jnp
import numpy as np
from jax import lax
from jax.experimental import pallas as pl
from jax.experimental.pallas import tpu as pltpu

f32 = jnp.float32
bf16 = jnp.bfloat16

D_MODEL = 1024
GLA_HEADS = 4
GLA_DK = 64
GLA_DV = 128
GLA_RANK = 16
GLA_TAU = 16.0
GLA_QK = GLA_HEADS * GLA_DK
GLA_V = GLA_HEADS * GLA_DV
GLA_CHUNK = 64
FOX_HEADS = 8
FOX_DH = 64
FOX_W = FOX_HEADS * FOX_DH
D_FF = 2816
LN_EPS = 1e-5
RMS_EPS = 1e-5
SPLIT_SIZES = (GLA_QK, GLA_QK, GLA_V, GLA_RANK, GLA_V, FOX_W, FOX_W, FOX_W, FOX_HEADS, D_MODEL, D_MODEL)
_OFF = np.concatenate([[0], np.cumsum(SPLIT_SIZES)]).astype(int)

LANES = 128
MXU_DIM = 256
FFN_BOUNDS = (0, 6 * MXU_DIM, D_FF)
SMALL_W = LANES
ATT_BLOCK = 128
LOG2E = 1.4426950408889634
SKIP_T2 = 110.0 * LOG2E
NORM_SLACK = 1.02
NEG_BIG = -1e30
VMEM_LIMIT = 56 * 1024 * 1024


def _dot(a, b):
    return jnp.dot(a, b, preferred_element_type=f32)


def _dot_nt(a, b):
    return lax.dot_general(a, b, (((1,), (1,)), ((), ())), preferred_element_type=f32)


def _dot_tn(a, b):
    return lax.dot_general(a, b, (((0,), (0,)), ((), ())), preferred_element_type=f32)


def _dot_hi(a, b):
    return jnp.dot(a, b, preferred_element_type=f32, precision=lax.Precision.HIGHEST)


def _log_sigmoid(x):
    return jnp.minimum(x, 0.0) - jnp.log1p(jnp.exp(-jnp.abs(x)))


def _const_spec(shape):
    return pl.BlockSpec(shape, lambda *_: (0,) * len(shape), pipeline_mode=pl.Buffered(1))


def _inproj_kernel(x_ref, w_ref, b_ref, wkv_ref, bkv_ref, wa_ref, ba_ref, indq_ref,
                   gq_ref, gk_ref, gv_ref, la_ref, sg_ref, fq_ref, fkt_ref, fvt_ref,
                   fkbt_ref, fvbt_ref, aux_ref, nk2t_ref):
    xb = x_ref[...].astype(bf16)
    tm = x_ref.shape[0]

    def group(lo, width):
        return _dot(xb, w_ref[:, lo:lo + width]) + b_ref[:, lo:lo + width]

    o = 0
    gq_ref[...] = (group(o, GLA_QK) * (GLA_DK ** -0.5)).astype(bf16)
    o += GLA_QK
    gk_ref[...] = group(o, GLA_QK).astype(bf16)
    o += GLA_QK
    gv_ref[...] = group(o, GLA_V).astype(bf16)
    o += GLA_V
    gr = group(o, GLA_V)
    sg_ref[...] = (gr * jax.nn.sigmoid(gr)).astype(bf16)
    o += GLA_V
    fq = (group(o, FOX_W) * (FOX_DH ** -0.5 * LOG2E)).astype(bf16)
    fq_ref[...] = fq
    o += FOX_W
    small = group(o, SMALL_W)
    la_pre = _dot(small.astype(bf16), wa_ref[...]) + ba_ref[...]
    la_ref[...] = _log_sigmoid(la_pre) * (1.0 / GLA_TAU)
    lane = lax.broadcasted_iota(jnp.int32, (1, SMALL_W), 1)
    lf = jnp.where(lane < FOX_HEADS, _log_sigmoid(small), 0.0)
    fq32 = fq.astype(f32)
    nq2 = _dot((fq32 * fq32).astype(bf16), indq_ref[...])
    aux_ref[...] = lf + nq2

    bias_t = jnp.concatenate([bkv_ref[...]] * (tm // LANES), axis=1)
    for g, (t_ref, tb_ref) in enumerate(((fkt_ref, fkbt_ref), (fvt_ref, fvbt_ref))):
        rows = slice(g * FOX_W, (g + 1) * FOX_W)
        kv_t = _dot_nt(wkv_ref[rows, :], xb) + bias_t[rows]
        t_ref[0] = kv_t
        kvb = kv_t.astype(bf16)
        tb_ref[0] = kvb
        if g == 0:
            k32 = kvb.astype(f32)
            k2 = k32 * k32
            nk2t_ref[0] = jnp.concatenate(
                [jnp.sum(k2[h * FOX_DH:(h + 1) * FOX_DH], axis=0, keepdims=True) for h in range(FOX_HEADS)], axis=0)


def _inproj(x2d, lo, w1, b1, wkv, bkv, wa, ba, indq):
    n = x2d.shape[0]
    bo = n // lo
    tm = min(512, lo)
    assert lo % tm == 0 and tm % LANES == 0
    tpb = lo // tm
    row = lambda w: pl.BlockSpec((tm, w), lambda i: (i, 0))
    col = lambda w: pl.BlockSpec((1, w, tm), lambda i: (i // tpb, 0, i % tpb))
    rows_out = lambda w, dt: (jax.ShapeDtypeStruct((n, w), dt), row(w))
    cols_out = lambda w, dt: (jax.ShapeDtypeStruct((bo, w, lo), dt), col(w))
    outs = (
        rows_out(GLA_QK, bf16),
        rows_out(GLA_QK, bf16),
        rows_out(GLA_V, bf16),
        rows_out(GLA_QK, f32),
        rows_out(GLA_V, bf16),
        rows_out(FOX_W, bf16),
        cols_out(FOX_W, f32),
        cols_out(FOX_W, f32),
        cols_out(FOX_W, bf16),
        cols_out(FOX_W, bf16),
        rows_out(SMALL_W, f32),
        cols_out(FOX_HEADS, f32),
    )
    consts = (w1, b1, wkv, bkv, wa, ba, indq)
    return pl.pallas_call(
        _inproj_kernel,
        grid=(n // tm,),
        in_specs=[row(D_MODEL)] + [_const_spec(a.shape) for a in consts],
        out_specs=[o[1] for o in outs],
        out_shape=[o[0] for o in outs],
        compiler_params=pltpu.CompilerParams(dimension_semantics=("arbitrary",), vmem_limit_bytes=VMEM_LIMIT),
        name="inproj",
    )(x2d, *consts)


def _gla_kernel(q_ref, k_ref, v_ref, la_ref, sg_ref, g_ref, s0_ref, o_ref, sfin_ref, st_ref, *, chunk, nsub):
    t = pl.program_id(1)

    @pl.when(t == 0)
    def _():
        st_ref[...] = s0_ref[0]

    C = chunk
    tl = C * nsub
    r_i = lax.broadcasted_iota(jnp.int32, (C, C), 0)
    c_i = lax.broadcasted_iota(jnp.int32, (C, C), 1)
    causal = c_i <= r_i
    rt = lax.broadcasted_iota(jnp.int32, (tl, tl), 0)
    ct = lax.broadcasted_iota(jnp.int32, (tl, tl), 1)
    chunk_tril = ((ct <= rt) & (ct >= (rt // C) * C)).astype(bf16)
    lane = lax.broadcasted_iota(jnp.int32, (1, LANES), 1)
    first_half = lane < GLA_DK
    mid = C // 2
    pairs = range(GLA_HEADS // 2)

    def lanes(p):
        return slice(p * LANES, (p + 1) * LANES)

    def one_head(x, hh):
        return jnp.where(first_half if hh == 0 else jnp.logical_not(first_half), x, jnp.zeros_like(x))

    la = la_ref[0]
    la_hi = la.astype(bf16)
    la_lo = (la - la_hi.astype(f32)).astype(bf16)
    bcum_all = _dot(chunk_tril, la_hi) + _dot(chunk_tril, la_lo)

    q_in, dec, amat, upd = [], [], {}, {}
    for c in range(nsub):
        rows = slice(c * C, (c + 1) * C)
        bcum = bcum_all[rows]
        b_last = bcum[C - 1:C, :]
        ref = bcum[mid:mid + 1, :]
        q = q_ref[0, rows, :].astype(f32)
        k = k_ref[0, rows, :].astype(f32)
        q_in.append((q * jnp.exp(bcum)).astype(bf16))
        q_a = (q * jnp.exp(jnp.clip(bcum - ref, -80.0, 80.0))).astype(bf16)
        k_a = (k * jnp.exp(jnp.clip(ref - bcum, -80.0, 80.0))).astype(bf16)
        k_d = (k * jnp.exp(b_last - bcum)).astype(bf16)
        dec.append(jnp.exp(b_last))
        for p in pairs:
            for hh in range(2):
                h = 2 * p + hh
                v_h = v_ref[0, rows, h * GLA_DV:(h + 1) * GLA_DV]
                a = _dot_nt(one_head(q_a[:, lanes(p)], hh), k_a[:, lanes(p)])
                amat[c, h] = jnp.where(causal, a, 0.0).astype(bf16)
                upd[c, h] = _dot_tn(v_h, k_d[:, lanes(p)])

    states = []
    st = [st_ref[p] for p in pairs]
    for c in range(nsub):
        states.append([s.astype(bf16) for s in st])
        st = [dec[c][:, lanes(p)] * st[p] + jnp.where(first_half, upd[c, 2 * p], upd[c, 2 * p + 1]) for p in pairs]
    for p in pairs:
        st_ref[p] = st[p]

    for c in range(nsub):
        rows = slice(c * C, (c + 1) * C)
        for h in range(GLA_HEADS):
            p, hh = divmod(h, 2)
            vs = slice(h * GLA_DV, (h + 1) * GLA_DV)
            o = _dot(amat[c, h], v_ref[0, rows, vs]) + _dot_nt(one_head(q_in[c][:, lanes(p)], hh), states[c][p])
            o = o * lax.rsqrt(jnp.mean(o * o, axis=-1, keepdims=True) + RMS_EPS)
            o = o * g_ref[:, vs] * sg_ref[0, rows, vs].astype(f32)
            o_ref[0, rows, vs] = o.astype(bf16)

    @pl.when(t == pl.num_programs(1) - 1)
    def _():
        sfin_ref[0] = st_ref[...]


def _gla(gq, gk, gv, la, sg, g, s0t):
    B, L, _ = gq.shape
    C = min(GLA_CHUNK, L)
    tl = min(256, L)
    assert L % tl == 0 and tl % C == 0
    tok = lambda w: pl.BlockSpec((1, tl, w), lambda b, t: (b, t, 0))
    st_spec = pl.BlockSpec((1, 2, GLA_DV, LANES), lambda b, t: (b, 0, 0, 0))
    return pl.pallas_call(
        functools.partial(_gla_kernel, chunk=C, nsub=tl // C),
        grid=(B, L // tl),
        in_specs=[tok(GLA_QK), tok(GLA_QK), tok(GLA_V), tok(GLA_QK), tok(GLA_V),
                  pl.BlockSpec((1, GLA_V), lambda b, t: (0, 0)), st_spec],
        out_specs=[tok(GLA_V), st_spec],
        out_shape=(jax.ShapeDtypeStruct((B, L, GLA_V), bf16),
                   jax.ShapeDtypeStruct((B, 2, GLA_DV, LANES), f32)),
        scratch_shapes=[pltpu.VMEM((2, GLA_DV, LANES), f32)],
        compiler_params=pltpu.CompilerParams(dimension_semantics=("arbitrary", "arbitrary"),
                                             vmem_limit_bytes=VMEM_LIMIT),
        name="gla_scan",
    )(gq, gk, gv, la, sg, g, s0t)


def _state_to_pairs(s):
    B = s.shape[0]
    s = s.reshape(B, 2, 2, GLA_DK, GLA_DV)
    return jnp.transpose(s, (0, 1, 4, 2, 3)).reshape(B, 2, GLA_DV, 2 * GLA_DK)


def _pairs_to_state(st):
    B = st.shape[0]
    st = st.reshape(B, 2, GLA_DV, 2, GLA_DK)
    return jnp.transpose(st, (0, 1, 3, 4, 2)).reshape(B, GLA_HEADS, GLA_DK, GLA_DV)


def _meta_kernel(a_ref, ct_ref, start_ref, *, nc):
    H = FOX_HEADS
    r_i = lax.broadcasted_iota(jnp.int32, (LANES, LANES), 0)
    c_i = lax.broadcasted_iota(jnp.int32, (LANES, LANES), 1)
    upper = (r_i <= c_i).astype(f32)
    rj = lax.broadcasted_iota(jnp.int32, (nc, nc), 0)
    cj = lax.broadcasted_iota(jnp.int32, (nc, nc), 1)
    strict_lower = (cj < rj).astype(f32)
    eye = rj == cj

    def to_row(col):
        return jnp.sum(jnp.where(eye, jnp.broadcast_to(col, (nc, nc)), 0.0), axis=0, keepdims=True)

    ys = [_dot_hi(a_ref[0, h], upper) for h in range(H)]
    prev = [_dot_hi(strict_lower, jnp.broadcast_to(y[:, LANES - 1:LANES], (nc, LANES))) for y in ys]
    start = None
    for h in range(H):
        c = (ys[h] + prev[h]) * LOG2E
        ct_ref[0, h] = c
        cmax = jnp.max(c, axis=1, keepdims=True)
        cmin = jnp.min(c, axis=1, keepdims=True)
        qmax = jnp.sqrt(jnp.max(a_ref[0, H + h], axis=1, keepdims=True))
        kmax = jnp.sqrt(jnp.max(jnp.max(a_ref[0, 2 * H + h], axis=1, keepdims=True), axis=0, keepdims=True))
        thr = cmax + (2.0 * NORM_SLACK) * qmax * kmax + SKIP_T2
        needed = (to_row(cmin) <= thr) & (cj <= rj)
        first = jnp.min(jnp.where(needed, cj, nc), axis=1, keepdims=True)
        start = first if start is None else jnp.minimum(start, first)
    start_ref[0] = to_row(start.astype(f32)).astype(jnp.int32)


def _fox_meta(aux_t):
    B, _, nc, _ = aux_t.shape
    return pl.pallas_call(
        functools.partial(_meta_kernel, nc=nc),
        grid=(B,),
        in_specs=[pl.BlockSpec((1, 3 * FOX_HEADS, nc, LANES), lambda b: (b, 0, 0, 0))],
        out_specs=[pl.BlockSpec((1, FOX_HEADS, nc, LANES), lambda b: (b, 0, 0, 0)),
                   pl.BlockSpec((1, 1, nc), lambda b: (b, 0, 0))],
        out_shape=(jax.ShapeDtypeStruct((B, FOX_HEADS, nc, LANES), f32),
                   jax.ShapeDtypeStruct((B, 1, nc), jnp.int32)),
        compiler_params=pltpu.CompilerParams(dimension_semantics=("arbitrary",), vmem_limit_bytes=VMEM_LIMIT),
        name="fox_meta",
    )(aux_t)


def _head_rows(h):
    return slice(h * FOX_DH, (h + 1) * FOX_DH)


def _pair_slab(h, width=LANES):
    return slice((h // 2) * width, (h // 2 + 1) * width)


def _v_aug(vt_h, h):
    one = jnp.ones_like(vt_h)
    return jnp.concatenate([vt_h, one] if h % 2 == 0 else [one, vt_h], axis=0)


def _fox_prompt_kernel(start_ref, q_ref, k0_ref, k1_ref, k2_ref, v0_ref, v1_ref, v2_ref,
                       ct_ref, kh_ref, vh_ref, o_ref, m_s, acc_s, kbuf, vbuf, sem):
    T = ATT_BLOCK
    H = FOX_HEADS
    b = pl.program_id(0)
    i = pl.program_id(1)
    start = start_ref[b, i]
    r_i = lax.broadcasted_iota(jnp.int32, (T, T), 0)
    c_i = lax.broadcasted_iota(jnp.int32, (T, T), 1)
    causal = c_i <= r_i
    lane = lax.broadcasted_iota(jnp.int32, (1, LANES), 1)
    first_half = lane < FOX_DH
    i1 = jnp.maximum(i - 1, 0)
    i2 = jnp.maximum(i - 2, 0)

    def head_q(h):
        hm = first_half if h % 2 == 0 else jnp.logical_not(first_half)
        qp = q_ref[0, :, _pair_slab(h)]
        return jnp.where(hm, qp, jnp.zeros_like(qp))

    def head_base(h):
        return ct_ref[0, h, pl.ds(i, 1), 0:1]

    visible = jnp.concatenate([jnp.ones((T, 2 * T), jnp.bool_), causal], axis=1)
    scores = []
    for h in range(H):
        base = head_base(h)
        ck = jnp.concatenate([jnp.where(i >= 2, ct_ref[0, h, pl.ds(i2, 1), :] - base, -NEG_BIG),
                              jnp.where(i >= 1, ct_ref[0, h, pl.ds(i1, 1), :] - base, -NEG_BIG),
                              ct_ref[0, h, pl.ds(i, 1), :] - base], axis=1)
        kt = jnp.concatenate([r[0, _pair_slab(h), :] for r in (k2_ref, k1_ref, k0_ref)], axis=1)
        s = jnp.where(visible, _dot(head_q(h), kt) - ck, NEG_BIG)
        m = jnp.max(s, axis=1, keepdims=True)
        m_s[h] = m
        scores.append((s, m))
    for h in range(H):
        s, m = scores[h]
        vt = jnp.concatenate([r[0, _head_rows(h), :] for r in (v2_ref, v1_ref, v0_ref)], axis=1)
        acc_s[h] = _dot_nt(jnp.exp2(s - m).astype(bf16), _v_aug(vt, h))

    n_far = jnp.maximum(i - 2 - start, 0)

    def copies(j, slot):
        cols = pl.ds(pl.multiple_of(j * T, T), T)
        return (pltpu.make_async_copy(kh_ref.at[b, :, cols], kbuf.at[slot], sem.at[0, slot]),
                pltpu.make_async_copy(vh_ref.at[b, :, cols], vbuf.at[slot], sem.at[1, slot]))

    @pl.when(n_far > 0)
    def _():
        for cp in copies(start, 0):
            cp.start()

        def body(t, carry):
            j = start + t
            slot = lax.rem(t, 2)
            for cp in copies(j, slot):
                cp.wait()

            @pl.when(t + 1 < n_far)
            def _():
                for cp in copies(j + 1, 1 - slot):
                    cp.start()

            for h in range(H):
                ck = ct_ref[0, h, pl.ds(j, 1), :] - head_base(h)
                s = _dot(head_q(h), kbuf[slot, _pair_slab(h), :]) - ck
                m_prev = m_s[h]
                m_new = jnp.maximum(m_prev, jnp.max(s, axis=1, keepdims=True))
                alpha = jnp.exp2(m_prev - m_new)
                p = jnp.exp2(s - m_new)
                acc_s[h] = alpha * acc_s[h] + _dot_nt(p.astype(bf16), _v_aug(vbuf[slot, _head_rows(h), :], h))
                m_s[h] = m_new
            return carry

        lax.fori_loop(0, n_far, body, 0)

    for p in range(H // 2):
        even = acc_s[2 * p]
        odd = acc_s[2 * p + 1]
        numer = jnp.where(first_half, even, odd)
        denom = pltpu.roll(jnp.where(first_half, odd, even), FOX_DH, 1)
        o_ref[0, :, p * LANES:(p + 1) * LANES] = (numer / denom).astype(bf16)


def _fox_prompt(fq, fkbt, fvbt, ct, start):
    B, L, _ = fq.shape
    T = ATT_BLOCK
    nq = L // T
    kv = lambda d: pl.BlockSpec((1, FOX_W, T), lambda b, i, s: (b, 0, jnp.maximum(i - d, 0)))
    grid_spec = pltpu.PrefetchScalarGridSpec(
        num_scalar_prefetch=1,
        grid=(B, nq),
        in_specs=[pl.BlockSpec((1, T, FOX_W), lambda b, i, s: (b, i, 0)),
                  kv(0), kv(1), kv(2), kv(0), kv(1), kv(2),
                  pl.BlockSpec((1, FOX_HEADS, nq, LANES), lambda b, i, s: (b, 0, 0, 0)),
                  pl.BlockSpec(memory_space=pl.ANY), pl.BlockSpec(memory_space=pl.ANY)],
        out_specs=pl.BlockSpec((1, T, FOX_W), lambda b, i, s: (b, i, 0)),
        scratch_shapes=[pltpu.VMEM((FOX_HEADS, T, 1), f32),
                        pltpu.VMEM((FOX_HEADS, T, LANES), f32),
                        pltpu.VMEM((2, FOX_W, T), bf16), pltpu.VMEM((2, FOX_W, T), bf16),
                        pltpu.SemaphoreType.DMA((2, 2))],
    )
    return pl.pallas_call(
        _fox_prompt_kernel,
        grid_spec=grid_spec,
        out_shape=jax.ShapeDtypeStruct((B, L, FOX_W), bf16),
        compiler_params=pltpu.CompilerParams(dimension_semantics=("arbitrary", "arbitrary"),
                                             vmem_limit_bytes=VMEM_LIMIT),
        name="fox_prompt",
    )(start, fq, fkbt, fkbt, fkbt, fvbt, fvbt, fvbt, ct, fkbt, fvbt)


def _fox_sample_kernel(q_ref, kn_ref, vn_ref, kp_ref, vp_ref, ct_ref, o_ref, *, past, lq):
    H = FOX_HEADS
    r_i = lax.broadcasted_iota(jnp.int32, (lq, lq), 0)
    c_i = lax.broadcasted_iota(jnp.int32, (lq, lq), 1)
    causal = c_i <= r_i
    lane = lax.broadcasted_iota(jnp.int32, (1, LANES), 1)
    first_half = lane < FOX_DH
    probs = []
    for h in range(H):
        hm = first_half if h % 2 == 0 else jnp.logical_not(first_half)
        qp = q_ref[0, :, _pair_slab(h)]
        qm = jnp.where(hm, qp, jnp.zeros_like(qp))
        base = ct_ref[0, h:h + 1, past:past + 1]
        s_p = _dot(qm, kp_ref[0, _pair_slab(h), :].astype(bf16)) - (ct_ref[0, h:h + 1, 0:past] - base)
        s_n = _dot(qm, kn_ref[0, _pair_slab(h), :]) - (ct_ref[0, h:h + 1, past:past + lq] - base)
        s_n = jnp.where(causal, s_n, NEG_BIG)
        m = jnp.maximum(jnp.max(s_p, axis=1, keepdims=True), jnp.max(s_n, axis=1, keepdims=True))
        probs.append((jnp.exp2(s_p - m).astype(bf16), jnp.exp2(s_n - m).astype(bf16)))
    accs = []
    for h in range(H):
        p_p, p_n = probs[h]
        accs.append(_dot_nt(p_p, _v_aug(vp_ref[0, _head_rows(h), :].astype(bf16), h))
                    + _dot_nt(p_n, _v_aug(vn_ref[0, _head_rows(h), :], h)))
    for p in range(H // 2):
        numer = jnp.where(first_half, accs[2 * p], accs[2 * p + 1])
        denom = pltpu.roll(jnp.where(first_half, accs[2 * p + 1], accs[2 * p]), FOX_DH, 1)
        o_ref[0, :, p * LANES:(p + 1) * LANES] = (numer / denom).astype(bf16)


def _fox_sample(fq, fkbt, fvbt, kt_past, vt_past, ct_flat):
    B, lq, _ = fq.shape
    past = kt_past.shape[2]
    lpad = ct_flat.shape[2]
    new = pl.BlockSpec((1, FOX_W, lq), lambda b: (b, 0, 0))
    old = pl.BlockSpec((1, FOX_W, past), lambda b: (b, 0, 0))
    qo = pl.BlockSpec((1, lq, FOX_W), lambda b: (b, 0, 0))
    return pl.pallas_call(
        functools.partial(_fox_sample_kernel, past=past, lq=lq),
        grid=(B,),
        in_specs=[qo, new, new, old, old, pl.BlockSpec((1, FOX_HEADS, lpad), lambda b: (b, 0, 0))],
        out_specs=qo,
        out_shape=jax.ShapeDtypeStruct((B, lq, FOX_W), bf16),
        compiler_params=pltpu.CompilerParams(dimension_semantics=("arbitrary",), vmem_limit_bytes=VMEM_LIMIT),
        name="fox_sample",
    )(fq, fkbt, fvbt, kt_past, vt_past, ct_flat)


def _layer_norm(t, g, b):
    mu = jnp.mean(t, axis=-1, keepdims=True)
    d = t - mu
    var = jnp.mean(d * d, axis=-1, keepdims=True)
    return d * lax.rsqrt(var + LN_EPS) * g + b


def _mix_kernel(x_ref, og_ref, of_ref, wz_ref, bz_ref, wpg_ref, wpf_ref, wo_ref, g1_ref, b1_ref, x1_ref, *, alpha):
    x = x_ref[...]
    xb = x.astype(bf16)
    zg = _dot(xb, wz_ref[:, :D_MODEL]) + bz_ref[:, :D_MODEL]
    zf = _dot(xb, wz_ref[:, D_MODEL:]) + bz_ref[:, D_MODEL:]
    m = jax.nn.sigmoid(zg) * _dot(og_ref[...], wpg_ref[...]) + jax.nn.sigmoid(zf) * _dot(of_ref[...], wpf_ref[...])
    t = alpha * x + _dot(m.astype(bf16), wo_ref[...])
    x1_ref[...] = _layer_norm(t, g1_ref[...], b1_ref[...])


def _ffn_kernel(x1_ref, wg_ref, wu_ref, wd_ref, g2_ref, b2_ref, y_ref, *, alpha, bounds):
    x1 = x1_ref[...]
    xb = x1.astype(bf16)
    acc = None
    for lo, hi in zip(bounds[:-1], bounds[1:]):
        cs = slice(lo, hi)
        g = _dot(xb, wg_ref[:, cs])
        u = _dot(xb, wu_ref[:, cs])
        hid = (g * jax.nn.sigmoid(g) * u).astype(bf16)
        part = _dot(hid, wd_ref[cs, :])
        acc = part if acc is None else acc + part
    y_ref[...] = _layer_norm(alpha * x1 + acc, g2_ref[...], b2_ref[...])


def _mix(x2d, og, of, wz, bz, wpg, wpf, wo, g1, b1, alpha):
    n = x2d.shape[0]
    tm = min(512, n)
    row = lambda w: pl.BlockSpec((tm, w), lambda i: (i, 0))
    return pl.pallas_call(
        functools.partial(_mix_kernel, alpha=alpha),
        grid=(n // tm,),
        in_specs=[row(D_MODEL), row(GLA_V), row(FOX_W)] + [_const_spec(a.shape) for a in (wz, bz, wpg, wpf, wo, g1, b1)],
        out_specs=row(D_MODEL),
        out_shape=jax.ShapeDtypeStruct((n, D_MODEL), f32),
        compiler_params=pltpu.CompilerParams(dimension_semantics=("arbitrary",), vmem_limit_bytes=VMEM_LIMIT),
        name="mix",
    )(x2d, og, of, wz, bz, wpg, wpf, wo, g1, b1)


def _ffn(x1, wg, wu, wd, g2, b2, alpha):
    n = x1.shape[0]
    tm = min(512, n)
    row = lambda w: pl.BlockSpec((tm, w), lambda i: (i, 0))
    return pl.pallas_call(
        functools.partial(_ffn_kernel, alpha=alpha, bounds=FFN_BOUNDS),
        grid=(n // tm,),
        in_specs=[row(D_MODEL)] + [_const_spec(a.shape) for a in (wg, wu, wd, g2, b2)],
        out_specs=row(D_MODEL),
        out_shape=jax.ShapeDtypeStruct((n, D_MODEL), f32),
        compiler_params=pltpu.CompilerParams(dimension_semantics=("arbitrary",), vmem_limit_bytes=VMEM_LIMIT),
        name="ffn",
    )(x1, wg, wu, wd, g2, b2)


def _prep_weights(w_in, b_in, w_alpha2, b_alpha2, gla_norm_g, w_proj_gla, w_proj_fox, w_out,
                  ln1_g, ln1_b, w_ffn_gate, w_ffn_up, w_ffn_down, ln2_g, ln2_b):
    def cols(a, idx):
        return a[..., _OFF[idx]:_OFF[idx + 1]]

    order = (0, 1, 2, 4, 5, 8, 3)
    pad = SMALL_W - FOX_HEADS - GLA_RANK
    w1 = jnp.concatenate([cols(w_in, i) for i in order] + [jnp.zeros((D_MODEL, pad), f32)], axis=1).astype(bf16)
    b1 = jnp.concatenate([cols(b_in, i) for i in order] + [jnp.zeros((pad,), f32)])[None, :]
    wkv = jnp.transpose(w_in[:, _OFF[6]:_OFF[8]]).astype(bf16)
    bkv = jnp.broadcast_to(b_in[_OFF[6]:_OFF[8], None], (2 * FOX_W, LANES)).astype(f32)
    wa = jnp.zeros((SMALL_W, GLA_QK), f32).at[FOX_HEADS:FOX_HEADS + GLA_RANK].set(w_alpha2).astype(bf16)
    head_of = np.arange(FOX_W) // FOX_DH
    indq = np.zeros((FOX_W, SMALL_W), np.float32)
    indq[np.arange(FOX_W), FOX_HEADS + head_of] = 1.0
    row = lambda a: a[None, :].astype(f32)
    return dict(
        w1=w1, b1=b1, wkv=wkv, bkv=bkv, wa=wa, ba=row(b_alpha2), indq=jnp.asarray(indq, bf16),
        g=row(gla_norm_g),
        wz=w_in[:, _OFF[9]:].astype(bf16), bz=row(b_in[_OFF[9]:]),
        wpg=w_proj_gla.astype(bf16), wpf=w_proj_fox.astype(bf16), wo=w_out.astype(bf16),
        g1=row(ln1_g), b1n=row(ln1_b),
        wg=w_ffn_gate.astype(bf16), wu=w_ffn_up.astype(bf16), wd=w_ffn_down.astype(bf16),
        g2=row(ln2_g), b2n=row(ln2_b),
    )


def _time_on_lanes(a, lpad):
    B, L, W = a.shape
    a = jnp.transpose(a, (0, 2, 1))
    return jnp.pad(a, ((0, 0), (0, 0), (0, lpad - L)))


def _heads_last(t, B, L):
    return jnp.transpose(t.reshape(B, FOX_HEADS, FOX_DH, L), (0, 3, 1, 2))


def _layer(x, s0, k_past, v_past, lf_past, p, alpha):
    B, L, _ = x.shape
    n = B * L
    x2d = x.reshape(n, D_MODEL)
    prompt = k_past is None
    lo = L if prompt else n
    gq, gk, gv, la, sg, fq, fkt, fvt, fkbt, fvbt, aux, nk2t = _inproj(
        x2d, lo, p["w1"], p["b1"], p["wkv"], p["bkv"], p["wa"], p["ba"], p["indq"])
    r3 = lambda a: a.reshape(B, L, a.shape[-1])

    o_gla, st = _gla(r3(gq), r3(gk), r3(gv), r3(la), r3(sg), p["g"], _state_to_pairs(s0))
    gla_state = _pairs_to_state(st)

    aux3 = r3(aux)
    lf = aux3[:, :, :FOX_HEADS]
    if prompt:
        nc = L // LANES
        rows_t = _time_on_lanes(aux3[:, :, :2 * FOX_HEADS], L)
        aux_t = jnp.concatenate([rows_t, nk2t], axis=1).reshape(B, 3 * FOX_HEADS, nc, LANES)
        ct, start = _fox_meta(aux_t)
        o_fox = _fox_prompt(r3(fq), fkbt, fvbt, ct, start.reshape(B, nc))
        fk, fv = _heads_last(fkt, B, L), _heads_last(fvt, B, L)
    else:
        past = k_past.shape[1]
        lk = past + L
        lpad = -(-lk // (8 * LANES)) * (8 * LANES)
        lf_all = jnp.concatenate([lf_past.astype(f32), lf], axis=1)
        stats = jnp.concatenate([lf_all, jnp.zeros((B, lk, 2 * FOX_HEADS), f32)], axis=2)
        aux_t = _time_on_lanes(stats, lpad).reshape(B, 3 * FOX_HEADS, lpad // LANES, LANES)
        ct, _ = _fox_meta(aux_t)
        per_stream = lambda t: jnp.transpose(t.reshape(FOX_W, B, L), (1, 0, 2))
        cache_t = lambda c: jnp.transpose(c, (0, 2, 3, 1)).reshape(B, FOX_W, past)
        o_fox = _fox_sample(r3(fq), per_stream(fkbt), per_stream(fvbt), cache_t(k_past), cache_t(v_past),
                            ct.reshape(B, FOX_HEADS, lpad))
        fk, fv = _heads_last(per_stream(fkt), B, L), _heads_last(per_stream(fvt), B, L)

    x1 = _mix(x2d, o_gla.reshape(n, GLA_V), o_fox.reshape(n, FOX_W), p["wz"], p["bz"], p["wpg"], p["wpf"], p["wo"],
              p["g1"], p["b1n"], alpha)
    y = _ffn(x1, p["wg"], p["wu"], p["wd"], p["g2"], p["b2n"], alpha)
    return y.reshape(B, L, D_MODEL), gla_state, fk, fv, lf


def kernel(x_prompt, x_sample, state_gla, cache_fox_k, cache_fox_v, cache_fox_logf, w_in, b_in, w_alpha2, b_alpha2,
           gla_norm_g, w_proj_gla, w_proj_fox, w_out, ln1_g, ln1_b, w_ffn_gate, w_ffn_up, w_ffn_down, ln2_g, ln2_b):
    depth = w_in.shape[0]
    alpha = (2.0 * depth) ** 0.25
    yp, ys = x_prompt, x_sample
    outs_p, outs_s = [], []
    for l in range(depth):
        p = _prep_weights(w_in[l], b_in[l], w_alpha2[l], b_alpha2[l], gla_norm_g[l], w_proj_gla[l], w_proj_fox[l],
                          w_out[l], ln1_g[l], ln1_b[l], w_ffn_gate[l], w_ffn_up[l], w_ffn_down[l], ln2_g[l], ln2_b[l])
        s0 = jnp.zeros((yp.shape[0], GLA_HEADS, GLA_DK, GLA_DV), f32)
        yp, *rest_p = _layer(yp, s0, None, None, None, p, alpha)
        outs_p.append(rest_p)
        ys, *rest_s = _layer(ys, state_gla[l], cache_fox_k[l], cache_fox_v[l], cache_fox_logf[l], p, alpha)
        outs_s.append(rest_s)
    stack = lambda outs, i: jnp.stack([o[i] for o in outs])
    return (yp, ys,
            stack(outs_p, 0), stack(outs_p, 1), stack(outs_p, 2), stack(outs_p, 3),
            stack(outs_s, 0), stack(outs_s, 1), stack(outs_s, 2), stack(outs_s, 3))
```

```python
import functools

import jax
import jax.numpy as jnp
import numpy as np
from jax import lax
from jax.experimental import pallas as pl
from jax.experimental.pallas import tpu as pltpu

f32 = jnp.float32
bf16 = jnp.bfloat16

D_MODEL = 1024
GLA_HEADS = 4
GLA_DK = 64
GLA_DV = 128
GLA_RANK = 16
GLA_TAU = 16.0
GLA_QK = GLA_HEADS * GLA_DK
GLA_V = GLA_HEADS * GLA_DV
GLA_CHUNK = 64
FOX_HEADS = 8
FOX_DH = 64
FOX_W = FOX_HEADS * FOX_DH
D_FF = 2816
LN_EPS = 1e-5
RMS_EPS = 1e-5
SPLIT_SIZES = (GLA_QK, GLA_QK, GLA_V, GLA_RANK, GLA_V, FOX_W, FOX_W, FOX_W, FOX_HEADS, D_MODEL, D_MODEL)
_OFF = np.concatenate([[0], np.cumsum(SPLIT_SIZES)]).astype(int)

LANES = 128
MXU_DIM = 256
FFN_BOUNDS = (0, 4 * MXU_DIM, 8 * MXU_DIM, D_FF)
TOKEN_TILE = 1024
SMALL_W = LANES
ATT_BLOCK = 128
LOG2E = 1.4426950408889634
SKIP_T2 = 110.0 * LOG2E
NORM_SLACK = 1.02
NEG_BIG = -1e30
VMEM_LIMIT = 56 * 1024 * 1024


def _dot(a, b):
    return jnp.dot(a, b, preferred_element_type=f32)


def _dot_nt(a, b):
    return lax.dot_general(a, b, (((1,), (1,)), ((), ())), preferred_element_type=f32)


def _dot_tn(a, b):
    return lax.dot_general(a, b, (((0,), (0,)), ((), ())), preferred_element_type=f32)


def _dot_hi(a, b):
    return jnp.dot(a, b, preferred_element_type=f32, precision=lax.Precision.HIGHEST)


def _log_sigmoid(x):
    return jnp.minimum(x, 0.0) - jnp.log1p(jnp.exp(-jnp.abs(x)))


def _const_spec(shape):
    return pl.BlockSpec(shape, lambda *_: (0,) * len(shape), pipeline_mode=pl.Buffered(1))


def _inproj_kernel(x_ref, w_ref, b_ref, wkv_ref, bkv_ref, wa_ref, ba_ref, indq_ref,
                   gq_ref, gk_ref, gv_ref, la_ref, sg_ref, fq_ref, fkt_ref, fvt_ref,
                   fkbt_ref, fvbt_ref, aux_ref, nk2t_ref):
    xb = x_ref[...].astype(bf16)
    tm = x_ref.shape[0]

    def group(lo, width):
        return _dot(xb, w_ref[:, lo:lo + width]) + b_ref[:, lo:lo + width]

    o = 0
    gq_ref[...] = (group(o, GLA_QK) * (GLA_DK ** -0.5)).astype(bf16)
    o += GLA_QK
    gk_ref[...] = group(o, GLA_QK).astype(bf16)
    o += GLA_QK
    gv_ref[...] = group(o, GLA_V).astype(bf16)
    o += GLA_V
    gr = group(o, GLA_V)
    sg_ref[...] = (gr * jax.nn.sigmoid(gr)).astype(bf16)
    o += GLA_V
    fq = (group(o, FOX_W) * (FOX_DH ** -0.5 * LOG2E)).astype(bf16)
    fq_ref[...] = fq
    o += FOX_W
    small = group(o, SMALL_W)
    la_pre = _dot(small.astype(bf16), wa_ref[...]) + ba_ref[...]
    la_ref[...] = _log_sigmoid(la_pre) * (1.0 / GLA_TAU)
    lane = lax.broadcasted_iota(jnp.int32, (1, SMALL_W), 1)
    lf = jnp.where(lane < FOX_HEADS, _log_sigmoid(small), 0.0)
    fq32 = fq.astype(f32)
    nq2 = _dot((fq32 * fq32).astype(bf16), indq_ref[...])
    aux_ref[...] = lf + nq2

    bias_t = jnp.concatenate([bkv_ref[...]] * (tm // LANES), axis=1)
    for g, (t_ref, tb_ref) in enumerate(((fkt_ref, fkbt_ref), (fvt_ref, fvbt_ref))):
        rows = slice(g * FOX_W, (g + 1) * FOX_W)
        kv_t = _dot_nt(wkv_ref[rows, :], xb) + bias_t[rows]
        t_ref[0] = kv_t
        kvb = kv_t.astype(bf16)
        tb_ref[0] = kvb
        if g == 0:
            k32 = kvb.astype(f32)
            k2 = k32 * k32
            nk2t_ref[0] = jnp.concatenate(
                [jnp.sum(k2[h * FOX_DH:(h + 1) * FOX_DH], axis=0, keepdims=True) for h in range(FOX_HEADS)], axis=0)


def _inproj(x2d, lo, w1, b1, wkv, bkv, wa, ba, indq):
    n = x2d.shape[0]
    bo = n // lo
    tm = min(TOKEN_TILE, lo)
    assert lo % tm == 0 and tm % LANES == 0
    tpb = lo // tm
    row = lambda w: pl.BlockSpec((tm, w), lambda i: (i, 0))
    col = lambda w: pl.BlockSpec((1, w, tm), lambda i: (i // tpb, 0, i % tpb))
    rows_out = lambda w, dt: (jax.ShapeDtypeStruct((n, w), dt), row(w))
    cols_out = lambda w, dt: (jax.ShapeDtypeStruct((bo, w, lo), dt), col(w))
    outs = (
        rows_out(GLA_QK, bf16),
        rows_out(GLA_QK, bf16),
        rows_out(GLA_V, bf16),
        rows_out(GLA_QK, f32),
        rows_out(GLA_V, bf16),
        rows_out(FOX_W, bf16),
        cols_out(FOX_W, f32),
        cols_out(FOX_W, f32),
        cols_out(FOX_W, bf16),
        cols_out(FOX_W, bf16),
        rows_out(SMALL_W, f32),
        cols_out(FOX_HEADS, f32),
    )
    consts = (w1, b1, wkv, bkv, wa, ba, indq)
    return pl.pallas_call(
        _inproj_kernel,
        grid=(n // tm,),
        in_specs=[row(D_MODEL)] + [_const_spec(a.shape) for a in consts],
        out_specs=[o[1] for o in outs],
        out_shape=[o[0] for o in outs],
        compiler_params=pltpu.CompilerParams(dimension_semantics=("arbitrary",), vmem_limit_bytes=VMEM_LIMIT),
        name="inproj",
    )(x2d, *consts)


def _gla_kernel(q_ref, k_ref, v_ref, la_ref, sg_ref, g_ref, s0_ref, o_ref, sfin_ref, st_ref, *, chunk, nsub):
    t = pl.program_id(1)

    @pl.when(t == 0)
    def _():
        st_ref[...] = s0_ref[0]

    C = chunk
    tl = C * nsub
    r_i = lax.broadcasted_iota(jnp.int32, (C, C), 0)
    c_i = lax.broadcasted_iota(jnp.int32, (C, C), 1)
    causal = c_i <= r_i
    grp = min(tl, MXU_DIM)
    rt = lax.broadcasted_iota(jnp.int32, (grp, grp), 0)
    ct = lax.broadcasted_iota(jnp.int32, (grp, grp), 1)
    chunk_tril = ((ct <= rt) & (ct >= (rt // C) * C)).astype(bf16)
    lane = lax.broadcasted_iota(jnp.int32, (1, LANES), 1)
    first_half = lane < GLA_DK
    mid = C // 2
    pairs = range(GLA_HEADS // 2)

    def lanes(p):
        return slice(p * LANES, (p + 1) * LANES)

    def one_head(x, hh):
        return jnp.where(first_half if hh == 0 else jnp.logical_not(first_half), x, jnp.zeros_like(x))

    la = la_ref[0]
    la_hi = la.astype(bf16)
    la_lo = (la - la_hi.astype(f32)).astype(bf16)
    bcum_all = jnp.concatenate(
        [_dot(chunk_tril, la_hi[r:r + grp]) + _dot(chunk_tril, la_lo[r:r + grp]) for r in range(0, tl, grp)], axis=0)

    q_in, dec, amat, upd = [], [], {}, {}
    for c in range(nsub):
        rows = slice(c * C, (c + 1) * C)
        bcum = bcum_all[rows]
        b_last = bcum[C - 1:C, :]
        ref = bcum[mid:mid + 1, :]
        q = q_ref[0, rows, :].astype(f32)
        k = k_ref[0, rows, :].astype(f32)
        q_in.append((q * jnp.exp(bcum)).astype(bf16))
        q_a = (q * jnp.exp(jnp.clip(bcum - ref, -80.0, 80.0))).astype(bf16)
        k_a = (k * jnp.exp(jnp.clip(ref - bcum, -80.0, 80.0))).astype(bf16)
        k_d = (k * jnp.exp(b_last - bcum)).astype(bf16)
        dec.append(jnp.exp(b_last))
        for p in pairs:
            for hh in range(2):
                h = 2 * p + hh
                v_h = v_ref[0, rows, h * GLA_DV:(h + 1) * GLA_DV]
                a = _dot_nt(one_head(q_a[:, lanes(p)], hh), k_a[:, lanes(p)])
                amat[c, h] = jnp.where(causal, a, 0.0).astype(bf16)
                upd[c, h] = _dot_tn(v_h, k_d[:, lanes(p)])

    states = []
    st = [st_ref[p] for p in pairs]
    for c in range(nsub):
        states.append([s.astype(bf16) for s in st])
        st = [dec[c][:, lanes(p)] * st[p] + jnp.where(first_half, upd[c, 2 * p], upd[c, 2 * p + 1]) for p in pairs]
    for p in pairs:
        st_ref[p] = st[p]

    for c in range(nsub):
        rows = slice(c * C, (c + 1) * C)
        for h in range(GLA_HEADS):
            p, hh = divmod(h, 2)
            vs = slice(h * GLA_DV, (h + 1) * GLA_DV)
            o = _dot(amat[c, h], v_ref[0, rows, vs]) + _dot_nt(one_head(q_in[c][:, lanes(p)], hh), states[c][p])
            o = o * lax.rsqrt(jnp.mean(o * o, axis=-1, keepdims=True) + RMS_EPS)
            o = o * g_ref[:, vs] * sg_ref[0, rows, vs].astype(f32)
            o_ref[0, rows, vs] = o.astype(bf16)

    @pl.when(t == pl.num_programs(1) - 1)
    def _():
        sfin_ref[0] = st_ref[...]


def _gla(gq, gk, gv, la, sg, g, s0t):
    B, L, _ = gq.shape
    C = min(GLA_CHUNK, L)
    tl = min(512, L)
    assert L % tl == 0 and tl % C == 0
    tok = lambda w: pl.BlockSpec((1, tl, w), lambda b, t: (b, t, 0))
    st_spec = pl.BlockSpec((1, 2, GLA_DV, LANES), lambda b, t: (b, 0, 0, 0))
    return pl.pallas_call(
        functools.partial(_gla_kernel, chunk=C, nsub=tl // C),
        grid=(B, L // tl),
        in_specs=[tok(GLA_QK), tok(GLA_QK), tok(GLA_V), tok(GLA_QK), tok(GLA_V),
                  pl.BlockSpec((1, GLA_V), lambda b, t: (0, 0)), st_spec],
        out_specs=[tok(GLA_V), st_spec],
        out_shape=(jax.ShapeDtypeStruct((B, L, GLA_V), bf16),
                   jax.ShapeDtypeStruct((B, 2, GLA_DV, LANES), f32)),
        scratch_shapes=[pltpu.VMEM((2, GLA_DV, LANES), f32)],
        compiler_params=pltpu.CompilerParams(dimension_semantics=("arbitrary", "arbitrary"),
                                             vmem_limit_bytes=VMEM_LIMIT),
        name="gla_scan",
    )(gq, gk, gv, la, sg, g, s0t)


def _state_to_pairs(s):
    B = s.shape[0]
    s = s.reshape(B, 2, 2, GLA_DK, GLA_DV)
    return jnp.transpose(s, (0, 1, 4, 2, 3)).reshape(B, 2, GLA_DV, 2 * GLA_DK)


def _pairs_to_state(st):
    B = st.shape[0]
    st = st.reshape(B, 2, GLA_DV, 2, GLA_DK)
    return jnp.transpose(st, (0, 1, 3, 4, 2)).reshape(B, GLA_HEADS, GLA_DK, GLA_DV)


def _meta_kernel(a_ref, ct_ref, start_ref, *, nc):
    H = FOX_HEADS
    r_i = lax.broadcasted_iota(jnp.int32, (LANES, LANES), 0)
    c_i = lax.broadcasted_iota(jnp.int32, (LANES, LANES), 1)
    upper = (r_i <= c_i).astype(f32)
    rj = lax.broadcasted_iota(jnp.int32, (nc, nc), 0)
    cj = lax.broadcasted_iota(jnp.int32, (nc, nc), 1)
    strict_lower = (cj < rj).astype(f32)
    eye = rj == cj

    def to_row(col):
        return jnp.sum(jnp.where(eye, jnp.broadcast_to(col, (nc, nc)), 0.0), axis=0, keepdims=True)

    ys = [_dot_hi(a_ref[0, h], upper) for h in range(H)]
    prev = [_dot_hi(strict_lower, jnp.broadcast_to(y[:, LANES - 1:LANES], (nc, LANES))) for y in ys]
    start = None
    for h in range(H):
        c = (ys[h] + prev[h]) * LOG2E
        ct_ref[0, h] = c
        cmax = jnp.max(c, axis=1, keepdims=True)
        cmin = jnp.min(c, axis=1, keepdims=True)
        qmax = jnp.sqrt(jnp.max(a_ref[0, H + h], axis=1, keepdims=True))
        kmax = jnp.sqrt(jnp.max(jnp.max(a_ref[0, 2 * H + h], axis=1, keepdims=True), axis=0, keepdims=True))
        thr = cmax + (2.0 * NORM_SLACK) * qmax * kmax + SKIP_T2
        needed = (to_row(cmin) <= thr) & (cj <= rj)
        first = jnp.min(jnp.where(needed, cj, nc), axis=1, keepdims=True)
        start = first if start is None else jnp.minimum(start, first)
    start_ref[0] = to_row(start.astype(f32)).astype(jnp.int32)


def _fox_meta(aux_t):
    B, _, nc, _ = aux_t.shape
    return pl.pallas_call(
        functools.partial(_meta_kernel, nc=nc),
        grid=(B,),
        in_specs=[pl.BlockSpec((1, 3 * FOX_HEADS, nc, LANES), lambda b: (b, 0, 0, 0))],
        out_specs=[pl.BlockSpec((1, FOX_HEADS, nc, LANES), lambda b: (b, 0, 0, 0)),
                   pl.BlockSpec((1, 1, nc), lambda b: (b, 0, 0))],
        out_shape=(jax.ShapeDtypeStruct((B, FOX_HEADS, nc, LANES), f32),
                   jax.ShapeDtypeStruct((B, 1, nc), jnp.int32)),
        compiler_params=pltpu.CompilerParams(dimension_semantics=("arbitrary",), vmem_limit_bytes=VMEM_LIMIT),
        name="fox_meta",
    )(aux_t)


def _head_rows(h):
    return slice(h * FOX_DH, (h + 1) * FOX_DH)


def _pair_slab(h, width=LANES):
    return slice((h // 2) * width, (h // 2 + 1) * width)


def _v_aug(vt_h, h):
    one = jnp.ones_like(vt_h)
    return jnp.concatenate([vt_h, one] if h % 2 == 0 else [one, vt_h], axis=0)


def _fox_prompt_kernel(start_ref, q_ref, kp_ref, kc_ref, vp_ref, vc_ref,
                       ct_ref, kh_ref, vh_ref, o_ref, m_s, acc_s, kbuf, vbuf, sem):
    T = ATT_BLOCK
    TQ = 2 * T
    H = FOX_HEADS
    b = pl.program_id(0)
    i = pl.program_id(1)
    start = jnp.minimum(start_ref[b, 2 * i], start_ref[b, 2 * i + 1])
    r_i = lax.broadcasted_iota(jnp.int32, (TQ, 2 * TQ), 0)
    c_i = lax.broadcasted_iota(jnp.int32, (TQ, 2 * TQ), 1)
    visible = c_i <= r_i + TQ
    lane = lax.broadcasted_iota(jnp.int32, (1, LANES), 1)
    first_half = lane < FOX_DH
    jp = 2 * jnp.maximum(i - 1, 0)

    def head_q(h):
        hm = first_half if h % 2 == 0 else jnp.logical_not(first_half)
        qp = q_ref[0, :, _pair_slab(h)]
        return jnp.where(hm, qp, jnp.zeros_like(qp))

    def head_base(h):
        return ct_ref[0, h, pl.ds(2 * i, 1), 0:1]

    def c_row(h, j):
        return ct_ref[0, h, pl.ds(j, 1), :]

    scores = []
    for h in range(H):
        base = head_base(h)
        ck = jnp.concatenate([jnp.where(i >= 1, c_row(h, jp) - base, -NEG_BIG),
                              jnp.where(i >= 1, c_row(h, jp + 1) - base, -NEG_BIG),
                              c_row(h, 2 * i) - base, c_row(h, 2 * i + 1) - base], axis=1)
        kt = jnp.concatenate([kp_ref[0, _pair_slab(h), :], kc_ref[0, _pair_slab(h), :]], axis=1)
        s = jnp.where(visible, _dot(head_q(h), kt) - ck, NEG_BIG)
        m = jnp.max(s, axis=1, keepdims=True)
        m_s[h] = m
        scores.append((s, m))
    for h in range(H):
        s, m = scores[h]
        vt = jnp.concatenate([vp_ref[0, _head_rows(h), :], vc_ref[0, _head_rows(h), :]], axis=1)
        acc_s[h] = _dot_nt(jnp.exp2(s - m).astype(bf16), _v_aug(vt, h))

    n_far = jnp.maximum(2 * i - 2 - start, 0)

    def copies(j, slot):
        cols = pl.ds(pl.multiple_of(j * T, T), T)
        return (pltpu.make_async_copy(kh_ref.at[b, :, cols], kbuf.at[slot], sem.at[0, slot]),
                pltpu.make_async_copy(vh_ref.at[b, :, cols], vbuf.at[slot], sem.at[1, slot]))

    @pl.when(n_far > 0)
    def _():
        for cp in copies(start, 0):
            cp.start()

        def body(t, carry):
            j = start + t
            slot = lax.rem(t, 2)
            for cp in copies(j, slot):
                cp.wait()

            @pl.when(t + 1 < n_far)
            def _():
                for cp in copies(j + 1, 1 - slot):
                    cp.start()

            for h in range(H):
                ck = c_row(h, j) - head_base(h)
                s = _dot(head_q(h), kbuf[slot, _pair_slab(h), :]) - ck
                m_prev = m_s[h]
                m_new = jnp.maximum(m_prev, jnp.max(s, axis=1, keepdims=True))
                alpha = jnp.exp2(m_prev - m_new)
                p = jnp.exp2(s - m_new)
                acc_s[h] = alpha * acc_s[h] + _dot_nt(p.astype(bf16), _v_aug(vbuf[slot, _head_rows(h), :], h))
                m_s[h] = m_new
            return carry

        lax.fori_loop(0, n_far, body, 0)

    for p in range(H // 2):
        even = acc_s[2 * p]
        odd = acc_s[2 * p + 1]
        numer = jnp.where(first_half, even, odd)
        denom = pltpu.roll(jnp.where(first_half, odd, even), FOX_DH, 1)
        o_ref[0, :, p * LANES:(p + 1) * LANES] = (numer / denom).astype(bf16)


def _fox_prompt(fq, fkbt, fvbt, ct, start):
    B, L, _ = fq.shape
    T = ATT_BLOCK
    TQ = 2 * T
    assert L % TQ == 0
    kv = lambda d: pl.BlockSpec((1, FOX_W, TQ), lambda b, i, s: (b, 0, jnp.maximum(i - d, 0)))
    grid_spec = pltpu.PrefetchScalarGridSpec(
        num_scalar_prefetch=1,
        grid=(B, L // TQ),
        in_specs=[pl.BlockSpec((1, TQ, FOX_W), lambda b, i, s: (b, i, 0)),
                  kv(1), kv(0), kv(1), kv(0),
                  pl.BlockSpec((1, FOX_HEADS, L // T, LANES), lambda b, i, s: (b, 0, 0, 0)),
                  pl.BlockSpec(memory_space=pl.ANY), pl.BlockSpec(memory_space=pl.ANY)],
        out_specs=pl.BlockSpec((1, TQ, FOX_W), lambda b, i, s: (b, i, 0)),
        scratch_shapes=[pltpu.VMEM((FOX_HEADS, TQ, 1), f32),
                        pltpu.VMEM((FOX_HEADS, TQ, LANES), f32),
                        pltpu.VMEM((2, FOX_W, T), bf16), pltpu.VMEM((2, FOX_W, T), bf16),
                        pltpu.SemaphoreType.DMA((2, 2))],
    )
    return pl.pallas_call(
        _fox_prompt_kernel,
        grid_spec=grid_spec,
        out_shape=jax.ShapeDtypeStruct((B, L, FOX_W), bf16),
        compiler_params=pltpu.CompilerParams(dimension_semantics=("arbitrary", "arbitrary"),
                                             vmem_limit_bytes=VMEM_LIMIT),
        name="fox_prompt",
    )(start, fq, fkbt, fkbt, fvbt, fvbt, ct, fkbt, fvbt)


def _fox_sample_kernel(q_ref, kn_ref, vn_ref, kp_ref, vp_ref, ct_ref, o_ref, *, past, lq):
    H = FOX_HEADS
    r_i = lax.broadcasted_iota(jnp.int32, (lq, lq), 0)
    c_i = lax.broadcasted_iota(jnp.int32, (lq, lq), 1)
    causal = c_i <= r_i
    lane = lax.broadcasted_iota(jnp.int32, (1, LANES), 1)
    first_half = lane < FOX_DH
    probs = []
    for h in range(H):
        hm = first_half if h % 2 == 0 else jnp.logical_not(first_half)
        qp = q_ref[0, :, _pair_slab(h)]
        qm = jnp.where(hm, qp, jnp.zeros_like(qp))
        base = ct_ref[0, h:h + 1, past:past + 1]
        s_p = _dot(qm, kp_ref[0, _pair_slab(h), :].astype(bf16)) - (ct_ref[0, h:h + 1, 0:past] - base)
        s_n = _dot(qm, kn_ref[0, _pair_slab(h), :]) - (ct_ref[0, h:h + 1, past:past + lq] - base)
        s_n = jnp.where(causal, s_n, NEG_BIG)
        m = jnp.maximum(jnp.max(s_p, axis=1, keepdims=True), jnp.max(s_n, axis=1, keepdims=True))
        probs.append((jnp.exp2(s_p - m).astype(bf16), jnp.exp2(s_n - m).astype(bf16)))
    accs = []
    for h in range(H):
        p_p, p_n = probs[h]
        accs.append(_dot_nt(p_p, _v_aug(vp_ref[0, _head_rows(h), :].astype(bf16), h))
                    + _dot_nt(p_n, _v_aug(vn_ref[0, _head_rows(h), :], h)))
    for p in range(H // 2):
        numer = jnp.where(first_half, accs[2 * p], accs[2 * p + 1])
        denom = pltpu.roll(jnp.where(first_half, accs[2 * p + 1], accs[2 * p]), FOX_DH, 1)
        o_ref[0, :, p * LANES:(p + 1) * LANES] = (numer / denom).astype(bf16)


def _fox_sample(fq, fkbt, fvbt, kt_past, vt_past, ct_flat):
    B, lq, _ = fq.shape
    past = kt_past.shape[2]
    lpad = ct_flat.shape[2]
    new = pl.BlockSpec((1, FOX_W, lq), lambda b: (b, 0, 0))
    old = pl.BlockSpec((1, FOX_W, past), lambda b: (b, 0, 0))
    qo = pl.BlockSpec((1, lq, FOX_W), lambda b: (b, 0, 0))
    return pl.pallas_call(
        functools.partial(_fox_sample_kernel, past=past, lq=lq),
        grid=(B,),
        in_specs=[qo, new, new, old, old, pl.BlockSpec((1, FOX_HEADS, lpad), lambda b: (b, 0, 0))],
        out_specs=qo,
        out_shape=jax.ShapeDtypeStruct((B, lq, FOX_W), bf16),
        compiler_params=pltpu.CompilerParams(dimension_semantics=("arbitrary",), vmem_limit_bytes=VMEM_LIMIT),
        name="fox_sample",
    )(fq, fkbt, fvbt, kt_past, vt_past, ct_flat)


def _layer_norm(t, g, b):
    mu = jnp.mean(t, axis=-1, keepdims=True)
    d = t - mu
    var = jnp.mean(d * d, axis=-1, keepdims=True)
    return d * lax.rsqrt(var + LN_EPS) * g + b


def _mix_kernel(x_ref, og_ref, of_ref, wz_ref, bz_ref, wpg_ref, wpf_ref, wo_ref, g1_ref, b1_ref, x1_ref, *, alpha):
    x = x_ref[...]
    xb = x.astype(bf16)
    zg = _dot(xb, wz_ref[:, :D_MODEL]) + bz_ref[:, :D_MODEL]
    zf = _dot(xb, wz_ref[:, D_MODEL:]) + bz_ref[:, D_MODEL:]
    m = jax.nn.sigmoid(zg) * _dot(og_ref[...], wpg_ref[...]) + jax.nn.sigmoid(zf) * _dot(of_ref[...], wpf_ref[...])
    t = alpha * x + _dot(m.astype(bf16), wo_ref[...])
    x1_ref[...] = _layer_norm(t, g1_ref[...], b1_ref[...])


def _ffn_kernel(x1_ref, wg_ref, wu_ref, wd_ref, g2_ref, b2_ref, y_ref, *, alpha, bounds):
    x1 = x1_ref[...]
    xb = x1.astype(bf16)
    acc = None
    for lo, hi in zip(bounds[:-1], bounds[1:]):
        cs = slice(lo, hi)
        g = _dot(xb, wg_ref[:, cs])
        u = _dot(xb, wu_ref[:, cs])
        hid = (g * jax.nn.sigmoid(g) * u).astype(bf16)
        part = _dot(hid, wd_ref[cs, :])
        acc = part if acc is None else acc + part
    y_ref[...] = _layer_norm(alpha * x1 + acc, g2_ref[...], b2_ref[...])


def _mix(x2d, og, of, wz, bz, wpg, wpf, wo, g1, b1, alpha):
    n = x2d.shape[0]
    tm = min(TOKEN_TILE, n)
    row = lambda w: pl.BlockSpec((tm, w), lambda i: (i, 0))
    return pl.pallas_call(
        functools.partial(_mix_kernel, alpha=alpha),
        grid=(n // tm,),
        in_specs=[row(D_MODEL), row(GLA_V), row(FOX_W)] + [_const_spec(a.shape) for a in (wz, bz, wpg, wpf, wo, g1, b1)],
        out_specs=row(D_MODEL),
        out_shape=jax.ShapeDtypeStruct((n, D_MODEL), f32),
        compiler_params=pltpu.CompilerParams(dimension_semantics=("arbitrary",), vmem_limit_bytes=VMEM_LIMIT),
        name="mix",
    )(x2d, og, of, wz, bz, wpg, wpf, wo, g1, b1)


def _ffn(x1, wg, wu, wd, g2, b2, alpha):
    n = x1.shape[0]
    tm = min(TOKEN_TILE, n)
    row = lambda w: pl.BlockSpec((tm, w), lambda i: (i, 0))
    return pl.pallas_call(
        functools.partial(_ffn_kernel, alpha=alpha, bounds=FFN_BOUNDS),
        grid=(n // tm,),
        in_specs=[row(D_MODEL)] + [_const_spec(a.shape) for a in (wg, wu, wd, g2, b2)],
        out_specs=row(D_MODEL),
        out_shape=jax.ShapeDtypeStruct((n, D_MODEL), f32),
        compiler_params=pltpu.CompilerParams(dimension_semantics=("arbitrary",), vmem_limit_bytes=VMEM_LIMIT),
        name="ffn",
    )(x1, wg, wu, wd, g2, b2)


def _prep_weights(w_in, b_in, w_alpha2, b_alpha2, gla_norm_g, w_proj_gla, w_proj_fox, w_out,
                  ln1_g, ln1_b, w_ffn_gate, w_ffn_up, w_ffn_down, ln2_g, ln2_b):
    def cols(a, idx):
        return a[..., _OFF[idx]:_OFF[idx + 1]]

    order = (0, 1, 2, 4, 5, 8, 3)
    pad = SMALL_W - FOX_HEADS - GLA_RANK
    w1 = jnp.concatenate([cols(w_in, i) for i in order] + [jnp.zeros((D_MODEL, pad), f32)], axis=1).astype(bf16)
    b1 = jnp.concatenate([cols(b_in, i) for i in order] + [jnp.zeros((pad,), f32)])[None, :]
    wkv = jnp.transpose(w_in[:, _OFF[6]:_OFF[8]]).astype(bf16)
    bkv = jnp.broadcast_to(b_in[_OFF[6]:_OFF[8], None], (2 * FOX_W, LANES)).astype(f32)
    wa = jnp.zeros((SMALL_W, GLA_QK), f32).at[FOX_HEADS:FOX_HEADS + GLA_RANK].set(w_alpha2).astype(bf16)
    head_of = np.arange(FOX_W) // FOX_DH
    indq = np.zeros((FOX_W, SMALL_W), np.float32)
    indq[np.arange(FOX_W), FOX_HEADS + head_of] = 1.0
    row = lambda a: a[None, :].astype(f32)
    return dict(
        w1=w1, b1=b1, wkv=wkv, bkv=bkv, wa=wa, ba=row(b_alpha2), indq=jnp.asarray(indq, bf16),
        g=row(gla_norm_g),
        wz=w_in[:, _OFF[9]:].astype(bf16), bz=row(b_in[_OFF[9]:]),
        wpg=w_proj_gla.astype(bf16), wpf=w_proj_fox.astype(bf16), wo=w_out.astype(bf16),
        g1=row(ln1_g), b1n=row(ln1_b),
        wg=w_ffn_gate.astype(bf16), wu=w_ffn_up.astype(bf16), wd=w_ffn_down.astype(bf16),
        g2=row(ln2_g), b2n=row(ln2_b),
    )


def _time_on_lanes(a, lpad):
    B, L, W = a.shape
    a = jnp.transpose(a, (0, 2, 1))
    return jnp.pad(a, ((0, 0), (0, 0), (0, lpad - L)))


def _heads_last(t, B, L):
    return jnp.transpose(t.reshape(B, FOX_HEADS, FOX_DH, L), (0, 3, 1, 2))


def _layer(x, s0, k_past, v_past, lf_past, p, alpha):
    B, L, _ = x.shape
    n = B * L
    x2d = x.reshape(n, D_MODEL)
    prompt = k_past is None
    lo = L if prompt else n
    gq, gk, gv, la, sg, fq, fkt, fvt, fkbt, fvbt, aux, nk2t = _inproj(
        x2d, lo, p["w1"], p["b1"], p["wkv"], p["bkv"], p["wa"], p["ba"], p["indq"])
    r3 = lambda a: a.reshape(B, L, a.shape[-1])

    o_gla, st = _gla(r3(gq), r3(gk), r3(gv), r3(la), r3(sg), p["g"], _state_to_pairs(s0))
    gla_state = _pairs_to_state(st)

    aux3 = r3(aux)
    lf = aux3[:, :, :FOX_HEADS]
    if prompt:
        nc = L // LANES
        rows_t = _time_on_lanes(aux3[:, :, :2 * FOX_HEADS], L)
        aux_t = jnp.concatenate([rows_t, nk2t], axis=1).reshape(B, 3 * FOX_HEADS, nc, LANES)
        ct, start = _fox_meta(aux_t)
        o_fox = _fox_prompt(r3(fq), fkbt, fvbt, ct, start.reshape(B, nc))
        fk, fv = _heads_last(fkt, B, L), _heads_last(fvt, B, L)
    else:
        past = k_past.shape[1]
        lk = past + L
        lpad = -(-lk // (8 * LANES)) * (8 * LANES)
        lf_all = jnp.concatenate([lf_past.astype(f32), lf], axis=1)
        stats = jnp.concatenate([lf_all, jnp.zeros((B, lk, 2 * FOX_HEADS), f32)], axis=2)
        aux_t = _time_on_lanes(stats, lpad).reshape(B, 3 * FOX_HEADS, lpad // LANES, LANES)
        ct, _ = _fox_meta(aux_t)
        per_stream = lambda t: jnp.transpose(t.reshape(FOX_W, B, L), (1, 0, 2))
        cache_t = lambda c: jnp.transpose(c, (0, 2, 3, 1)).reshape(B, FOX_W, past)
        o_fox = _fox_sample(r3(fq), per_stream(fkbt), per_stream(fvbt), cache_t(k_past), cache_t(v_past),
                            ct.reshape(B, FOX_HEADS, lpad))
        fk, fv = _heads_last(per_stream(fkt), B, L), _heads_last(per_stream(fvt), B, L)

    x1 = _mix(x2d, o_gla.reshape(n, GLA_V), o_fox.reshape(n, FOX_W), p["wz"], p["bz"], p["wpg"], p["wpf"], p["wo"],
              p["g1"], p["b1n"], alpha)
    y = _ffn(x1, p["wg"], p["wu"], p["wd"], p["g2"], p["b2n"], alpha)
    return y.reshape(B, L, D_MODEL), gla_state, fk, fv, lf


def kernel(x_prompt, x_sample, state_gla, cache_fox_k, cache_fox_v, cache_fox_logf, w_in, b_in, w_alpha2, b_alpha2,
           gla_norm_g, w_proj_gla, w_proj_fox, w_out, ln1_g, ln1_b, w_ffn_gate, w_ffn_up, w_ffn_down, ln2_g, ln2_b):
    depth = w_in.shape[0]
    alpha = (2.0 * depth) ** 0.25
    yp, ys = x_prompt, x_sample
    outs_p, outs_s = [], []
    for l in range(depth):
        p = _prep_weights(w_in[l], b_in[l], w_alpha2[l], b_alpha2[l], gla_norm_g[l], w_proj_gla[l], w_proj_fox[l],
                          w_out[l], ln1_g[l], ln1_b[l], w_ffn_gate[l], w_ffn_up[l], w_ffn_down[l], ln2_g[l], ln2_b[l])
        s0 = jnp.zeros((yp.shape[0], GLA_HEADS, GLA_DK, GLA_DV), f32)
        yp, *rest_p = _layer(yp, s0, None, None, None, p, alpha)
        outs_p.append(rest_p)
        ys, *rest_s = _layer(ys, state_gla[l], cache_fox_k[l], cache_fox_v[l], cache_fox_logf[l], p, alpha)
        outs_s.append(rest_s)
    stack = lambda outs, i: jnp.stack([o[i] for o in outs])
    return (yp, ys,
            stack(outs_p, 0), stack(outs_p, 1), stack(outs_p, 2), stack(outs_p, 3),
            stack(outs_s, 0), stack(outs_s, 1), stack(outs_s, 2), stack(outs_s, 3))
```

```python
import functools

import jax
import jax.numpy as jnp
import numpy as np
from jax import lax
from jax.experimental import pallas as pl
from jax.experimental.pallas import tpu as pltpu

f32 = jnp.float32
bf16 = jnp.bfloat16

D_MODEL = 1024
GLA_HEADS = 4
GLA_DK = 64
GLA_DV = 128
GLA_RANK = 16
GLA_TAU = 16.0
GLA_QK = GLA_HEADS * GLA_DK
GLA_V = GLA_HEADS * GLA_DV
GLA_CHUNK = 64
FOX_HEADS = 8
FOX_DH = 64
FOX_W = FOX_HEADS * FOX_DH
D_FF = 2816
LN_EPS = 1e-5
RMS_EPS = 1e-5
SPLIT_SIZES = (GLA_QK, GLA_QK, GLA_V, GLA_RANK, GLA_V, FOX_W, FOX_W, FOX_W, FOX_HEADS, D_MODEL, D_MODEL)
_OFF = np.concatenate([[0], np.cumsum(SPLIT_SIZES)]).astype(int)

LANES = 128
MXU_DIM = 256
FFN_BOUNDS = (0, 4 * MXU_DIM, 8 * MXU_DIM, D_FF)
TOKEN_TILE = 1024
SMALL_W = LANES
FF_ROWS = 16
ATT_BLOCK = 128
LOG2E = 1.4426950408889634
SKIP_T2 = 110.0 * LOG2E
NORM_SLACK = 1.02
NEG_BIG = -1e30
VMEM_LIMIT = 56 * 1024 * 1024


def _dot(a, b):
    return jnp.dot(a, b, preferred_element_type=f32)


def _dot_nt(a, b):
    return lax.dot_general(a, b, (((1,), (1,)), ((), ())), preferred_element_type=f32)


def _dot_tn(a, b):
    return lax.dot_general(a, b, (((0,), (0,)), ((), ())), preferred_element_type=f32)


def _dot_hi(a, b):
    return jnp.dot(a, b, preferred_element_type=f32, precision=lax.Precision.HIGHEST)


def _log_sigmoid(x):
    return jnp.minimum(x, 0.0) - jnp.log1p(jnp.exp(-jnp.abs(x)))


def _const_spec(shape):
    return pl.BlockSpec(shape, lambda *_: (0,) * len(shape), pipeline_mode=pl.Buffered(1))


def _inproj_kernel(x_ref, w_ref, b_ref, wkv_ref, bkv_ref, wa_ref, ba_ref, indq_ref,
                   gq_ref, gk_ref, gv_ref, la_ref, sg_ref, fq_ref, fkt_ref, fvt_ref,
                   fkbt_ref, fvbt_ref, lft_ref, nk2t_ref, qmx_ref):
    xb = x_ref[...].astype(bf16)
    tm = x_ref.shape[0]

    def group(lo, width):
        return _dot(xb, w_ref[:, lo:lo + width]) + b_ref[:, lo:lo + width]

    o_gq, o_gk, o_gv = 0, GLA_QK, 2 * GLA_QK
    o_gr = o_gv + GLA_V
    o_fq = o_gr + GLA_V
    o_small = o_fq + FOX_W
    small_b = group(o_small, SMALL_W).astype(bf16)
    fq = (group(o_fq, FOX_W) * (FOX_DH ** -0.5 * LOG2E)).astype(bf16)
    fq_ref[...] = fq
    fq32 = fq.astype(f32)
    fq2 = (fq32 * fq32).astype(bf16)
    gq_ref[...] = (group(o_gq, GLA_QK) * (GLA_DK ** -0.5)).astype(bf16)
    gk_ref[...] = group(o_gk, GLA_QK).astype(bf16)
    gv_ref[...] = group(o_gv, GLA_V).astype(bf16)
    la_pre = _dot(small_b, wa_ref[...]) + ba_ref[...]
    la_ref[...] = _log_sigmoid(la_pre) * (1.0 / GLA_TAU)
    nq2 = _dot(fq2, indq_ref[...])
    qmx_ref[...] = jnp.max(nq2.reshape(tm // ATT_BLOCK, ATT_BLOCK, SMALL_W), axis=1)
    gr = group(o_gr, GLA_V)
    sg_ref[...] = (gr * jax.nn.sigmoid(gr)).astype(bf16)

    bias_t = jnp.concatenate([bkv_ref[...]] * (tm // LANES), axis=1)
    ff_rows = slice(2 * FOX_W, 2 * FOX_W + FF_ROWS)
    lft_ref[0] = _log_sigmoid(_dot_nt(wkv_ref[ff_rows, :], xb) + bias_t[ff_rows])[:FOX_HEADS]
    for g, (t_ref, tb_ref) in enumerate(((fkt_ref, fkbt_ref), (fvt_ref, fvbt_ref))):
        rows = slice(g * FOX_W, (g + 1) * FOX_W)
        kv_t = _dot_nt(wkv_ref[rows, :], xb) + bias_t[rows]
        t_ref[0] = kv_t
        kvb = kv_t.astype(bf16)
        tb_ref[0] = kvb
        if g == 0:
            k32 = kvb.astype(f32)
            k2 = k32 * k32
            nk2t_ref[0] = jnp.concatenate(
                [jnp.sum(k2[h * FOX_DH:(h + 1) * FOX_DH], axis=0, keepdims=True) for h in range(FOX_HEADS)], axis=0)


def _inproj(x2d, lo, w1, b1, wkv, bkv, wa, ba, indq):
    n = x2d.shape[0]
    bo = n // lo
    tm = min(TOKEN_TILE, lo)
    assert lo % tm == 0 and tm % LANES == 0
    tpb = lo // tm
    row = lambda w: pl.BlockSpec((tm, w), lambda i: (i, 0))
    col = lambda w: pl.BlockSpec((1, w, tm), lambda i: (i // tpb, 0, i % tpb))
    rows_out = lambda w, dt: (jax.ShapeDtypeStruct((n, w), dt), row(w))
    cols_out = lambda w, dt: (jax.ShapeDtypeStruct((bo, w, lo), dt), col(w))
    outs = (
        rows_out(GLA_QK, bf16),
        rows_out(GLA_QK, bf16),
        rows_out(GLA_V, bf16),
        rows_out(GLA_QK, f32),
        rows_out(GLA_V, bf16),
        rows_out(FOX_W, bf16),
        cols_out(FOX_W, f32),
        cols_out(FOX_W, f32),
        cols_out(FOX_W, bf16),
        cols_out(FOX_W, bf16),
        cols_out(FOX_HEADS, f32),
        cols_out(FOX_HEADS, f32),
        (jax.ShapeDtypeStruct((n // ATT_BLOCK, SMALL_W), f32),
         pl.BlockSpec((tm // ATT_BLOCK, SMALL_W), lambda i: (i, 0))),
    )
    consts = (w1, b1, wkv, bkv, wa, ba, indq)
    return pl.pallas_call(
        _inproj_kernel,
        grid=(n // tm,),
        in_specs=[row(D_MODEL)] + [_const_spec(a.shape) for a in consts],
        out_specs=[o[1] for o in outs],
        out_shape=[o[0] for o in outs],
        compiler_params=pltpu.CompilerParams(dimension_semantics=("arbitrary",), vmem_limit_bytes=VMEM_LIMIT),
        name="inproj",
    )(x2d, *consts)


def _gla_kernel(q_ref, k_ref, v_ref, la_ref, sg_ref, g_ref, s0_ref, o_ref, sfin_ref, st_ref, *, chunk, nsub):
    t = pl.program_id(1)

    @pl.when(t == 0)
    def _():
        st_ref[...] = s0_ref[0]

    C = chunk
    tl = C * nsub
    r_i = lax.broadcasted_iota(jnp.int32, (C, C), 0)
    c_i = lax.broadcasted_iota(jnp.int32, (C, C), 1)
    causal = c_i <= r_i
    grp = min(tl, MXU_DIM)
    rt = lax.broadcasted_iota(jnp.int32, (grp, grp), 0)
    ct = lax.broadcasted_iota(jnp.int32, (grp, grp), 1)
    chunk_tril = ((ct <= rt) & (ct >= (rt // C) * C)).astype(bf16)
    lane = lax.broadcasted_iota(jnp.int32, (1, LANES), 1)
    first_half = lane < GLA_DK
    mid = C // 2
    pairs = range(GLA_HEADS // 2)

    def lanes(p):
        return slice(p * LANES, (p + 1) * LANES)

    def one_head(x, hh):
        return jnp.where(first_half if hh == 0 else jnp.logical_not(first_half), x, jnp.zeros_like(x))

    la = la_ref[0]
    la_hi = la.astype(bf16)
    la_lo = (la - la_hi.astype(f32)).astype(bf16)
    bcum_all = jnp.concatenate(
        [_dot(chunk_tril, la_hi[r:r + grp]) + _dot(chunk_tril, la_lo[r:r + grp]) for r in range(0, tl, grp)], axis=0)

    q_in, dec, amat, upd = [], [], {}, {}
    for c in range(nsub):
        rows = slice(c * C, (c + 1) * C)
        bcum = bcum_all[rows]
        b_last = bcum[C - 1:C, :]
        ref = bcum[mid:mid + 1, :]
        q = q_ref[0, rows, :].astype(f32)
        k = k_ref[0, rows, :].astype(f32)
        q_in.append((q * jnp.exp(bcum)).astype(bf16))
        q_a = (q * jnp.exp(jnp.clip(bcum - ref, -80.0, 80.0))).astype(bf16)
        k_a = (k * jnp.exp(jnp.clip(ref - bcum, -80.0, 80.0))).astype(bf16)
        k_d = (k * jnp.exp(b_last - bcum)).astype(bf16)
        dec.append(jnp.exp(b_last))
        for p in pairs:
            for hh in range(2):
                h = 2 * p + hh
                v_h = v_ref[0, rows, h * GLA_DV:(h + 1) * GLA_DV]
                a = _dot_nt(one_head(q_a[:, lanes(p)], hh), k_a[:, lanes(p)])
                amat[c, h] = jnp.where(causal, a, 0.0).astype(bf16)
                upd[c, h] = _dot_tn(v_h, k_d[:, lanes(p)])

    states = []
    st = [st_ref[p] for p in pairs]
    for c in range(nsub):
        states.append([s.astype(bf16) for s in st])
        st = [dec[c][:, lanes(p)] * st[p] + jnp.where(first_half, upd[c, 2 * p], upd[c, 2 * p + 1]) for p in pairs]
    for p in pairs:
        st_ref[p] = st[p]

    for c in range(nsub):
        rows = slice(c * C, (c + 1) * C)
        for h in range(GLA_HEADS):
            p, hh = divmod(h, 2)
            vs = slice(h * GLA_DV, (h + 1) * GLA_DV)
            o = _dot(amat[c, h], v_ref[0, rows, vs]) + _dot_nt(one_head(q_in[c][:, lanes(p)], hh), states[c][p])
            o = o * lax.rsqrt(jnp.mean(o * o, axis=-1, keepdims=True) + RMS_EPS)
            o = o * g_ref[:, vs] * sg_ref[0, rows, vs].astype(f32)
            o_ref[0, rows, vs] = o.astype(bf16)

    @pl.when(t == pl.num_programs(1) - 1)
    def _():
        sfin_ref[0] = st_ref[...]


def _gla(gq, gk, gv, la, sg, g, s0t):
    B, L, _ = gq.shape
    C = min(GLA_CHUNK, L)
    tl = min(512, L)
    assert L % tl == 0 and tl % C == 0
    tok = lambda w: pl.BlockSpec((1, tl, w), lambda b, t: (b, t, 0))
    st_spec = pl.BlockSpec((1, 2, GLA_DV, LANES), lambda b, t: (b, 0, 0, 0))
    return pl.pallas_call(
        functools.partial(_gla_kernel, chunk=C, nsub=tl // C),
        grid=(B, L // tl),
        in_specs=[tok(GLA_QK), tok(GLA_QK), tok(GLA_V), tok(GLA_QK), tok(GLA_V),
                  pl.BlockSpec((1, GLA_V), lambda b, t: (0, 0)), st_spec],
        out_specs=[tok(GLA_V), st_spec],
        out_shape=(jax.ShapeDtypeStruct((B, L, GLA_V), bf16),
                   jax.ShapeDtypeStruct((B, 2, GLA_DV, LANES), f32)),
        scratch_shapes=[pltpu.VMEM((2, GLA_DV, LANES), f32)],
        compiler_params=pltpu.CompilerParams(dimension_semantics=("arbitrary", "arbitrary"),
                                             vmem_limit_bytes=VMEM_LIMIT),
        name="gla_scan",
    )(gq, gk, gv, la, sg, g, s0t)


def _state_to_pairs(s):
    B = s.shape[0]
    s = s.reshape(B, 2, 2, GLA_DK, GLA_DV)
    return jnp.transpose(s, (0, 1, 4, 2, 3)).reshape(B, 2, GLA_DV, 2 * GLA_DK)


def _pairs_to_state(st):
    B = st.shape[0]
    st = st.reshape(B, 2, GLA_DV, 2, GLA_DK)
    return jnp.transpose(st, (0, 1, 3, 4, 2)).reshape(B, GLA_HEADS, GLA_DK, GLA_DV)


def _meta_kernel(lf_ref, nk_ref, qm_ref, ct_ref, start_ref, *, nc):
    H = FOX_HEADS
    r_i = lax.broadcasted_iota(jnp.int32, (LANES, LANES), 0)
    c_i = lax.broadcasted_iota(jnp.int32, (LANES, LANES), 1)
    upper = (r_i <= c_i).astype(f32)
    rj = lax.broadcasted_iota(jnp.int32, (nc, nc), 0)
    cj = lax.broadcasted_iota(jnp.int32, (nc, nc), 1)
    strict_lower = (cj < rj).astype(f32)
    eye = rj == cj

    def to_row(col):
        return jnp.sum(jnp.where(eye, jnp.broadcast_to(col, (nc, nc)), 0.0), axis=0, keepdims=True)

    ys = [_dot_hi(lf_ref[0, h], upper) for h in range(H)]
    prev = [_dot_hi(strict_lower, jnp.broadcast_to(y[:, LANES - 1:LANES], (nc, LANES))) for y in ys]
    start = None
    for h in range(H):
        c = (ys[h] + prev[h]) * LOG2E
        ct_ref[0, h] = c
        cmax = jnp.max(c, axis=1, keepdims=True)
        cmin = jnp.min(c, axis=1, keepdims=True)
        qmax = jnp.sqrt(qm_ref[0, :, H + h:H + h + 1])
        kmax = jnp.sqrt(jnp.max(jnp.max(nk_ref[0, h], axis=1, keepdims=True), axis=0, keepdims=True))
        thr = cmax + (2.0 * NORM_SLACK) * qmax * kmax + SKIP_T2
        needed = (to_row(cmin) <= thr) & (cj <= rj)
        first = jnp.min(jnp.where(needed, cj, nc), axis=1, keepdims=True)
        start = first if start is None else jnp.minimum(start, first)
    start_ref[0] = to_row(start.astype(f32)).astype(jnp.int32)


def _fox_meta(lf_t, nk2_t, qmx):
    B, _, nc, _ = lf_t.shape
    per_head = pl.BlockSpec((1, FOX_HEADS, nc, LANES), lambda b: (b, 0, 0, 0))
    return pl.pallas_call(
        functools.partial(_meta_kernel, nc=nc),
        grid=(B,),
        in_specs=[per_head, per_head, pl.BlockSpec((1, nc, LANES), lambda b: (b, 0, 0))],
        out_specs=[pl.BlockSpec((1, FOX_HEADS, nc, LANES), lambda b: (b, 0, 0, 0)),
                   pl.BlockSpec((1, 1, nc), lambda b: (b, 0, 0))],
        out_shape=(jax.ShapeDtypeStruct((B, FOX_HEADS, nc, LANES), f32),
                   jax.ShapeDtypeStruct((B, 1, nc), jnp.int32)),
        compiler_params=pltpu.CompilerParams(dimension_semantics=("arbitrary",), vmem_limit_bytes=VMEM_LIMIT),
        name="fox_meta",
    )(lf_t, nk2_t, qmx)


def _head_rows(h):
    return slice(h * FOX_DH, (h + 1) * FOX_DH)


def _pair_slab(h, width=LANES):
    return slice((h // 2) * width, (h // 2 + 1) * width)


def _v_aug(vt_h, h):
    one = jnp.ones_like(vt_h)
    return jnp.concatenate([vt_h, one] if h % 2 == 0 else [one, vt_h], axis=0)


def _fox_prompt_kernel(start_ref, q_ref, kp_ref, kc_ref, vp_ref, vc_ref,
                       ct_ref, kh_ref, vh_ref, o_ref, m_s, acc_s, kbuf, vbuf, sem):
    T = ATT_BLOCK
    TQ = 2 * T
    H = FOX_HEADS
    b = pl.program_id(0)
    i = pl.program_id(1)
    start = jnp.minimum(start_ref[b, 2 * i], start_ref[b, 2 * i + 1])
    r_i = lax.broadcasted_iota(jnp.int32, (TQ, 2 * TQ), 0)
    c_i = lax.broadcasted_iota(jnp.int32, (TQ, 2 * TQ), 1)
    visible = c_i <= r_i + TQ
    lane = lax.broadcasted_iota(jnp.int32, (1, LANES), 1)
    first_half = lane < FOX_DH
    jp = 2 * jnp.maximum(i - 1, 0)

    def head_q(h):
        hm = first_half if h % 2 == 0 else jnp.logical_not(first_half)
        qp = q_ref[0, :, _pair_slab(h)]
        return jnp.where(hm, qp, jnp.zeros_like(qp))

    def head_base(h):
        return ct_ref[0, h, pl.ds(2 * i, 1), 0:1]

    def c_row(h, j):
        return ct_ref[0, h, pl.ds(j, 1), :]

    scores = []
    for h in range(H):
        base = head_base(h)
        ck = jnp.concatenate([jnp.where(i >= 1, c_row(h, jp) - base, -NEG_BIG),
                              jnp.where(i >= 1, c_row(h, jp + 1) - base, -NEG_BIG),
                              c_row(h, 2 * i) - base, c_row(h, 2 * i + 1) - base], axis=1)
        kt = jnp.concatenate([kp_ref[0, _pair_slab(h), :], kc_ref[0, _pair_slab(h), :]], axis=1)
        s = jnp.where(visible, _dot(head_q(h), kt) - ck, NEG_BIG)
        m = jnp.max(s, axis=1, keepdims=True)
        m_s[h] = m
        scores.append((s, m))
    for h in range(H):
        s, m = scores[h]
        vt = jnp.concatenate([vp_ref[0, _head_rows(h), :], vc_ref[0, _head_rows(h), :]], axis=1)
        acc_s[h] = _dot_nt(jnp.exp2(s - m).astype(bf16), _v_aug(vt, h))

    n_far = jnp.maximum(2 * i - 2 - start, 0)

    def copies(j, slot):
        cols = pl.ds(pl.multiple_of(j * T, T), T)
        return (pltpu.make_async_copy(kh_ref.at[b, :, cols], kbuf.at[slot], sem.at[0, slot]),
                pltpu.make_async_copy(vh_ref.at[b, :, cols], vbuf.at[slot], sem.at[1, slot]))

    @pl.when(n_far > 0)
    def _():
        for cp in copies(start, 0):
            cp.start()

        def body(t, carry):
            j = start + t
            slot = lax.rem(t, 2)
            for cp in copies(j, slot):
                cp.wait()

            @pl.when(t + 1 < n_far)
            def _():
                for cp in copies(j + 1, 1 - slot):
                    cp.start()

            for h in range(H):
                ck = c_row(h, j) - head_base(h)
                s = _dot(head_q(h), kbuf[slot, _pair_slab(h), :]) - ck
                m_prev = m_s[h]
                m_new = jnp.maximum(m_prev, jnp.max(s, axis=1, keepdims=True))
                alpha = jnp.exp2(m_prev - m_new)
                p = jnp.exp2(s - m_new)
                acc_s[h] = alpha * acc_s[h] + _dot_nt(p.astype(bf16), _v_aug(vbuf[slot, _head_rows(h), :], h))
                m_s[h] = m_new
            return carry

        lax.fori_loop(0, n_far, body, 0)

    for p in range(H // 2):
        even = acc_s[2 * p]
        odd = acc_s[2 * p + 1]
        numer = jnp.where(first_half, even, odd)
        denom = pltpu.roll(jnp.where(first_half, odd, even), FOX_DH, 1)
        o_ref[0, :, p * LANES:(p + 1) * LANES] = (numer / denom).astype(bf16)


def _fox_prompt(fq, fkbt, fvbt, ct, start):
    B, L, _ = fq.shape
    T = ATT_BLOCK
    TQ = 2 * T
    assert L % TQ == 0
    kv = lambda d: pl.BlockSpec((1, FOX_W, TQ), lambda b, i, s: (b, 0, jnp.maximum(i - d, 0)))
    grid_spec = pltpu.PrefetchScalarGridSpec(
        num_scalar_prefetch=1,
        grid=(B, L // TQ),
        in_specs=[pl.BlockSpec((1, TQ, FOX_W), lambda b, i, s: (b, i, 0)),
                  kv(1), kv(0), kv(1), kv(0),
                  pl.BlockSpec((1, FOX_HEADS, L // T, LANES), lambda b, i, s: (b, 0, 0, 0)),
                  pl.BlockSpec(memory_space=pl.ANY), pl.BlockSpec(memory_space=pl.ANY)],
        out_specs=pl.BlockSpec((1, TQ, FOX_W), lambda b, i, s: (b, i, 0)),
        scratch_shapes=[pltpu.VMEM((FOX_HEADS, TQ, 1), f32),
                        pltpu.VMEM((FOX_HEADS, TQ, LANES), f32),
                        pltpu.VMEM((2, FOX_W, T), bf16), pltpu.VMEM((2, FOX_W, T), bf16),
                        pltpu.SemaphoreType.DMA((2, 2))],
    )
    return pl.pallas_call(
        _fox_prompt_kernel,
        grid_spec=grid_spec,
        out_shape=jax.ShapeDtypeStruct((B, L, FOX_W), bf16),
        compiler_params=pltpu.CompilerParams(dimension_semantics=("arbitrary", "arbitrary"),
                                             vmem_limit_bytes=VMEM_LIMIT),
        name="fox_prompt",
    )(start, fq, fkbt, fkbt, fvbt, fvbt, ct, fkbt, fvbt)


def _fox_sample_kernel(q_ref, kn_ref, vn_ref, kp_ref, vp_ref, ct_ref, o_ref, *, past, lq):
    H = FOX_HEADS
    r_i = lax.broadcasted_iota(jnp.int32, (lq, lq), 0)
    c_i = lax.broadcasted_iota(jnp.int32, (lq, lq), 1)
    causal = c_i <= r_i
    lane = lax.broadcasted_iota(jnp.int32, (1, LANES), 1)
    first_half = lane < FOX_DH
    probs = []
    for h in range(H):
        hm = first_half if h % 2 == 0 else jnp.logical_not(first_half)
        qp = q_ref[0, :, _pair_slab(h)]
        qm = jnp.where(hm, qp, jnp.zeros_like(qp))
        base = ct_ref[0, h:h + 1, past:past + 1]
        s_p = _dot(qm, kp_ref[0, _pair_slab(h), :].astype(bf16)) - (ct_ref[0, h:h + 1, 0:past] - base)
        s_n = _dot(qm, kn_ref[0, _pair_slab(h), :]) - (ct_ref[0, h:h + 1, past:past + lq] - base)
        s_n = jnp.where(causal, s_n, NEG_BIG)
        m = jnp.maximum(jnp.max(s_p, axis=1, keepdims=True), jnp.max(s_n, axis=1, keepdims=True))
        probs.append((jnp.exp2(s_p - m).astype(bf16), jnp.exp2(s_n - m).astype(bf16)))
    accs = []
    for h in range(H):
        p_p, p_n = probs[h]
        accs.append(_dot_nt(p_p, _v_aug(vp_ref[0, _head_rows(h), :].astype(bf16), h))
                    + _dot_nt(p_n, _v_aug(vn_ref[0, _head_rows(h), :], h)))
    for p in range(H // 2):
        numer = jnp.where(first_half, accs[2 * p], accs[2 * p + 1])
        denom = pltpu.roll(jnp.where(first_half, accs[2 * p + 1], accs[2 * p]), FOX_DH, 1)
        o_ref[0, :, p * LANES:(p + 1) * LANES] = (numer / denom).astype(bf16)


def _fox_sample(fq, fkbt, fvbt, kt_past, vt_past, ct_flat):
    B, lq, _ = fq.shape
    past = kt_past.shape[2]
    lpad = ct_flat.shape[2]
    new = pl.BlockSpec((1, FOX_W, lq), lambda b: (b, 0, 0))
    old = pl.BlockSpec((1, FOX_W, past), lambda b: (b, 0, 0))
    qo = pl.BlockSpec((1, lq, FOX_W), lambda b: (b, 0, 0))
    return pl.pallas_call(
        functools.partial(_fox_sample_kernel, past=past, lq=lq),
        grid=(B,),
        in_specs=[qo, new, new, old, old, pl.BlockSpec((1, FOX_HEADS, lpad), lambda b: (b, 0, 0))],
        out_specs=qo,
        out_shape=jax.ShapeDtypeStruct((B, lq, FOX_W), bf16),
        compiler_params=pltpu.CompilerParams(dimension_semantics=("arbitrary",), vmem_limit_bytes=VMEM_LIMIT),
        name="fox_sample",
    )(fq, fkbt, fvbt, kt_past, vt_past, ct_flat)


def _layer_norm(t, g, b):
    mu = jnp.mean(t, axis=-1, keepdims=True)
    d = t - mu
    var = jnp.mean(d * d, axis=-1, keepdims=True)
    return d * lax.rsqrt(var + LN_EPS) * g + b


def _mix_kernel(x_ref, og_ref, of_ref, wz_ref, bz_ref, wpg_ref, wpf_ref, wo_ref, g1_ref, b1_ref, x1_ref, *, alpha):
    groups = _row_groups(x_ref.shape[0])
    merged = []
    for rows in groups:
        xb = x_ref[rows, :].astype(bf16)
        zg = _dot(xb, wz_ref[:, :D_MODEL]) + bz_ref[:, :D_MODEL]
        zf = _dot(xb, wz_ref[:, D_MODEL:]) + bz_ref[:, D_MODEL:]
        m = (jax.nn.sigmoid(zg) * _dot(og_ref[rows, :], wpg_ref[...])
             + jax.nn.sigmoid(zf) * _dot(of_ref[rows, :], wpf_ref[...]))
        merged.append(m.astype(bf16))
    for rows, m in zip(groups, merged):
        t = alpha * x_ref[rows, :] + _dot(m, wo_ref[...])
        x1_ref[rows, :] = _layer_norm(t, g1_ref[...], b1_ref[...])


def _row_groups(tm):
    rows = MXU_DIM if tm % MXU_DIM == 0 else tm
    return tuple(slice(r, r + rows) for r in range(0, tm, rows))


def _ffn_kernel(x1_ref, wg_ref, wu_ref, wd_ref, g2_ref, b2_ref, y_ref, *, alpha, bounds):
    chunks = [slice(lo, hi) for lo, hi in zip(bounds[:-1], bounds[1:])]
    stages = [(rows, cs) for rows in _row_groups(x1_ref.shape[0]) for cs in chunks]
    acc = {}
    pending = None

    def down(rows, cs, hid):
        part = _dot(hid, wd_ref[cs, :])
        key = (rows.start, rows.stop)
        acc[key] = part if key not in acc else acc[key] + part
        if cs is chunks[-1]:
            x1 = x1_ref[rows, :]
            y_ref[rows, :] = _layer_norm(alpha * x1 + acc[key], g2_ref[...], b2_ref[...])

    for rows, cs in stages:
        xb = x1_ref[rows, :].astype(bf16)
        g = _dot(xb, wg_ref[:, cs])
        u = _dot(xb, wu_ref[:, cs])
        hid = (g * jax.nn.sigmoid(g) * u).astype(bf16)
        if pending is not None:
            down(*pending)
        pending = (rows, cs, hid)
    down(*pending)


def _mix(x2d, og, of, wz, bz, wpg, wpf, wo, g1, b1, alpha):
    n = x2d.shape[0]
    tm = min(TOKEN_TILE, n)
    row = lambda w: pl.BlockSpec((tm, w), lambda i: (i, 0))
    return pl.pallas_call(
        functools.partial(_mix_kernel, alpha=alpha),
        grid=(n // tm,),
        in_specs=[row(D_MODEL), row(GLA_V), row(FOX_W)] + [_const_spec(a.shape) for a in (wz, bz, wpg, wpf, wo, g1, b1)],
        out_specs=row(D_MODEL),
        out_shape=jax.ShapeDtypeStruct((n, D_MODEL), f32),
        compiler_params=pltpu.CompilerParams(dimension_semantics=("arbitrary",), vmem_limit_bytes=VMEM_LIMIT),
        name="mix",
    )(x2d, og, of, wz, bz, wpg, wpf, wo, g1, b1)


def _ffn(x1, wg, wu, wd, g2, b2, alpha):
    n = x1.shape[0]
    tm = min(TOKEN_TILE, n)
    row = lambda w: pl.BlockSpec((tm, w), lambda i: (i, 0))
    return pl.pallas_call(
        functools.partial(_ffn_kernel, alpha=alpha, bounds=FFN_BOUNDS),
        grid=(n // tm,),
        in_specs=[row(D_MODEL)] + [_const_spec(a.shape) for a in (wg, wu, wd, g2, b2)],
        out_specs=row(D_MODEL),
        out_shape=jax.ShapeDtypeStruct((n, D_MODEL), f32),
        compiler_params=pltpu.CompilerParams(dimension_semantics=("arbitrary",), vmem_limit_bytes=VMEM_LIMIT),
        name="ffn",
    )(x1, wg, wu, wd, g2, b2)


def _prep_weights(w_in, b_in, w_alpha2, b_alpha2, gla_norm_g, w_proj_gla, w_proj_fox, w_out,
                  ln1_g, ln1_b, w_ffn_gate, w_ffn_up, w_ffn_down, ln2_g, ln2_b):
    def cols(a, idx):
        return a[..., _OFF[idx]:_OFF[idx + 1]]

    order = (0, 1, 2, 4, 5, 3)
    pad = SMALL_W - GLA_RANK
    w1 = jnp.concatenate([cols(w_in, i) for i in order] + [jnp.zeros((D_MODEL, pad), f32)], axis=1).astype(bf16)
    b1 = jnp.concatenate([cols(b_in, i) for i in order] + [jnp.zeros((pad,), f32)])[None, :]
    t_cols = (slice(_OFF[6], _OFF[9]),)
    ff_pad = FF_ROWS - FOX_HEADS
    wkv = jnp.pad(jnp.transpose(w_in[:, t_cols[0]]), ((0, ff_pad), (0, 0))).astype(bf16)
    bkv = jnp.broadcast_to(jnp.pad(b_in[t_cols[0]], (0, ff_pad))[:, None], (2 * FOX_W + FF_ROWS, LANES)).astype(f32)
    wa = jnp.zeros((SMALL_W, GLA_QK), f32).at[:GLA_RANK].set(w_alpha2).astype(bf16)
    head_of = np.arange(FOX_W) // FOX_DH
    indq = np.zeros((FOX_W, SMALL_W), np.float32)
    indq[np.arange(FOX_W), FOX_HEADS + head_of] = 1.0
    row = lambda a: a[None, :].astype(f32)
    return dict(
        w1=w1, b1=b1, wkv=wkv, bkv=bkv, wa=wa, ba=row(b_alpha2), indq=jnp.asarray(indq, bf16),
        g=row(gla_norm_g),
        wz=w_in[:, _OFF[9]:].astype(bf16), bz=row(b_in[_OFF[9]:]),
        wpg=w_proj_gla.astype(bf16), wpf=w_proj_fox.astype(bf16), wo=w_out.astype(bf16),
        g1=row(ln1_g), b1n=row(ln1_b),
        wg=w_ffn_gate.astype(bf16), wu=w_ffn_up.astype(bf16), wd=w_ffn_down.astype(bf16),
        g2=row(ln2_g), b2n=row(ln2_b),
    )


def _heads_last(t, B, L):
    return jnp.transpose(t.reshape(B, FOX_HEADS, FOX_DH, L), (0, 3, 1, 2))


def _layer(x, s0, k_past, v_past, lf_past, p, alpha):
    B, L, _ = x.shape
    n = B * L
    x2d = x.reshape(n, D_MODEL)
    prompt = k_past is None
    lo = L if prompt else n
    gq, gk, gv, la, sg, fq, fkt, fvt, fkbt, fvbt, lft, nk2t, qmx = _inproj(
        x2d, lo, p["w1"], p["b1"], p["wkv"], p["bkv"], p["wa"], p["ba"], p["indq"])
    r3 = lambda a: a.reshape(B, L, a.shape[-1])

    o_gla, st = _gla(r3(gq), r3(gk), r3(gv), r3(la), r3(sg), p["g"], _state_to_pairs(s0))
    gla_state = _pairs_to_state(st)

    if prompt:
        nc = L // LANES
        chunks = lambda t: t.reshape(B, FOX_HEADS, nc, LANES)
        ct, start = _fox_meta(chunks(lft), chunks(nk2t), qmx.reshape(B, nc, SMALL_W))
        o_fox = _fox_prompt(r3(fq), fkbt, fvbt, ct, start.reshape(B, nc))
        fk, fv = _heads_last(fkt, B, L), _heads_last(fvt, B, L)
        lf = jnp.transpose(lft, (0, 2, 1))
    else:
        past = k_past.shape[1]
        lk = past + L
        lpad = -(-lk // (8 * LANES)) * (8 * LANES)
        nc = lpad // LANES
        per_stream = lambda t: jnp.transpose(t.reshape(t.shape[1], B, L), (1, 0, 2))
        lft_s = per_stream(lft)
        lf_all_t = jnp.concatenate([jnp.transpose(lf_past.astype(f32), (0, 2, 1)), lft_s], axis=2)
        lf_all_t = jnp.pad(lf_all_t, ((0, 0), (0, 0), (0, lpad - lk))).reshape(B, FOX_HEADS, nc, LANES)
        ct, _ = _fox_meta(lf_all_t, jnp.zeros((B, FOX_HEADS, nc, LANES), f32), jnp.zeros((B, nc, SMALL_W), f32))
        cache_t = lambda c: jnp.transpose(c, (0, 2, 3, 1)).reshape(B, FOX_W, past)
        o_fox = _fox_sample(r3(fq), per_stream(fkbt), per_stream(fvbt), cache_t(k_past), cache_t(v_past),
                            ct.reshape(B, FOX_HEADS, lpad))
        fk, fv = _heads_last(per_stream(fkt), B, L), _heads_last(per_stream(fvt), B, L)
        lf = jnp.transpose(lft_s, (0, 2, 1))

    x1 = _mix(x2d, o_gla.reshape(n, GLA_V), o_fox.reshape(n, FOX_W), p["wz"], p["bz"], p["wpg"], p["wpf"], p["wo"],
              p["g1"], p["b1n"], alpha)
    y = _ffn(x1, p["wg"], p["wu"], p["wd"], p["g2"], p["b2n"], alpha)
    return y.reshape(B, L, D_MODEL), gla_state, fk, fv, lf


def kernel(x_prompt, x_sample, state_gla, cache_fox_k, cache_fox_v, cache_fox_logf, w_in, b_in, w_alpha2, b_alpha2,
           gla_norm_g, w_proj_gla, w_proj_fox, w_out, ln1_g, ln1_b, w_ffn_gate, w_ffn_up, w_ffn_down, ln2_g, ln2_b):
    depth = w_in.shape[0]
    alpha = (2.0 * depth) ** 0.25
    yp, ys = x_prompt, x_sample
    outs_p, outs_s = [], []
    for l in range(depth):
        p = _prep_weights(w_in[l], b_in[l], w_alpha2[l], b_alpha2[l], gla_norm_g[l], w_proj_gla[l], w_proj_fox[l],
                          w_out[l], ln1_g[l], ln1_b[l], w_ffn_gate[l], w_ffn_up[l], w_ffn_down[l], ln2_g[l], ln2_b[l])
        s0 = jnp.zeros((yp.shape[0], GLA_HEADS, GLA_DK, GLA_DV), f32)
        yp, *rest_p = _layer(yp, s0, None, None, None, p, alpha)
        outs_p.append(rest_p)
        ys, *rest_s = _layer(ys, state_gla[l], cache_fox_k[l], cache_fox_v[l], cache_fox_logf[l], p, alpha)
        outs_s.append(rest_s)
    stack = lambda outs, i: jnp.stack([o[i] for o in outs])
    return (yp, ys,
            stack(outs_p, 0), stack(outs_p, 1), stack(outs_p, 2), stack(outs_p, 3),
            stack(outs_s, 0), stack(outs_s, 1), stack(outs_s, 2), stack(outs_s, 3))
```

```python
import functools

import jax
import jax.numpy as jnp
import numpy as np
from jax import lax
from jax.experimental import pallas as pl
from jax.experimental.pallas import tpu as pltpu

f32 = jnp.float32
bf16 = jnp.bfloat16

D_MODEL = 1024
GLA_HEADS = 4
GLA_DK = 64
GLA_DV = 128
GLA_RANK = 16
GLA_TAU = 16.0
GLA_QK = GLA_HEADS * GLA_DK
GLA_V = GLA_HEADS * GLA_DV
GLA_CHUNK = 64
FOX_HEADS = 8
FOX_DH = 64
FOX_W = FOX_HEADS * FOX_DH
D_FF = 2816
LN_EPS = 1e-5
RMS_EPS = 1e-5
SPLIT_SIZES = (GLA_QK, GLA_QK, GLA_V, GLA_RANK, GLA_V, FOX_W, FOX_W, FOX_W, FOX_HEADS, D_MODEL, D_MODEL)
_OFF = np.concatenate([[0], np.cumsum(SPLIT_SIZES)]).astype(int)

LANES = 128
MXU_DIM = 256
FFN_BOUNDS = (0, 4 * MXU_DIM, 8 * MXU_DIM, D_FF)
TOKEN_TILE = 1024
SMALL_W = LANES
FF_ROWS = 16
ATT_BLOCK = 128
LOG2E = 1.4426950408889634
SKIP_T2 = 110.0 * LOG2E
NORM_SLACK = 1.02
NEG_BIG = -1e30
VMEM_LIMIT = 56 * 1024 * 1024


def _dot(a, b):
    return jnp.dot(a, b, preferred_element_type=f32)


def _dot_nt(a, b):
    return lax.dot_general(a, b, (((1,), (1,)), ((), ())), preferred_element_type=f32)


def _dot_tn(a, b):
    return lax.dot_general(a, b, (((0,), (0,)), ((), ())), preferred_element_type=f32)


def _dot_hi(a, b):
    return jnp.dot(a, b, preferred_element_type=f32, precision=lax.Precision.HIGHEST)


def _log_sigmoid(x):
    return jnp.minimum(x, 0.0) - jnp.log1p(jnp.exp(-jnp.abs(x)))


def _const_spec(shape):
    return pl.BlockSpec(shape, lambda *_: (0,) * len(shape), pipeline_mode=pl.Buffered(1))


def _inproj_kernel(x_ref, w_ref, b_ref, wkv_ref, bkv_ref, wa_ref, ba_ref, indq_ref,
                   gq_ref, gk_ref, gv_ref, la_ref, sg_ref, fq_ref, fkt_ref, fvt_ref,
                   fkbt_ref, fvbt_ref, lft_ref, nk2t_ref, qmx_ref):
    xb = x_ref[...].astype(bf16)
    tm = x_ref.shape[0]

    def group(lo, width):
        return _dot(xb, w_ref[:, lo:lo + width]) + b_ref[:, lo:lo + width]

    o_gq, o_gk, o_gv = 0, GLA_QK, 2 * GLA_QK
    o_gr = o_gv + GLA_V
    o_fq = o_gr + GLA_V
    o_small = o_fq + FOX_W
    small_b = group(o_small, SMALL_W).astype(bf16)
    fq = (group(o_fq, FOX_W) * (FOX_DH ** -0.5 * LOG2E)).astype(bf16)
    fq_ref[...] = fq
    fq32 = fq.astype(f32)
    fq2 = (fq32 * fq32).astype(bf16)
    gq_ref[...] = (group(o_gq, GLA_QK) * (GLA_DK ** -0.5)).astype(bf16)
    gk_ref[...] = group(o_gk, GLA_QK).astype(bf16)
    gv_ref[...] = group(o_gv, GLA_V).astype(bf16)
    la_pre = _dot(small_b, wa_ref[...]) + ba_ref[...]
    la_ref[...] = _log_sigmoid(la_pre) * (1.0 / GLA_TAU)
    nq2 = _dot(fq2, indq_ref[...])
    qmx_ref[...] = jnp.max(nq2.reshape(tm // ATT_BLOCK, ATT_BLOCK, SMALL_W), axis=1)
    gr = group(o_gr, GLA_V)
    sg_ref[...] = (gr * jax.nn.sigmoid(gr)).astype(bf16)

    bias_t = jnp.concatenate([bkv_ref[...]] * (tm // LANES), axis=1)
    k_t = _dot_nt(wkv_ref[:FOX_W, :], xb) + bias_t[:FOX_W]
    fkt_ref[0] = k_t
    kb = k_t.astype(bf16)
    fkbt_ref[0] = kb
    k32 = kb.astype(f32)
    k2 = k32 * k32
    nk2t_ref[0] = jnp.concatenate(
        [jnp.sum(k2[h * FOX_DH:(h + 1) * FOX_DH], axis=0, keepdims=True) for h in range(FOX_HEADS)], axis=0)
    vf_t = _dot_nt(wkv_ref[FOX_W:, :], xb) + bias_t[FOX_W:]
    fvt_ref[0] = vf_t[:FOX_W]
    fvbt_ref[0] = vf_t[:FOX_W].astype(bf16)
    lft_ref[0] = _log_sigmoid(vf_t[FOX_W:FOX_W + FOX_HEADS])


def _inproj(x2d, lo, w1, b1, wkv, bkv, wa, ba, indq):
    n = x2d.shape[0]
    bo = n // lo
    tm = min(TOKEN_TILE, lo)
    assert lo % tm == 0 and tm % LANES == 0
    tpb = lo // tm
    row = lambda w: pl.BlockSpec((tm, w), lambda i: (i, 0))
    col = lambda w: pl.BlockSpec((1, w, tm), lambda i: (i // tpb, 0, i % tpb))
    rows_out = lambda w, dt: (jax.ShapeDtypeStruct((n, w), dt), row(w))
    cols_out = lambda w, dt: (jax.ShapeDtypeStruct((bo, w, lo), dt), col(w))
    outs = (
        rows_out(GLA_QK, bf16),
        rows_out(GLA_QK, bf16),
        rows_out(GLA_V, bf16),
        rows_out(GLA_QK, f32),
        rows_out(GLA_V, bf16),
        rows_out(FOX_W, bf16),
        cols_out(FOX_W, f32),
        cols_out(FOX_W, f32),
        cols_out(FOX_W, bf16),
        cols_out(FOX_W, bf16),
        cols_out(FOX_HEADS, f32),
        cols_out(FOX_HEADS, f32),
        (jax.ShapeDtypeStruct((n // ATT_BLOCK, SMALL_W), f32),
         pl.BlockSpec((tm // ATT_BLOCK, SMALL_W), lambda i: (i, 0))),
    )
    consts = (w1, b1, wkv, bkv, wa, ba, indq)
    return pl.pallas_call(
        _inproj_kernel,
        grid=(n // tm,),
        in_specs=[row(D_MODEL)] + [_const_spec(a.shape) for a in consts],
        out_specs=[o[1] for o in outs],
        out_shape=[o[0] for o in outs],
        compiler_params=pltpu.CompilerParams(dimension_semantics=("arbitrary",), vmem_limit_bytes=VMEM_LIMIT),
        name="inproj",
    )(x2d, *consts)


def _gla_kernel(q_ref, k_ref, v_ref, la_ref, sg_ref, g_ref, s0_ref, o_ref, sfin_ref, st_ref, *, chunk, nsub):
    t = pl.program_id(1)

    @pl.when(t == 0)
    def _():
        st_ref[...] = s0_ref[0]

    C = chunk
    tl = C * nsub
    r_i = lax.broadcasted_iota(jnp.int32, (C, C), 0)
    c_i = lax.broadcasted_iota(jnp.int32, (C, C), 1)
    causal = c_i <= r_i
    grp = min(tl, MXU_DIM)
    rt = lax.broadcasted_iota(jnp.int32, (grp, grp), 0)
    ct = lax.broadcasted_iota(jnp.int32, (grp, grp), 1)
    chunk_tril = ((ct <= rt) & (ct >= (rt // C) * C)).astype(bf16)
    lane = lax.broadcasted_iota(jnp.int32, (1, LANES), 1)
    first_half = lane < GLA_DK
    mid = C // 2
    pairs = range(GLA_HEADS // 2)

    def lanes(p):
        return slice(p * LANES, (p + 1) * LANES)

    def one_head(x, hh):
        return jnp.where(first_half if hh == 0 else jnp.logical_not(first_half), x, jnp.zeros_like(x))

    la = la_ref[0]
    la_hi = la.astype(bf16)
    la_lo = (la - la_hi.astype(f32)).astype(bf16)
    bcum_all = jnp.concatenate(
        [_dot(chunk_tril, la_hi[r:r + grp]) + _dot(chunk_tril, la_lo[r:r + grp]) for r in range(0, tl, grp)], axis=0)

    q_in, dec, amat, upd = [], [], {}, {}
    for c in range(nsub):
        rows = slice(c * C, (c + 1) * C)
        bcum = bcum_all[rows]
        b_last = bcum[C - 1:C, :]
        ref = bcum[mid:mid + 1, :]
        q = q_ref[0, rows, :].astype(f32)
        k = k_ref[0, rows, :].astype(f32)
        q_in.append((q * jnp.exp(bcum)).astype(bf16))
        q_a = (q * jnp.exp(jnp.clip(bcum - ref, -80.0, 80.0))).astype(bf16)
        k_a = (k * jnp.exp(jnp.clip(ref - bcum, -80.0, 80.0))).astype(bf16)
        k_d = (k * jnp.exp(b_last - bcum)).astype(bf16)
        dec.append(jnp.exp(b_last))
        for p in pairs:
            for hh in range(2):
                h = 2 * p + hh
                v_h = v_ref[0, rows, h * GLA_DV:(h + 1) * GLA_DV]
                a = _dot_nt(one_head(q_a[:, lanes(p)], hh), k_a[:, lanes(p)])
                amat[c, h] = jnp.where(causal, a, 0.0).astype(bf16)
                upd[c, h] = _dot_tn(v_h, k_d[:, lanes(p)])

    states = []
    st = [st_ref[p] for p in pairs]
    for c in range(nsub):
        states.append([s.astype(bf16) for s in st])
        st = [dec[c][:, lanes(p)] * st[p] + jnp.where(first_half, upd[c, 2 * p], upd[c, 2 * p + 1]) for p in pairs]
    for p in pairs:
        st_ref[p] = st[p]

    for c in range(nsub):
        rows = slice(c * C, (c + 1) * C)
        for h in range(GLA_HEADS):
            p, hh = divmod(h, 2)
            vs = slice(h * GLA_DV, (h + 1) * GLA_DV)
            o = _dot(amat[c, h], v_ref[0, rows, vs]) + _dot_nt(one_head(q_in[c][:, lanes(p)], hh), states[c][p])
            o = o * lax.rsqrt(jnp.mean(o * o, axis=-1, keepdims=True) + RMS_EPS)
            o = o * g_ref[:, vs] * sg_ref[0, rows, vs].astype(f32)
            o_ref[0, rows, vs] = o.astype(bf16)

    @pl.when(t == pl.num_programs(1) - 1)
    def _():
        sfin_ref[0] = st_ref[...]


def _gla(gq, gk, gv, la, sg, g, s0t):
    B, L, _ = gq.shape
    C = min(GLA_CHUNK, L)
    tl = min(1024, L)
    assert L % tl == 0 and tl % C == 0
    tok = lambda w: pl.BlockSpec((1, tl, w), lambda b, t: (b, t, 0))
    st_spec = pl.BlockSpec((1, 2, GLA_DV, LANES), lambda b, t: (b, 0, 0, 0))
    return pl.pallas_call(
        functools.partial(_gla_kernel, chunk=C, nsub=tl // C),
        grid=(B, L // tl),
        in_specs=[tok(GLA_QK), tok(GLA_QK), tok(GLA_V), tok(GLA_QK), tok(GLA_V),
                  pl.BlockSpec((1, GLA_V), lambda b, t: (0, 0)), st_spec],
        out_specs=[tok(GLA_V), st_spec],
        out_shape=(jax.ShapeDtypeStruct((B, L, GLA_V), bf16),
                   jax.ShapeDtypeStruct((B, 2, GLA_DV, LANES), f32)),
        scratch_shapes=[pltpu.VMEM((2, GLA_DV, LANES), f32)],
        compiler_params=pltpu.CompilerParams(dimension_semantics=("arbitrary", "arbitrary"),
                                             vmem_limit_bytes=VMEM_LIMIT),
        name="gla_scan",
    )(gq, gk, gv, la, sg, g, s0t)


def _state_to_pairs(s):
    B = s.shape[0]
    s = s.reshape(B, 2, 2, GLA_DK, GLA_DV)
    return jnp.transpose(s, (0, 1, 4, 2, 3)).reshape(B, 2, GLA_DV, 2 * GLA_DK)


def _pairs_to_state(st):
    B = st.shape[0]
    st = st.reshape(B, 2, GLA_DV, 2, GLA_DK)
    return jnp.transpose(st, (0, 1, 3, 4, 2)).reshape(B, GLA_HEADS, GLA_DK, GLA_DV)


def _meta_kernel(lf_ref, nk_ref, qm_ref, ct_ref, start_ref, *, nc):
    H = FOX_HEADS
    r_i = lax.broadcasted_iota(jnp.int32, (LANES, LANES), 0)
    c_i = lax.broadcasted_iota(jnp.int32, (LANES, LANES), 1)
    upper = (r_i <= c_i).astype(f32)
    rj = lax.broadcasted_iota(jnp.int32, (nc, nc), 0)
    cj = lax.broadcasted_iota(jnp.int32, (nc, nc), 1)
    strict_lower = (cj < rj).astype(f32)
    eye = rj == cj

    def to_row(col):
        return jnp.sum(jnp.where(eye, jnp.broadcast_to(col, (nc, nc)), 0.0), axis=0, keepdims=True)

    ys = [_dot_hi(lf_ref[0, h], upper) for h in range(H)]
    prev = [_dot_hi(strict_lower, jnp.broadcast_to(y[:, LANES - 1:LANES], (nc, LANES))) for y in ys]
    start = None
    for h in range(H):
        c = (ys[h] + prev[h]) * LOG2E
        ct_ref[0, h] = c
        cmax = jnp.max(c, axis=1, keepdims=True)
        cmin = jnp.min(c, axis=1, keepdims=True)
        qmax = jnp.sqrt(qm_ref[0, :, H + h:H + h + 1])
        kmax = jnp.sqrt(jnp.max(jnp.max(nk_ref[0, h], axis=1, keepdims=True), axis=0, keepdims=True))
        thr = cmax + (2.0 * NORM_SLACK) * qmax * kmax + SKIP_T2
        needed = (to_row(cmin) <= thr) & (cj <= rj)
        first = jnp.min(jnp.where(needed, cj, nc), axis=1, keepdims=True)
        start = first if start is None else jnp.minimum(start, first)
    start_ref[0] = to_row(start.astype(f32)).astype(jnp.int32)


def _fox_meta(lf_t, nk2_t, qmx):
    B, _, nc, _ = lf_t.shape
    per_head = pl.BlockSpec((1, FOX_HEADS, nc, LANES), lambda b: (b, 0, 0, 0))
    return pl.pallas_call(
        functools.partial(_meta_kernel, nc=nc),
        grid=(B,),
        in_specs=[per_head, per_head, pl.BlockSpec((1, nc, LANES), lambda b: (b, 0, 0))],
        out_specs=[pl.BlockSpec((1, FOX_HEADS, nc, LANES), lambda b: (b, 0, 0, 0)),
                   pl.BlockSpec((1, 1, nc), lambda b: (b, 0, 0))],
        out_shape=(jax.ShapeDtypeStruct((B, FOX_HEADS, nc, LANES), f32),
                   jax.ShapeDtypeStruct((B, 1, nc), jnp.int32)),
        compiler_params=pltpu.CompilerParams(dimension_semantics=("arbitrary",), vmem_limit_bytes=VMEM_LIMIT),
        name="fox_meta",
    )(lf_t, nk2_t, qmx)


def _cumsum_kernel(lf_ref, ct_ref, *, nb, nc):
    r_i = lax.broadcasted_iota(jnp.int32, (LANES, LANES), 0)
    c_i = lax.broadcasted_iota(jnp.int32, (LANES, LANES), 1)
    upper = (r_i <= c_i).astype(f32)
    rj = lax.broadcasted_iota(jnp.int32, (nc, nc), 0)
    cj = lax.broadcasted_iota(jnp.int32, (nc, nc), 1)
    strict_lower = (cj < rj).astype(f32)
    seqs = [(s, h) for s in range(nb) for h in range(FOX_HEADS)]
    ys = [_dot_hi(lf_ref[s, h], upper) for s, h in seqs]
    prev = [_dot_hi(strict_lower, jnp.broadcast_to(y[:, LANES - 1:LANES], (nc, LANES))) for y in ys]
    for (s, h), y, p in zip(seqs, ys, prev):
        ct_ref[s, h] = (y + p) * LOG2E


def _fox_cumsum(lf_t):
    B, _, nc, _ = lf_t.shape
    nb = next(d for d in (4, 2, 1) if B % d == 0)
    spec = pl.BlockSpec((nb, FOX_HEADS, nc, LANES), lambda b: (b, 0, 0, 0))
    return pl.pallas_call(
        functools.partial(_cumsum_kernel, nb=nb, nc=nc),
        grid=(B // nb,),
        in_specs=[spec],
        out_specs=spec,
        out_shape=jax.ShapeDtypeStruct(lf_t.shape, f32),
        compiler_params=pltpu.CompilerParams(dimension_semantics=("arbitrary",), vmem_limit_bytes=VMEM_LIMIT),
        name="fox_cumsum",
    )(lf_t)


def _head_rows(h):
    return slice(h * FOX_DH, (h + 1) * FOX_DH)


def _pair_slab(h, width=LANES):
    return slice((h // 2) * width, (h // 2 + 1) * width)


def _v_aug(vt_h, h):
    one = jnp.ones_like(vt_h)
    return jnp.concatenate([vt_h, one] if h % 2 == 0 else [one, vt_h], axis=0)


def _fox_prompt_kernel(start_ref, q_ref, kp_ref, kc_ref, vp_ref, vc_ref,
                       ct_ref, kh_ref, vh_ref, o_ref, m_s, acc_s, kbuf, vbuf, sem):
    T = ATT_BLOCK
    TQ = 2 * T
    H = FOX_HEADS
    b = pl.program_id(0)
    i = pl.program_id(1)
    r_i = lax.broadcasted_iota(jnp.int32, (TQ, 2 * TQ), 0)
    c_i = lax.broadcasted_iota(jnp.int32, (TQ, 2 * TQ), 1)
    visible = c_i <= r_i + TQ
    lane = lax.broadcasted_iota(jnp.int32, (1, LANES), 1)
    first_half = lane < FOX_DH
    halves = (0, 1)

    def q_rows(u):
        return slice(u * TQ, (u + 1) * TQ)

    def first_block(u):
        return 4 * i + 2 * u - 2

    def head_q(u, h):
        hm = first_half if h % 2 == 0 else jnp.logical_not(first_half)
        qp = q_ref[0, q_rows(u), _pair_slab(h)]
        return jnp.where(hm, qp, jnp.zeros_like(qp))

    def head_base(u, h):
        return ct_ref[0, h, pl.ds(4 * i + 2 * u, 1), 0:1]

    def c_row(h, j):
        return ct_ref[0, h, pl.ds(j, 1), :]

    def window(u, prev_ref, cur_ref, rows):
        if u == 0:
            return jnp.concatenate([prev_ref[0, rows, :], cur_ref[0, rows, 0:TQ]], axis=1)
        return cur_ref[0, rows, :]

    scores = {}
    for u in halves:
        j0 = first_block(u)
        jc = jnp.maximum(j0, 0)
        for h in range(H):
            base = head_base(u, h)
            ck = jnp.concatenate([jnp.where(j0 >= 0, c_row(h, jc) - base, -NEG_BIG),
                                  jnp.where(j0 >= 0, c_row(h, jc + 1) - base, -NEG_BIG),
                                  c_row(h, j0 + 2) - base, c_row(h, j0 + 3) - base], axis=1)
            s = jnp.where(visible, _dot(head_q(u, h), window(u, kp_ref, kc_ref, _pair_slab(h))) - ck, NEG_BIG)
            m = jnp.max(s, axis=1, keepdims=True)
            m_s[h, q_rows(u), :] = m
            scores[u, h] = (s, m)
    for u in halves:
        for h in range(H):
            s, m = scores[u, h]
            acc_s[h, q_rows(u), :] = _dot_nt(jnp.exp2(s - m).astype(bf16),
                                            _v_aug(window(u, vp_ref, vc_ref, _head_rows(h)), h))

    def copies(j, slot):
        cols = pl.ds(pl.multiple_of(j * T, T), T)
        return (pltpu.make_async_copy(kh_ref.at[b, :, cols], kbuf.at[slot], sem.at[0, slot]),
                pltpu.make_async_copy(vh_ref.at[b, :, cols], vbuf.at[slot], sem.at[1, slot]))

    for u in halves:
        start = jnp.minimum(start_ref[b, 4 * i + 2 * u], start_ref[b, 4 * i + 2 * u + 1])
        n_far = jnp.maximum(first_block(u) - start, 0)

        @pl.when(n_far > 0)
        def _(u=u, start=start, n_far=n_far):
            for cp in copies(start, 0):
                cp.start()

            def body(t, carry):
                j = start + t
                slot = lax.rem(t, 2)
                for cp in copies(j, slot):
                    cp.wait()

                @pl.when(t + 1 < n_far)
                def _():
                    for cp in copies(j + 1, 1 - slot):
                        cp.start()

                for h in range(H):
                    ck = c_row(h, j) - head_base(u, h)
                    s = _dot(head_q(u, h), kbuf[slot, _pair_slab(h), :]) - ck
                    m_prev = m_s[h, q_rows(u), :]
                    m_new = jnp.maximum(m_prev, jnp.max(s, axis=1, keepdims=True))
                    alpha = jnp.exp2(m_prev - m_new)
                    p = jnp.exp2(s - m_new)
                    acc_s[h, q_rows(u), :] = (alpha * acc_s[h, q_rows(u), :]
                                             + _dot_nt(p.astype(bf16), _v_aug(vbuf[slot, _head_rows(h), :], h)))
                    m_s[h, q_rows(u), :] = m_new
                return carry

            lax.fori_loop(0, n_far, body, 0)

    for p in range(H // 2):
        even = acc_s[2 * p]
        odd = acc_s[2 * p + 1]
        numer = jnp.where(first_half, even, odd)
        denom = pltpu.roll(jnp.where(first_half, odd, even), FOX_DH, 1)
        o_ref[0, :, p * LANES:(p + 1) * LANES] = (numer / denom).astype(bf16)


def _fox_prompt(fq, fkbt, fvbt, ct, start):
    B, L, _ = fq.shape
    T = ATT_BLOCK
    TQ = 2 * T
    TS = 2 * TQ
    assert L % TS == 0
    prev = pl.BlockSpec((1, FOX_W, TQ), lambda b, i, s: (b, 0, jnp.maximum(2 * i - 1, 0)))
    cur = pl.BlockSpec((1, FOX_W, TS), lambda b, i, s: (b, 0, i))
    grid_spec = pltpu.PrefetchScalarGridSpec(
        num_scalar_prefetch=1,
        grid=(B, L // TS),
        in_specs=[pl.BlockSpec((1, TS, FOX_W), lambda b, i, s: (b, i, 0)),
                  prev, cur, prev, cur,
                  pl.BlockSpec((1, FOX_HEADS, L // T, LANES), lambda b, i, s: (b, 0, 0, 0)),
                  pl.BlockSpec(memory_space=pl.ANY), pl.BlockSpec(memory_space=pl.ANY)],
        out_specs=pl.BlockSpec((1, TS, FOX_W), lambda b, i, s: (b, i, 0)),
        scratch_shapes=[pltpu.VMEM((FOX_HEADS, TS, 1), f32),
                        pltpu.VMEM((FOX_HEADS, TS, LANES), f32),
                        pltpu.VMEM((2, FOX_W, T), bf16), pltpu.VMEM((2, FOX_W, T), bf16),
                        pltpu.SemaphoreType.DMA((2, 2))],
    )
    return pl.pallas_call(
        _fox_prompt_kernel,
        grid_spec=grid_spec,
        out_shape=jax.ShapeDtypeStruct((B, L, FOX_W), bf16),
        compiler_params=pltpu.CompilerParams(dimension_semantics=("arbitrary", "arbitrary"),
                                             vmem_limit_bytes=VMEM_LIMIT),
        name="fox_prompt",
    )(start, fq, fkbt, fkbt, fvbt, fvbt, ct, fkbt, fvbt)


def _fox_sample_kernel(q_ref, kn_ref, vn_ref, kp_ref, vp_ref, ct_ref, o_ref, *, past, lq):
    H = FOX_HEADS
    r_i = lax.broadcasted_iota(jnp.int32, (lq, lq), 0)
    c_i = lax.broadcasted_iota(jnp.int32, (lq, lq), 1)
    causal = c_i <= r_i
    lane = lax.broadcasted_iota(jnp.int32, (1, LANES), 1)
    first_half = lane < FOX_DH
    probs = []
    for h in range(H):
        hm = first_half if h % 2 == 0 else jnp.logical_not(first_half)
        qp = q_ref[0, :, _pair_slab(h)]
        qm = jnp.where(hm, qp, jnp.zeros_like(qp))
        base = ct_ref[0, h:h + 1, past:past + 1]
        s_p = _dot(qm, kp_ref[0, _pair_slab(h), :].astype(bf16)) - (ct_ref[0, h:h + 1, 0:past] - base)
        s_n = _dot(qm, kn_ref[0, _pair_slab(h), :]) - (ct_ref[0, h:h + 1, past:past + lq] - base)
        s_n = jnp.where(causal, s_n, NEG_BIG)
        m = jnp.maximum(jnp.max(s_p, axis=1, keepdims=True), jnp.max(s_n, axis=1, keepdims=True))
        probs.append((jnp.exp2(s_p - m).astype(bf16), jnp.exp2(s_n - m).astype(bf16)))
    accs = []
    for h in range(H):
        p_p, p_n = probs[h]
        accs.append(_dot_nt(p_p, _v_aug(vp_ref[0, _head_rows(h), :].astype(bf16), h))
                    + _dot_nt(p_n, _v_aug(vn_ref[0, _head_rows(h), :], h)))
    for p in range(H // 2):
        numer = jnp.where(first_half, accs[2 * p], accs[2 * p + 1])
        denom = pltpu.roll(jnp.where(first_half, accs[2 * p + 1], accs[2 * p]), FOX_DH, 1)
        o_ref[0, :, p * LANES:(p + 1) * LANES] = (numer / denom).astype(bf16)


def _fox_sample(fq, fkbt, fvbt, kt_past, vt_past, ct_flat):
    B, lq, _ = fq.shape
    past = kt_past.shape[2]
    lpad = ct_flat.shape[2]
    new = pl.BlockSpec((1, FOX_W, lq), lambda b: (b, 0, 0))
    old = pl.BlockSpec((1, FOX_W, past), lambda b: (b, 0, 0))
    qo = pl.BlockSpec((1, lq, FOX_W), lambda b: (b, 0, 0))
    return pl.pallas_call(
        functools.partial(_fox_sample_kernel, past=past, lq=lq),
        grid=(B,),
        in_specs=[qo, new, new, old, old, pl.BlockSpec((1, FOX_HEADS, lpad), lambda b: (b, 0, 0))],
        out_specs=qo,
        out_shape=jax.ShapeDtypeStruct((B, lq, FOX_W), bf16),
        compiler_params=pltpu.CompilerParams(dimension_semantics=("arbitrary",), vmem_limit_bytes=VMEM_LIMIT),
        name="fox_sample",
    )(fq, fkbt, fvbt, kt_past, vt_past, ct_flat)


def _layer_norm(t, g, b):
    mu = jnp.mean(t, axis=-1, keepdims=True)
    d = t - mu
    var = jnp.mean(d * d, axis=-1, keepdims=True)
    return d * lax.rsqrt(var + LN_EPS) * g + b


def _mix_kernel(x_ref, og_ref, of_ref, wz_ref, bz_ref, wpg_ref, wpf_ref, wo_ref, g1_ref, b1_ref, x1_ref, *, alpha):
    groups = _row_groups(x_ref.shape[0])
    merged = []
    for rows in groups:
        xb = x_ref[rows, :].astype(bf16)
        zg = _dot(xb, wz_ref[:, :D_MODEL]) + bz_ref[:, :D_MODEL]
        zf = _dot(xb, wz_ref[:, D_MODEL:]) + bz_ref[:, D_MODEL:]
        m = (jax.nn.sigmoid(zg) * _dot(og_ref[rows, :], wpg_ref[...])
             + jax.nn.sigmoid(zf) * _dot(of_ref[rows, :], wpf_ref[...]))
        merged.append(m.astype(bf16))
    for rows, m in zip(groups, merged):
        t = alpha * x_ref[rows, :] + _dot(m, wo_ref[...])
        x1_ref[rows, :] = _layer_norm(t, g1_ref[...], b1_ref[...])


def _row_groups(tm):
    rows = MXU_DIM if tm % MXU_DIM == 0 else tm
    return tuple(slice(r, r + rows) for r in range(0, tm, rows))


def _ffn_kernel(x1_ref, wg_ref, wu_ref, wd_ref, g2_ref, b2_ref, y_ref, *, alpha, bounds):
    chunks = [slice(lo, hi) for lo, hi in zip(bounds[:-1], bounds[1:])]
    stages = [(rows, cs) for rows in _row_groups(x1_ref.shape[0]) for cs in chunks]
    acc = {}
    pending = None

    def down(rows, cs, hid):
        part = _dot(hid, wd_ref[cs, :])
        key = (rows.start, rows.stop)
        acc[key] = part if key not in acc else acc[key] + part
        if cs is chunks[-1]:
            x1 = x1_ref[rows, :]
            y_ref[rows, :] = _layer_norm(alpha * x1 + acc[key], g2_ref[...], b2_ref[...])

    for rows, cs in stages:
        xb = x1_ref[rows, :].astype(bf16)
        g = _dot(xb, wg_ref[:, cs])
        u = _dot(xb, wu_ref[:, cs])
        hid = (g * jax.nn.sigmoid(g) * u).astype(bf16)
        if pending is not None:
            down(*pending)
        pending = (rows, cs, hid)
    down(*pending)


def _mix(x2d, og, of, wz, bz, wpg, wpf, wo, g1, b1, alpha):
    n = x2d.shape[0]
    tm = min(TOKEN_TILE, n)
    row = lambda w: pl.BlockSpec((tm, w), lambda i: (i, 0))
    return pl.pallas_call(
        functools.partial(_mix_kernel, alpha=alpha),
        grid=(n // tm,),
        in_specs=[row(D_MODEL), row(GLA_V), row(FOX_W)] + [_const_spec(a.shape) for a in (wz, bz, wpg, wpf, wo, g1, b1)],
        out_specs=row(D_MODEL),
        out_shape=jax.ShapeDtypeStruct((n, D_MODEL), f32),
        compiler_params=pltpu.CompilerParams(dimension_semantics=("arbitrary",), vmem_limit_bytes=VMEM_LIMIT),
        name="mix",
    )(x2d, og, of, wz, bz, wpg, wpf, wo, g1, b1)


def _ffn(x1, wg, wu, wd, g2, b2, alpha):
    n = x1.shape[0]
    tm = min(TOKEN_TILE, n)
    row = lambda w: pl.BlockSpec((tm, w), lambda i: (i, 0))
    return pl.pallas_call(
        functools.partial(_ffn_kernel, alpha=alpha, bounds=FFN_BOUNDS),
        grid=(n // tm,),
        in_specs=[row(D_MODEL)] + [_const_spec(a.shape) for a in (wg, wu, wd, g2, b2)],
        out_specs=row(D_MODEL),
        out_shape=jax.ShapeDtypeStruct((n, D_MODEL), f32),
        compiler_params=pltpu.CompilerParams(dimension_semantics=("arbitrary",), vmem_limit_bytes=VMEM_LIMIT),
        name="ffn",
    )(x1, wg, wu, wd, g2, b2)


def _prep_weights(w_in, b_in, w_alpha2, b_alpha2, gla_norm_g, w_proj_gla, w_proj_fox, w_out,
                  ln1_g, ln1_b, w_ffn_gate, w_ffn_up, w_ffn_down, ln2_g, ln2_b):
    def cols(a, idx):
        return a[..., _OFF[idx]:_OFF[idx + 1]]

    order = (0, 1, 2, 4, 5, 3)
    pad = SMALL_W - GLA_RANK
    w1 = jnp.concatenate([cols(w_in, i) for i in order] + [jnp.zeros((D_MODEL, pad), f32)], axis=1).astype(bf16)
    b1 = jnp.concatenate([cols(b_in, i) for i in order] + [jnp.zeros((pad,), f32)])[None, :]
    t_cols = (slice(_OFF[6], _OFF[9]),)
    ff_pad = FF_ROWS - FOX_HEADS
    wkv = jnp.pad(jnp.transpose(w_in[:, t_cols[0]]), ((0, ff_pad), (0, 0))).astype(bf16)
    bkv = jnp.broadcast_to(jnp.pad(b_in[t_cols[0]], (0, ff_pad))[:, None], (2 * FOX_W + FF_ROWS, LANES)).astype(f32)
    wa = jnp.zeros((SMALL_W, GLA_QK), f32).at[:GLA_RANK].set(w_alpha2).astype(bf16)
    head_of = np.arange(FOX_W) // FOX_DH
    indq = np.zeros((FOX_W, SMALL_W), np.float32)
    indq[np.arange(FOX_W), FOX_HEADS + head_of] = 1.0
    row = lambda a: a[None, :].astype(f32)
    return dict(
        w1=w1, b1=b1, wkv=wkv, bkv=bkv, wa=wa, ba=row(b_alpha2), indq=jnp.asarray(indq, bf16),
        g=row(gla_norm_g),
        wz=w_in[:, _OFF[9]:].astype(bf16), bz=row(b_in[_OFF[9]:]),
        wpg=w_proj_gla.astype(bf16), wpf=w_proj_fox.astype(bf16), wo=w_out.astype(bf16),
        g1=row(ln1_g), b1n=row(ln1_b),
        wg=w_ffn_gate.astype(bf16), wu=w_ffn_up.astype(bf16), wd=w_ffn_down.astype(bf16),
        g2=row(ln2_g), b2n=row(ln2_b),
    )


def _heads_last(t, B, L):
    return jnp.transpose(t.reshape(B, FOX_HEADS, FOX_DH, L), (0, 3, 1, 2))


def _layer(x, s0, k_past, v_past, lf_past, p, alpha):
    B, L, _ = x.shape
    n = B * L
    x2d = x.reshape(n, D_MODEL)
    prompt = k_past is None
    lo = L if prompt else n
    gq, gk, gv, la, sg, fq, fkt, fvt, fkbt, fvbt, lft, nk2t, qmx = _inproj(
        x2d, lo, p["w1"], p["b1"], p["wkv"], p["bkv"], p["wa"], p["ba"], p["indq"])
    r3 = lambda a: a.reshape(B, L, a.shape[-1])

    o_gla, st = _gla(r3(gq), r3(gk), r3(gv), r3(la), r3(sg), p["g"], _state_to_pairs(s0))
    gla_state = _pairs_to_state(st)

    if prompt:
        nc = L // LANES
        chunks = lambda t: t.reshape(B, FOX_HEADS, nc, LANES)
        ct, start = _fox_meta(chunks(lft), chunks(nk2t), qmx.reshape(B, nc, SMALL_W))
        o_fox = _fox_prompt(r3(fq), fkbt, fvbt, ct, start.reshape(B, nc))
        fk, fv = _heads_last(fkt, B, L), _heads_last(fvt, B, L)
        lf = jnp.transpose(lft, (0, 2, 1))
    else:
        past = k_past.shape[1]
        lk = past + L
        lpad = -(-lk // (8 * LANES)) * (8 * LANES)
        nc = lpad // LANES
        per_stream = lambda t: jnp.transpose(t.reshape(t.shape[1], B, L), (1, 0, 2))
        lft_s = per_stream(lft)
        lf_all_t = jnp.concatenate([jnp.transpose(lf_past.astype(f32), (0, 2, 1)), lft_s], axis=2)
        lf_all_t = jnp.pad(lf_all_t, ((0, 0), (0, 0), (0, lpad - lk))).reshape(B, FOX_HEADS, nc, LANES)
        ct = _fox_cumsum(lf_all_t)
        cache_t = lambda c: jnp.transpose(c, (0, 2, 3, 1)).reshape(B, FOX_W, past)
        o_fox = _fox_sample(r3(fq), per_stream(fkbt), per_stream(fvbt), cache_t(k_past), cache_t(v_past),
                            ct.reshape(B, FOX_HEADS, lpad))
        fk, fv = _heads_last(per_stream(fkt), B, L), _heads_last(per_stream(fvt), B, L)
        lf = jnp.transpose(lft_s, (0, 2, 1))

    x1 = _mix(x2d, o_gla.reshape(n, GLA_V), o_fox.reshape(n, FOX_W), p["wz"], p["bz"], p["wpg"], p["wpf"], p["wo"],
              p["g1"], p["b1n"], alpha)
    y = _ffn(x1, p["wg"], p["wu"], p["wd"], p["g2"], p["b2n"], alpha)
    return y.reshape(B, L, D_MODEL), gla_state, fk, fv, lf


def kernel(x_prompt, x_sample, state_gla, cache_fox_k, cache_fox_v, cache_fox_logf, w_in, b_in, w_alpha2, b_alpha2,
           gla_norm_g, w_proj_gla, w_proj_fox, w_out, ln1_g, ln1_b, w_ffn_gate, w_ffn_up, w_ffn_down, ln2_g, ln2_b):
    depth = w_in.shape[0]
    alpha = (2.0 * depth) ** 0.25
    yp, ys = x_prompt, x_sample
    outs_p, outs_s = [], []
    for l in range(depth):
        p = _prep_weights(w_in[l], b_in[l], w_alpha2[l], b_alpha2[l], gla_norm_g[l], w_proj_gla[l], w_proj_fox[l],
                          w_out[l], ln1_g[l], ln1_b[l], w_ffn_gate[l], w_ffn_up[l], w_ffn_down[l], ln2_g[l], ln2_b[l])
        s0 = jnp.zeros((yp.shape[0], GLA_HEADS, GLA_DK, GLA_DV), f32)
        yp, *rest_p = _layer(yp, s0, None, None, None, p, alpha)
        outs_p.append(rest_p)
        ys, *rest_s = _layer(ys, state_gla[l], cache_fox_k[l], cache_fox_v[l], cache_fox_logf[l], p, alpha)
        outs_s.append(rest_s)
    stack = lambda outs, i: jnp.stack([o[i] for o in outs])
    return (yp, ys,
            stack(outs_p, 0), stack(outs_p, 1), stack(outs_p, 2), stack(outs_p, 3),
            stack(outs_s, 0), stack(outs_s, 1), stack(outs_s, 2), stack(outs_s, 3))
```

```python
import functools

import jax
import jax.numpy as jnp
import numpy as np
from jax import lax
from jax.experimental import pallas as pl
from jax.experimental.pallas import tpu as pltpu

f32 = jnp.float32
bf16 = jnp.bfloat16

D_MODEL = 1024
GLA_HEADS = 4
GLA_DK = 64
GLA_DV = 128
GLA_RANK = 16
GLA_TAU = 16.0
GLA_QK = GLA_HEADS * GLA_DK
GLA_V = GLA_HEADS * GLA_DV
GLA_CHUNK = 64
FOX_HEADS = 8
FOX_DH = 64
FOX_W = FOX_HEADS * FOX_DH
D_FF = 2816
LN_EPS = 1e-5
RMS_EPS = 1e-5
SPLIT_SIZES = (GLA_QK, GLA_QK, GLA_V, GLA_RANK, GLA_V, FOX_W, FOX_W, FOX_W, FOX_HEADS, D_MODEL, D_MODEL)
_OFF = np.concatenate([[0], np.cumsum(SPLIT_SIZES)]).astype(int)

LANES = 128
MXU_DIM = 256
FFN_BOUNDS = (0, 4 * MXU_DIM, 8 * MXU_DIM, D_FF)
TOKEN_TILE = 1024
SMALL_W = LANES
FF_ROWS = 16
ATT_BLOCK = 128
FOX_PARTS = 4
LOG2E = 1.4426950408889634
SKIP_T2 = 110.0 * LOG2E
NORM_SLACK = 1.02
NEG_BIG = -1e30
VMEM_LIMIT = 56 * 1024 * 1024


def _dot(a, b):
    return jnp.dot(a, b, preferred_element_type=f32)


def _dot_nt(a, b):
    return lax.dot_general(a, b, (((1,), (1,)), ((), ())), preferred_element_type=f32)


def _dot_tn(a, b):
    return lax.dot_general(a, b, (((0,), (0,)), ((), ())), preferred_element_type=f32)


def _dot_hi(a, b):
    return jnp.dot(a, b, preferred_element_type=f32, precision=lax.Precision.HIGHEST)


def _log_sigmoid(x):
    return jnp.minimum(x, 0.0) - jnp.log1p(jnp.exp(-jnp.abs(x)))


def _const_spec(shape):
    return pl.BlockSpec(shape, lambda *_: (0,) * len(shape), pipeline_mode=pl.Buffered(1))


def _inproj_kernel(x_ref, w_ref, b_ref, wkv_ref, bkv_ref, wa_ref, ba_ref, indq_ref,
                   gq_ref, gk_ref, gv_ref, la_ref, sg_ref, fq_ref, fkt_ref, fvt_ref,
                   fkbt_ref, fvbt_ref, lft_ref, nk2t_ref, qmx_ref):
    xb = x_ref[...].astype(bf16)
    tm = x_ref.shape[0]

    def group(lo, width):
        return _dot(xb, w_ref[:, lo:lo + width]) + b_ref[:, lo:lo + width]

    o_gq, o_gk, o_gv = 0, GLA_QK, 2 * GLA_QK
    o_gr = o_gv + GLA_V
    o_fq = o_gr + GLA_V
    o_small = o_fq + FOX_W
    small_b = group(o_small, SMALL_W).astype(bf16)
    fq = (group(o_fq, FOX_W) * (FOX_DH ** -0.5 * LOG2E)).astype(bf16)
    fq_ref[...] = fq
    fq32 = fq.astype(f32)
    fq2 = (fq32 * fq32).astype(bf16)
    gq_ref[...] = (group(o_gq, GLA_QK) * (GLA_DK ** -0.5)).astype(bf16)
    gk_ref[...] = group(o_gk, GLA_QK).astype(bf16)
    gv_ref[...] = group(o_gv, GLA_V).astype(bf16)
    la_pre = _dot(small_b, wa_ref[...]) + ba_ref[...]
    la_ref[...] = _log_sigmoid(la_pre) * (1.0 / GLA_TAU)
    nq2 = _dot(fq2, indq_ref[...])
    qmx_ref[...] = jnp.max(nq2.reshape(tm // ATT_BLOCK, ATT_BLOCK, SMALL_W), axis=1)
    gr = group(o_gr, GLA_V)
    sg_ref[...] = (gr * jax.nn.sigmoid(gr)).astype(bf16)

    bias_t = jnp.concatenate([bkv_ref[...]] * (tm // LANES), axis=1)
    k_t = _dot_nt(wkv_ref[:FOX_W, :], xb) + bias_t[:FOX_W]
    fkt_ref[0] = k_t
    kb = k_t.astype(bf16)
    fkbt_ref[0] = kb
    k32 = kb.astype(f32)
    k2 = k32 * k32
    nk2t_ref[0] = jnp.concatenate(
        [jnp.sum(k2[h * FOX_DH:(h + 1) * FOX_DH], axis=0, keepdims=True) for h in range(FOX_HEADS)], axis=0)
    vf_t = _dot_nt(wkv_ref[FOX_W:, :], xb) + bias_t[FOX_W:]
    fvt_ref[0] = vf_t[:FOX_W]
    fvbt_ref[0] = vf_t[:FOX_W].astype(bf16)
    lft_ref[0] = _log_sigmoid(vf_t[FOX_W:FOX_W + FOX_HEADS])


def _inproj(x2d, lo, w1, b1, wkv, bkv, wa, ba, indq):
    n = x2d.shape[0]
    bo = n // lo
    tm = min(TOKEN_TILE, lo)
    assert lo % tm == 0 and tm % LANES == 0
    tpb = lo // tm
    row = lambda w: pl.BlockSpec((tm, w), lambda i: (i, 0))
    col = lambda w: pl.BlockSpec((1, w, tm), lambda i: (i // tpb, 0, i % tpb))
    rows_out = lambda w, dt: (jax.ShapeDtypeStruct((n, w), dt), row(w))
    cols_out = lambda w, dt: (jax.ShapeDtypeStruct((bo, w, lo), dt), col(w))
    outs = (
        rows_out(GLA_QK, bf16),
        rows_out(GLA_QK, bf16),
        rows_out(GLA_V, bf16),
        rows_out(GLA_QK, f32),
        rows_out(GLA_V, bf16),
        rows_out(FOX_W, bf16),
        cols_out(FOX_W, f32),
        cols_out(FOX_W, f32),
        cols_out(FOX_W, bf16),
        cols_out(FOX_W, bf16),
        cols_out(FOX_HEADS, f32),
        cols_out(FOX_HEADS, f32),
        (jax.ShapeDtypeStruct((n // ATT_BLOCK, SMALL_W), f32),
         pl.BlockSpec((tm // ATT_BLOCK, SMALL_W), lambda i: (i, 0))),
    )
    consts = (w1, b1, wkv, bkv, wa, ba, indq)
    return pl.pallas_call(
        _inproj_kernel,
        grid=(n // tm,),
        in_specs=[row(D_MODEL)] + [_const_spec(a.shape) for a in consts],
        out_specs=[o[1] for o in outs],
        out_shape=[o[0] for o in outs],
        compiler_params=pltpu.CompilerParams(dimension_semantics=("arbitrary",), vmem_limit_bytes=VMEM_LIMIT),
        name="inproj",
    )(x2d, *consts)


def _gla_kernel(q_ref, k_ref, v_ref, la_ref, sg_ref, g_ref, s0_ref, o_ref, sfin_ref, st_ref, *, chunk, nsub):
    t = pl.program_id(1)

    @pl.when(t == 0)
    def _():
        st_ref[...] = s0_ref[0]

    C = chunk
    tl = C * nsub
    r_i = lax.broadcasted_iota(jnp.int32, (C, C), 0)
    c_i = lax.broadcasted_iota(jnp.int32, (C, C), 1)
    causal = c_i <= r_i
    grp = min(tl, MXU_DIM)
    rt = lax.broadcasted_iota(jnp.int32, (grp, grp), 0)
    ct = lax.broadcasted_iota(jnp.int32, (grp, grp), 1)
    chunk_tril = ((ct <= rt) & (ct >= (rt // C) * C)).astype(bf16)
    lane = lax.broadcasted_iota(jnp.int32, (1, LANES), 1)
    first_half = lane < GLA_DK
    mid = C // 2
    pairs = range(GLA_HEADS // 2)

    def lanes(p):
        return slice(p * LANES, (p + 1) * LANES)

    def one_head(x, hh):
        return jnp.where(first_half if hh == 0 else jnp.logical_not(first_half), x, jnp.zeros_like(x))

    la = la_ref[0]
    la_hi = la.astype(bf16)
    la_lo = (la - la_hi.astype(f32)).astype(bf16)
    bcum_all = jnp.concatenate(
        [_dot(chunk_tril, la_hi[r:r + grp]) + _dot(chunk_tril, la_lo[r:r + grp]) for r in range(0, tl, grp)], axis=0)

    q_in, dec, amat, upd = [], [], {}, {}
    for c in range(nsub):
        rows = slice(c * C, (c + 1) * C)
        bcum = bcum_all[rows]
        b_last = bcum[C - 1:C, :]
        ref = bcum[mid:mid + 1, :]
        q = q_ref[0, rows, :].astype(f32)
        k = k_ref[0, rows, :].astype(f32)
        q_in.append((q * jnp.exp(bcum)).astype(bf16))
        q_a = (q * jnp.exp(jnp.clip(bcum - ref, -80.0, 80.0))).astype(bf16)
        k_a = (k * jnp.exp(jnp.clip(ref - bcum, -80.0, 80.0))).astype(bf16)
        k_d = (k * jnp.exp(b_last - bcum)).astype(bf16)
        dec.append(jnp.exp(b_last))
        for p in pairs:
            for hh in range(2):
                h = 2 * p + hh
                v_h = v_ref[0, rows, h * GLA_DV:(h + 1) * GLA_DV]
                a = _dot_nt(one_head(q_a[:, lanes(p)], hh), k_a[:, lanes(p)])
                amat[c, h] = jnp.where(causal, a, 0.0).astype(bf16)
                upd[c, h] = _dot_tn(v_h, k_d[:, lanes(p)])

    states = []
    st = [st_ref[p] for p in pairs]
    for c in range(nsub):
        states.append([s.astype(bf16) for s in st])
        st = [dec[c][:, lanes(p)] * st[p] + jnp.where(first_half, upd[c, 2 * p], upd[c, 2 * p + 1]) for p in pairs]
    for p in pairs:
        st_ref[p] = st[p]

    for c in range(nsub):
        rows = slice(c * C, (c + 1) * C)
        for h in range(GLA_HEADS):
            p, hh = divmod(h, 2)
            vs = slice(h * GLA_DV, (h + 1) * GLA_DV)
            o = _dot(amat[c, h], v_ref[0, rows, vs]) + _dot_nt(one_head(q_in[c][:, lanes(p)], hh), states[c][p])
            o = o * lax.rsqrt(jnp.mean(o * o, axis=-1, keepdims=True) + RMS_EPS)
            o = o * g_ref[:, vs] * sg_ref[0, rows, vs].astype(f32)
            o_ref[0, rows, vs] = o.astype(bf16)

    @pl.when(t == pl.num_programs(1) - 1)
    def _():
        sfin_ref[0] = st_ref[...]


def _gla(gq, gk, gv, la, sg, g, s0t):
    B, L, _ = gq.shape
    C = min(GLA_CHUNK, L)
    tl = min(1024, L)
    assert L % tl == 0 and tl % C == 0
    tok = lambda w: pl.BlockSpec((1, tl, w), lambda b, t: (b, t, 0))
    st_spec = pl.BlockSpec((1, 2, GLA_DV, LANES), lambda b, t: (b, 0, 0, 0))
    return pl.pallas_call(
        functools.partial(_gla_kernel, chunk=C, nsub=tl // C),
        grid=(B, L // tl),
        in_specs=[tok(GLA_QK), tok(GLA_QK), tok(GLA_V), tok(GLA_QK), tok(GLA_V),
                  pl.BlockSpec((1, GLA_V), lambda b, t: (0, 0)), st_spec],
        out_specs=[tok(GLA_V), st_spec],
        out_shape=(jax.ShapeDtypeStruct((B, L, GLA_V), bf16),
                   jax.ShapeDtypeStruct((B, 2, GLA_DV, LANES), f32)),
        scratch_shapes=[pltpu.VMEM((2, GLA_DV, LANES), f32)],
        compiler_params=pltpu.CompilerParams(dimension_semantics=("arbitrary", "arbitrary"),
                                             vmem_limit_bytes=VMEM_LIMIT),
        name="gla_scan",
    )(gq, gk, gv, la, sg, g, s0t)


def _state_to_pairs(s):
    B = s.shape[0]
    s = s.reshape(B, 2, 2, GLA_DK, GLA_DV)
    return jnp.transpose(s, (0, 1, 4, 2, 3)).reshape(B, 2, GLA_DV, 2 * GLA_DK)


def _pairs_to_state(st):
    B = st.shape[0]
    st = st.reshape(B, 2, GLA_DV, 2, GLA_DK)
    return jnp.transpose(st, (0, 1, 3, 4, 2)).reshape(B, GLA_HEADS, GLA_DK, GLA_DV)


def _meta_kernel(lf_ref, nk_ref, qm_ref, ct_ref, start_ref, *, nc):
    H = FOX_HEADS
    r_i = lax.broadcasted_iota(jnp.int32, (LANES, LANES), 0)
    c_i = lax.broadcasted_iota(jnp.int32, (LANES, LANES), 1)
    upper = (r_i <= c_i).astype(f32)
    rj = lax.broadcasted_iota(jnp.int32, (nc, nc), 0)
    cj = lax.broadcasted_iota(jnp.int32, (nc, nc), 1)
    strict_lower = (cj < rj).astype(f32)
    eye = rj == cj

    def to_row(col):
        return jnp.sum(jnp.where(eye, jnp.broadcast_to(col, (nc, nc)), 0.0), axis=0, keepdims=True)

    ys = [_dot_hi(lf_ref[0, h], upper) for h in range(H)]
    prev = [_dot_hi(strict_lower, jnp.broadcast_to(y[:, LANES - 1:LANES], (nc, LANES))) for y in ys]
    start = None
    for h in range(H):
        c = (ys[h] + prev[h]) * LOG2E
        ct_ref[0, h] = c
        cmax = jnp.max(c, axis=1, keepdims=True)
        cmin = jnp.min(c, axis=1, keepdims=True)
        qmax = jnp.sqrt(qm_ref[0, :, H + h:H + h + 1])
        kmax = jnp.sqrt(jnp.max(jnp.max(nk_ref[0, h], axis=1, keepdims=True), axis=0, keepdims=True))
        thr = cmax + (2.0 * NORM_SLACK) * qmax * kmax + SKIP_T2
        needed = (to_row(cmin) <= thr) & (cj <= rj)
        first = jnp.min(jnp.where(needed, cj, nc), axis=1, keepdims=True)
        start = first if start is None else jnp.minimum(start, first)
    start_ref[0] = to_row(start.astype(f32)).astype(jnp.int32)


def _fox_meta(lf_t, nk2_t, qmx):
    B, _, nc, _ = lf_t.shape
    per_head = pl.BlockSpec((1, FOX_HEADS, nc, LANES), lambda b: (b, 0, 0, 0))
    return pl.pallas_call(
        functools.partial(_meta_kernel, nc=nc),
        grid=(B,),
        in_specs=[per_head, per_head, pl.BlockSpec((1, nc, LANES), lambda b: (b, 0, 0))],
        out_specs=[pl.BlockSpec((1, FOX_HEADS, nc, LANES), lambda b: (b, 0, 0, 0)),
                   pl.BlockSpec((1, 1, nc), lambda b: (b, 0, 0))],
        out_shape=(jax.ShapeDtypeStruct((B, FOX_HEADS, nc, LANES), f32),
                   jax.ShapeDtypeStruct((B, 1, nc), jnp.int32)),
        compiler_params=pltpu.CompilerParams(dimension_semantics=("arbitrary",), vmem_limit_bytes=VMEM_LIMIT),
        name="fox_meta",
    )(lf_t, nk2_t, qmx)


def _cumsum_kernel(lf_ref, ct_ref, *, nb, nc):
    r_i = lax.broadcasted_iota(jnp.int32, (LANES, LANES), 0)
    c_i = lax.broadcasted_iota(jnp.int32, (LANES, LANES), 1)
    upper = (r_i <= c_i).astype(f32)
    rj = lax.broadcasted_iota(jnp.int32, (nc, nc), 0)
    cj = lax.broadcasted_iota(jnp.int32, (nc, nc), 1)
    strict_lower = (cj < rj).astype(f32)
    seqs = [(s, h) for s in range(nb) for h in range(FOX_HEADS)]
    ys = [_dot_hi(lf_ref[s, h], upper) for s, h in seqs]
    prev = [_dot_hi(strict_lower, jnp.broadcast_to(y[:, LANES - 1:LANES], (nc, LANES))) for y in ys]
    for (s, h), y, p in zip(seqs, ys, prev):
        ct_ref[s, h] = (y + p) * LOG2E


def _fox_cumsum(lf_t):
    B, _, nc, _ = lf_t.shape
    nb = next(d for d in (4, 2, 1) if B % d == 0)
    spec = pl.BlockSpec((nb, FOX_HEADS, nc, LANES), lambda b: (b, 0, 0, 0))
    return pl.pallas_call(
        functools.partial(_cumsum_kernel, nb=nb, nc=nc),
        grid=(B // nb,),
        in_specs=[spec],
        out_specs=spec,
        out_shape=jax.ShapeDtypeStruct(lf_t.shape, f32),
        compiler_params=pltpu.CompilerParams(dimension_semantics=("arbitrary",), vmem_limit_bytes=VMEM_LIMIT),
        name="fox_cumsum",
    )(lf_t)


def _head_rows(h):
    return slice(h * FOX_DH, (h + 1) * FOX_DH)


def _pair_slab(h, width=LANES):
    return slice((h // 2) * width, (h // 2 + 1) * width)


def _v_aug(vt_h, h):
    one = jnp.ones_like(vt_h)
    return jnp.concatenate([vt_h, one] if h % 2 == 0 else [one, vt_h], axis=0)


def _fox_prompt_kernel(start_ref, q_ref, kp_ref, kc_ref, vp_ref, vc_ref,
                       ct_ref, kh_ref, vh_ref, o_ref, m_s, acc_s, kbuf, vbuf, sem, *, parts):
    T = ATT_BLOCK
    TQ = 2 * T
    H = FOX_HEADS
    b = pl.program_id(0)
    i = pl.program_id(1)
    r_i = lax.broadcasted_iota(jnp.int32, (TQ, 2 * TQ), 0)
    c_i = lax.broadcasted_iota(jnp.int32, (TQ, 2 * TQ), 1)
    visible = c_i <= r_i + TQ
    lane = lax.broadcasted_iota(jnp.int32, (1, LANES), 1)
    first_half = lane < FOX_DH
    halves = range(parts)
    bps = 2 * parts

    def q_rows(u):
        return slice(u * TQ, (u + 1) * TQ)

    def first_block(u):
        return bps * i + 2 * u - 2

    def head_q(u, h):
        hm = first_half if h % 2 == 0 else jnp.logical_not(first_half)
        qp = q_ref[0, q_rows(u), _pair_slab(h)]
        return jnp.where(hm, qp, jnp.zeros_like(qp))

    def head_base(u, h):
        return ct_ref[0, h, pl.ds(bps * i + 2 * u, 1), 0:1]

    def c_row(h, j):
        return ct_ref[0, h, pl.ds(j, 1), :]

    def window(u, prev_ref, cur_ref, rows):
        if u == 0:
            return jnp.concatenate([prev_ref[0, rows, :], cur_ref[0, rows, 0:TQ]], axis=1)
        return cur_ref[0, rows, (u - 1) * TQ:(u + 1) * TQ]

    scores = {}
    for u in halves:
        j0 = first_block(u)
        jc = jnp.maximum(j0, 0)
        for h in range(H):
            base = head_base(u, h)
            ck = jnp.concatenate([jnp.where(j0 >= 0, c_row(h, jc) - base, -NEG_BIG),
                                  jnp.where(j0 >= 0, c_row(h, jc + 1) - base, -NEG_BIG),
                                  c_row(h, j0 + 2) - base, c_row(h, j0 + 3) - base], axis=1)
            s = jnp.where(visible, _dot(head_q(u, h), window(u, kp_ref, kc_ref, _pair_slab(h))) - ck, NEG_BIG)
            m = jnp.max(s, axis=1, keepdims=True)
            m_s[h, q_rows(u), :] = m
            scores[u, h] = (s, m)
    for u in halves:
        for h in range(H):
            s, m = scores[u, h]
            acc_s[h, q_rows(u), :] = _dot_nt(jnp.exp2(s - m).astype(bf16),
                                            _v_aug(window(u, vp_ref, vc_ref, _head_rows(h)), h))

    def copies(j, slot):
        cols = pl.ds(pl.multiple_of(j * T, T), T)
        return (pltpu.make_async_copy(kh_ref.at[b, :, cols], kbuf.at[slot], sem.at[0, slot]),
                pltpu.make_async_copy(vh_ref.at[b, :, cols], vbuf.at[slot], sem.at[1, slot]))

    for u in halves:
        start = jnp.minimum(start_ref[b, bps * i + 2 * u], start_ref[b, bps * i + 2 * u + 1])
        n_far = jnp.maximum(first_block(u) - start, 0)

        @pl.when(n_far > 0)
        def _(u=u, start=start, n_far=n_far):
            for cp in copies(start, 0):
                cp.start()

            def body(t, carry):
                j = start + t
                slot = lax.rem(t, 2)
                for cp in copies(j, slot):
                    cp.wait()

                @pl.when(t + 1 < n_far)
                def _():
                    for cp in copies(j + 1, 1 - slot):
                        cp.start()

                for h in range(H):
                    ck = c_row(h, j) - head_base(u, h)
                    s = _dot(head_q(u, h), kbuf[slot, _pair_slab(h), :]) - ck
                    m_prev = m_s[h, q_rows(u), :]
                    m_new = jnp.maximum(m_prev, jnp.max(s, axis=1, keepdims=True))
                    alpha = jnp.exp2(m_prev - m_new)
                    p = jnp.exp2(s - m_new)
                    acc_s[h, q_rows(u), :] = (alpha * acc_s[h, q_rows(u), :]
                                             + _dot_nt(p.astype(bf16), _v_aug(vbuf[slot, _head_rows(h), :], h)))
                    m_s[h, q_rows(u), :] = m_new
                return carry

            lax.fori_loop(0, n_far, body, 0)

    for p in range(H // 2):
        even = acc_s[2 * p]
        odd = acc_s[2 * p + 1]
        numer = jnp.where(first_half, even, odd)
        denom = pltpu.roll(jnp.where(first_half, odd, even), FOX_DH, 1)
        o_ref[0, :, p * LANES:(p + 1) * LANES] = (numer / denom).astype(bf16)


def _fox_prompt(fq, fkbt, fvbt, ct, start):
    B, L, _ = fq.shape
    T = ATT_BLOCK
    TQ = 2 * T
    parts = next(d for d in (FOX_PARTS, 2, 1) if L % (d * TQ) == 0)
    TS = parts * TQ
    prev = pl.BlockSpec((1, FOX_W, TQ), lambda b, i, s: (b, 0, jnp.maximum(parts * i - 1, 0)))
    cur = pl.BlockSpec((1, FOX_W, TS), lambda b, i, s: (b, 0, i))
    grid_spec = pltpu.PrefetchScalarGridSpec(
        num_scalar_prefetch=1,
        grid=(B, L // TS),
        in_specs=[pl.BlockSpec((1, TS, FOX_W), lambda b, i, s: (b, i, 0)),
                  prev, cur, prev, cur,
                  pl.BlockSpec((1, FOX_HEADS, L // T, LANES), lambda b, i, s: (b, 0, 0, 0)),
                  pl.BlockSpec(memory_space=pl.ANY), pl.BlockSpec(memory_space=pl.ANY)],
        out_specs=pl.BlockSpec((1, TS, FOX_W), lambda b, i, s: (b, i, 0)),
        scratch_shapes=[pltpu.VMEM((FOX_HEADS, TS, 1), f32),
                        pltpu.VMEM((FOX_HEADS, TS, LANES), f32),
                        pltpu.VMEM((2, FOX_W, T), bf16), pltpu.VMEM((2, FOX_W, T), bf16),
                        pltpu.SemaphoreType.DMA((2, 2))],
    )
    return pl.pallas_call(
        functools.partial(_fox_prompt_kernel, parts=parts),
        grid_spec=grid_spec,
        out_shape=jax.ShapeDtypeStruct((B, L, FOX_W), bf16),
        compiler_params=pltpu.CompilerParams(dimension_semantics=("arbitrary", "arbitrary"),
                                             vmem_limit_bytes=VMEM_LIMIT),
        name="fox_prompt",
    )(start, fq, fkbt, fkbt, fvbt, fvbt, ct, fkbt, fvbt)


def _fox_sample_kernel(q_ref, kn_ref, vn_ref, kp_ref, vp_ref, ct_ref, o_ref, *, past, lq):
    H = FOX_HEADS
    r_i = lax.broadcasted_iota(jnp.int32, (lq, lq), 0)
    c_i = lax.broadcasted_iota(jnp.int32, (lq, lq), 1)
    causal = c_i <= r_i
    lane = lax.broadcasted_iota(jnp.int32, (1, LANES), 1)
    first_half = lane < FOX_DH
    probs = []
    for h in range(H):
        hm = first_half if h % 2 == 0 else jnp.logical_not(first_half)
        qp = q_ref[0, :, _pair_slab(h)]
        qm = jnp.where(hm, qp, jnp.zeros_like(qp))
        base = ct_ref[0, h:h + 1, past:past + 1]
        s_p = _dot(qm, kp_ref[0, _pair_slab(h), :].astype(bf16)) - (ct_ref[0, h:h + 1, 0:past] - base)
        s_n = _dot(qm, kn_ref[0, _pair_slab(h), :]) - (ct_ref[0, h:h + 1, past:past + lq] - base)
        s_n = jnp.where(causal, s_n, NEG_BIG)
        m = jnp.maximum(jnp.max(s_p, axis=1, keepdims=True), jnp.max(s_n, axis=1, keepdims=True))
        probs.append((jnp.exp2(s_p - m).astype(bf16), jnp.exp2(s_n - m).astype(bf16)))
    accs = []
    for h in range(H):
        p_p, p_n = probs[h]
        accs.append(_dot_nt(p_p, _v_aug(vp_ref[0, _head_rows(h), :].astype(bf16), h))
                    + _dot_nt(p_n, _v_aug(vn_ref[0, _head_rows(h), :], h)))
    for p in range(H // 2):
        numer = jnp.where(first_half, accs[2 * p], accs[2 * p + 1])
        denom = pltpu.roll(jnp.where(first_half, accs[2 * p + 1], accs[2 * p]), FOX_DH, 1)
        o_ref[0, :, p * LANES:(p + 1) * LANES] = (numer / denom).astype(bf16)


def _fox_sample(fq, fkbt, fvbt, kt_past, vt_past, ct_flat):
    B, lq, _ = fq.shape
    past = kt_past.shape[2]
    lpad = ct_flat.shape[2]
    new = pl.BlockSpec((1, FOX_W, lq), lambda b: (b, 0, 0))
    old = pl.BlockSpec((1, FOX_W, past), lambda b: (b, 0, 0))
    qo = pl.BlockSpec((1, lq, FOX_W), lambda b: (b, 0, 0))
    return pl.pallas_call(
        functools.partial(_fox_sample_kernel, past=past, lq=lq),
        grid=(B,),
        in_specs=[qo, new, new, old, old, pl.BlockSpec((1, FOX_HEADS, lpad), lambda b: (b, 0, 0))],
        out_specs=qo,
        out_shape=jax.ShapeDtypeStruct((B, lq, FOX_W), bf16),
        compiler_params=pltpu.CompilerParams(dimension_semantics=("arbitrary",), vmem_limit_bytes=VMEM_LIMIT),
        name="fox_sample",
    )(fq, fkbt, fvbt, kt_past, vt_past, ct_flat)


def _layer_norm(t, g, b):
    mu = jnp.mean(t, axis=-1, keepdims=True)
    d = t - mu
    var = jnp.mean(d * d, axis=-1, keepdims=True)
    return d * lax.rsqrt(var + LN_EPS) * g + b


def _mix_kernel(x_ref, og_ref, of_ref, wz_ref, bz_ref, wpg_ref, wpf_ref, wo_ref, g1_ref, b1_ref, x1_ref, *, alpha):
    groups = _row_groups(x_ref.shape[0])
    merged = []
    for rows in groups:
        xb = x_ref[rows, :].astype(bf16)
        zg = _dot(xb, wz_ref[:, :D_MODEL]) + bz_ref[:, :D_MODEL]
        zf = _dot(xb, wz_ref[:, D_MODEL:]) + bz_ref[:, D_MODEL:]
        m = (jax.nn.sigmoid(zg) * _dot(og_ref[rows, :], wpg_ref[...])
             + jax.nn.sigmoid(zf) * _dot(of_ref[rows, :], wpf_ref[...]))
        merged.append(m.astype(bf16))
    for rows, m in zip(groups, merged):
        t = alpha * x_ref[rows, :] + _dot(m, wo_ref[...])
        x1_ref[rows, :] = _layer_norm(t, g1_ref[...], b1_ref[...])


def _row_groups(tm):
    rows = MXU_DIM if tm % MXU_DIM == 0 else tm
    return tuple(slice(r, r + rows) for r in range(0, tm, rows))


def _ffn_kernel(x1_ref, wg_ref, wu_ref, wd_ref, g2_ref, b2_ref, y_ref, *, alpha, bounds):
    chunks = [slice(lo, hi) for lo, hi in zip(bounds[:-1], bounds[1:])]
    stages = [(rows, cs) for rows in _row_groups(x1_ref.shape[0]) for cs in chunks]
    acc = {}
    pending = None

    def down(rows, cs, hid):
        part = _dot(hid, wd_ref[cs, :])
        key = (rows.start, rows.stop)
        acc[key] = part if key not in acc else acc[key] + part
        if cs is chunks[-1]:
            x1 = x1_ref[rows, :]
            y_ref[rows, :] = _layer_norm(alpha * x1 + acc[key], g2_ref[...], b2_ref[...])

    for rows, cs in stages:
        xb = x1_ref[rows, :].astype(bf16)
        g = _dot(xb, wg_ref[:, cs])
        u = _dot(xb, wu_ref[:, cs])
        hid = (g * jax.nn.sigmoid(g) * u).astype(bf16)
        if pending is not None:
            down(*pending)
        pending = (rows, cs, hid)
    down(*pending)


def _mix(x2d, og, of, wz, bz, wpg, wpf, wo, g1, b1, alpha):
    n = x2d.shape[0]
    tm = min(TOKEN_TILE, n)
    row = lambda w: pl.BlockSpec((tm, w), lambda i: (i, 0))
    return pl.pallas_call(
        functools.partial(_mix_kernel, alpha=alpha),
        grid=(n // tm,),
        in_specs=[row(D_MODEL), row(GLA_V), row(FOX_W)] + [_const_spec(a.shape) for a in (wz, bz, wpg, wpf, wo, g1, b1)],
        out_specs=row(D_MODEL),
        out_shape=jax.ShapeDtypeStruct((n, D_MODEL), f32),
        compiler_params=pltpu.CompilerParams(dimension_semantics=("arbitrary",), vmem_limit_bytes=VMEM_LIMIT),
        name="mix",
    )(x2d, og, of, wz, bz, wpg, wpf, wo, g1, b1)


def _ffn(x1, wg, wu, wd, g2, b2, alpha):
    n = x1.shape[0]
    tm = min(TOKEN_TILE, n)
    row = lambda w: pl.BlockSpec((tm, w), lambda i: (i, 0))
    return pl.pallas_call(
        functools.partial(_ffn_kernel, alpha=alpha, bounds=FFN_BOUNDS),
        grid=(n // tm,),
        in_specs=[row(D_MODEL)] + [_const_spec(a.shape) for a in (wg, wu, wd, g2, b2)],
        out_specs=row(D_MODEL),
        out_shape=jax.ShapeDtypeStruct((n, D_MODEL), f32),
        compiler_params=pltpu.CompilerParams(dimension_semantics=("arbitrary",), vmem_limit_bytes=VMEM_LIMIT),
        name="ffn",
    )(x1, wg, wu, wd, g2, b2)


def _prep_weights(w_in, b_in, w_alpha2, b_alpha2, gla_norm_g, w_proj_gla, w_proj_fox, w_out,
                  ln1_g, ln1_b, w_ffn_gate, w_ffn_up, w_ffn_down, ln2_g, ln2_b):
    def cols(a, idx):
        return a[..., _OFF[idx]:_OFF[idx + 1]]

    order = (0, 1, 2, 4, 5, 3)
    pad = SMALL_W - GLA_RANK
    w1 = jnp.concatenate([cols(w_in, i).astype(bf16) for i in order] + [jnp.zeros((D_MODEL, pad), bf16)], axis=1)
    b1 = jnp.concatenate([cols(b_in, i) for i in order] + [jnp.zeros((pad,), f32)])[None, :]
    t_cols = (slice(_OFF[6], _OFF[9]),)
    ff_pad = FF_ROWS - FOX_HEADS
    wkv = jnp.pad(jnp.transpose(w_in[:, t_cols[0]].astype(bf16)), ((0, ff_pad), (0, 0)))
    bkv = jnp.broadcast_to(jnp.pad(b_in[t_cols[0]], (0, ff_pad))[:, None], (2 * FOX_W + FF_ROWS, LANES)).astype(f32)
    wa = jnp.zeros((SMALL_W, GLA_QK), f32).at[:GLA_RANK].set(w_alpha2).astype(bf16)
    head_of = np.arange(FOX_W) // FOX_DH
    indq = np.zeros((FOX_W, SMALL_W), np.float32)
    indq[np.arange(FOX_W), FOX_HEADS + head_of] = 1.0
    row = lambda a: a[None, :].astype(f32)
    return dict(
        w1=w1, b1=b1, wkv=wkv, bkv=bkv, wa=wa, ba=row(b_alpha2), indq=jnp.asarray(indq, bf16),
        g=row(gla_norm_g),
        wz=w_in[:, _OFF[9]:].astype(bf16), bz=row(b_in[_OFF[9]:]),
        wpg=w_proj_gla.astype(bf16), wpf=w_proj_fox.astype(bf16), wo=w_out.astype(bf16),
        g1=row(ln1_g), b1n=row(ln1_b),
        wg=w_ffn_gate.astype(bf16), wu=w_ffn_up.astype(bf16), wd=w_ffn_down.astype(bf16),
        g2=row(ln2_g), b2n=row(ln2_b),
    )


def _heads_last(t, B, L):
    return jnp.transpose(t.reshape(B, FOX_HEADS, FOX_DH, L), (0, 3, 1, 2))


def _layer(x, s0, k_past, v_past, lf_past, p, alpha):
    B, L, _ = x.shape
    n = B * L
    x2d = x.reshape(n, D_MODEL)
    prompt = k_past is None
    lo = L if prompt else n
    gq, gk, gv, la, sg, fq, fkt, fvt, fkbt, fvbt, lft, nk2t, qmx = _inproj(
        x2d, lo, p["w1"], p["b1"], p["wkv"], p["bkv"], p["wa"], p["ba"], p["indq"])
    r3 = lambda a: a.reshape(B, L, a.shape[-1])

    o_gla, st = _gla(r3(gq), r3(gk), r3(gv), r3(la), r3(sg), p["g"], _state_to_pairs(s0))
    gla_state = _pairs_to_state(st)

    if prompt:
        nc = L // LANES
        chunks = lambda t: t.reshape(B, FOX_HEADS, nc, LANES)
        ct, start = _fox_meta(chunks(lft), chunks(nk2t), qmx.reshape(B, nc, SMALL_W))
        o_fox = _fox_prompt(r3(fq), fkbt, fvbt, ct, start.reshape(B, nc))
        fk, fv = _heads_last(fkt, B, L), _heads_last(fvt, B, L)
        lf = jnp.transpose(lft, (0, 2, 1))
    else:
        past = k_past.shape[1]
        lk = past + L
        lpad = -(-lk // (8 * LANES)) * (8 * LANES)
        nc = lpad // LANES
        per_stream = lambda t: jnp.transpose(t.reshape(t.shape[1], B, L), (1, 0, 2))
        lft_s = per_stream(lft)
        lf_all_t = jnp.concatenate([jnp.transpose(lf_past.astype(f32), (0, 2, 1)), lft_s], axis=2)
        lf_all_t = jnp.pad(lf_all_t, ((0, 0), (0, 0), (0, lpad - lk))).reshape(B, FOX_HEADS, nc, LANES)
        ct = _fox_cumsum(lf_all_t)
        cache_t = lambda c: jnp.transpose(c, (0, 2, 3, 1)).reshape(B, FOX_W, past)
        o_fox = _fox_sample(r3(fq), per_stream(fkbt), per_stream(fvbt), cache_t(k_past), cache_t(v_past),
                            ct.reshape(B, FOX_HEADS, lpad))
        fk, fv = _heads_last(per_stream(fkt), B, L), _heads_last(per_stream(fvt), B, L)
        lf = jnp.transpose(lft_s, (0, 2, 1))

    x1 = _mix(x2d, o_gla.reshape(n, GLA_V), o_fox.reshape(n, FOX_W), p["wz"], p["bz"], p["wpg"], p["wpf"], p["wo"],
              p["g1"], p["b1n"], alpha)
    y = _ffn(x1, p["wg"], p["wu"], p["wd"], p["g2"], p["b2n"], alpha)
    return y.reshape(B, L, D_MODEL), gla_state, fk, fv, lf


def kernel(x_prompt, x_sample, state_gla, cache_fox_k, cache_fox_v, cache_fox_logf, w_in, b_in, w_alpha2, b_alpha2,
           gla_norm_g, w_proj_gla, w_proj_fox, w_out, ln1_g, ln1_b, w_ffn_gate, w_ffn_up, w_ffn_down, ln2_g, ln2_b):
    depth = w_in.shape[0]
    alpha = (2.0 * depth) ** 0.25
    yp, ys = x_prompt, x_sample
    outs_p, outs_s = [], []
    for l in range(depth):
        p = _prep_weights(w_in[l], b_in[l], w_alpha2[l], b_alpha2[l], gla_norm_g[l], w_proj_gla[l], w_proj_fox[l],
                          w_out[l], ln1_g[l], ln1_b[l], w_ffn_gate[l], w_ffn_up[l], w_ffn_down[l], ln2_g[l], ln2_b[l])
        s0 = jnp.zeros((yp.shape[0], GLA_HEADS, GLA_DK, GLA_DV), f32)
        yp, *rest_p = _layer(yp, s0, None, None, None, p, alpha)
        outs_p.append(rest_p)
        ys, *rest_s = _layer(ys, state_gla[l], cache_fox_k[l], cache_fox_v[l], cache_fox_logf[l], p, alpha)
        outs_s.append(rest_s)
    stack = lambda outs, i: jnp.stack([o[i] for o in outs])
    return (yp, ys,
            stack(outs_p, 0), stack(outs_p, 1), stack(outs_p, 2), stack(outs_p, 3),
            stack(outs_s, 0), stack(outs_s, 1), stack(outs_s, 2), stack(outs_s, 3))
```

```python
import functools

import jax
import jax.numpy as jnp
import numpy as np
from jax import lax
from jax.experimental import pallas as pl
from jax.experimental.pallas import tpu as pltpu

f32 = jnp.float32
bf16 = jnp.bfloat16

D_MODEL = 1024
GLA_HEADS = 4
GLA_DK = 64
GLA_DV = 128
GLA_RANK = 16
GLA_TAU = 16.0
GLA_QK = GLA_HEADS * GLA_DK
GLA_V = GLA_HEADS * GLA_DV
GLA_CHUNK = 64
FOX_HEADS = 8
FOX_DH = 64
FOX_W = FOX_HEADS * FOX_DH
D_FF = 2816
LN_EPS = 1e-5
RMS_EPS = 1e-5
SPLIT_SIZES = (GLA_QK, GLA_QK, GLA_V, GLA_RANK, GLA_V, FOX_W, FOX_W, FOX_W, FOX_HEADS, D_MODEL, D_MODEL)
_OFF = np.concatenate([[0], np.cumsum(SPLIT_SIZES)]).astype(int)

LANES = 128
MXU_DIM = 256
FFN_BOUNDS = (0, 4 * MXU_DIM, 8 * MXU_DIM, D_FF)
TOKEN_TILE = 1024
SMALL_W = LANES
FF_ROWS = 16
ATT_BLOCK = 128
FOX_PARTS = 2
LOG2E = 1.4426950408889634
SKIP_T2 = 110.0 * LOG2E
NORM_SLACK = 1.02
NEG_BIG = -1e30
VMEM_LIMIT = 56 * 1024 * 1024


def _dot(a, b):
    return jnp.dot(a, b, preferred_element_type=f32)


def _dot_nt(a, b):
    return lax.dot_general(a, b, (((1,), (1,)), ((), ())), preferred_element_type=f32)


def _dot_tn(a, b):
    return lax.dot_general(a, b, (((0,), (0,)), ((), ())), preferred_element_type=f32)


def _dot_hi(a, b):
    return jnp.dot(a, b, preferred_element_type=f32, precision=lax.Precision.HIGHEST)


def _log_sigmoid(x):
    return jnp.minimum(x, 0.0) - jnp.log1p(jnp.exp(-jnp.abs(x)))


def _const_spec(shape):
    return pl.BlockSpec(shape, lambda *_: (0,) * len(shape), pipeline_mode=pl.Buffered(1))


def _inproj_kernel(x_ref, w_ref, b_ref, wkv_ref, bkv_ref, wa_ref, ba_ref, indq_ref,
                   gq_ref, gk_ref, gv_ref, la_ref, sg_ref, fq_ref, fkt_ref, fvt_ref,
                   fkbt_ref, fvbt_ref, lft_ref, nk2t_ref, qmx_ref):
    xb = x_ref[...].astype(bf16)
    tm = x_ref.shape[0]

    def group(lo, width):
        return _dot(xb, w_ref[:, lo:lo + width]) + b_ref[:, lo:lo + width]

    o_gq, o_gk, o_gv = 0, GLA_QK, 2 * GLA_QK
    o_gr = o_gv + GLA_V
    o_fq = o_gr + GLA_V
    o_small = o_fq + FOX_W
    small_b = group(o_small, SMALL_W).astype(bf16)
    fq = (group(o_fq, FOX_W) * (FOX_DH ** -0.5 * LOG2E)).astype(bf16)
    fq_ref[...] = fq
    fq32 = fq.astype(f32)
    fq2 = (fq32 * fq32).astype(bf16)
    gq_ref[...] = (group(o_gq, GLA_QK) * (GLA_DK ** -0.5)).astype(bf16)
    gk_ref[...] = group(o_gk, GLA_QK).astype(bf16)
    gv_ref[...] = group(o_gv, GLA_V).astype(bf16)
    la_pre = _dot(small_b, wa_ref[...]) + ba_ref[...]
    la_ref[...] = _log_sigmoid(la_pre) * (1.0 / GLA_TAU)
    nq2 = _dot(fq2, indq_ref[...])
    qmx_ref[...] = jnp.max(nq2.reshape(tm // ATT_BLOCK, ATT_BLOCK, SMALL_W), axis=1)
    gr = group(o_gr, GLA_V)
    sg_ref[...] = (gr * jax.nn.sigmoid(gr)).astype(bf16)

    bias_t = jnp.concatenate([bkv_ref[...]] * (tm // LANES), axis=1)
    k_t = _dot_nt(wkv_ref[:FOX_W, :], xb) + bias_t[:FOX_W]
    fkt_ref[0] = k_t
    kb = k_t.astype(bf16)
    fkbt_ref[0] = kb
    k32 = kb.astype(f32)
    k2 = k32 * k32
    nk2t_ref[0] = jnp.concatenate(
        [jnp.sum(k2[h * FOX_DH:(h + 1) * FOX_DH], axis=0, keepdims=True) for h in range(FOX_HEADS)], axis=0)
    vf_t = _dot_nt(wkv_ref[FOX_W:, :], xb) + bias_t[FOX_W:]
    fvt_ref[0] = vf_t[:FOX_W]
    fvbt_ref[0] = vf_t[:FOX_W].astype(bf16)
    lft_ref[0] = _log_sigmoid(vf_t[FOX_W:FOX_W + FOX_HEADS])


def _inproj(x2d, lo, w1, b1, wkv, bkv, wa, ba, indq):
    n = x2d.shape[0]
    bo = n // lo
    tm = min(TOKEN_TILE, lo)
    assert lo % tm == 0 and tm % LANES == 0
    tpb = lo // tm
    row = lambda w: pl.BlockSpec((tm, w), lambda i: (i, 0))
    col = lambda w: pl.BlockSpec((1, w, tm), lambda i: (i // tpb, 0, i % tpb))
    rows_out = lambda w, dt: (jax.ShapeDtypeStruct((n, w), dt), row(w))
    cols_out = lambda w, dt: (jax.ShapeDtypeStruct((bo, w, lo), dt), col(w))
    outs = (
        rows_out(GLA_QK, bf16),
        rows_out(GLA_QK, bf16),
        rows_out(GLA_V, bf16),
        rows_out(GLA_QK, f32),
        rows_out(GLA_V, bf16),
        rows_out(FOX_W, bf16),
        cols_out(FOX_W, f32),
        cols_out(FOX_W, f32),
        cols_out(FOX_W, bf16),
        cols_out(FOX_W, bf16),
        cols_out(FOX_HEADS, f32),
        cols_out(FOX_HEADS, f32),
        (jax.ShapeDtypeStruct((n // ATT_BLOCK, SMALL_W), f32),
         pl.BlockSpec((tm // ATT_BLOCK, SMALL_W), lambda i: (i, 0))),
    )
    consts = (w1, b1, wkv, bkv, wa, ba, indq)
    return pl.pallas_call(
        _inproj_kernel,
        grid=(n // tm,),
        in_specs=[row(D_MODEL)] + [_const_spec(a.shape) for a in consts],
        out_specs=[o[1] for o in outs],
        out_shape=[o[0] for o in outs],
        compiler_params=pltpu.CompilerParams(dimension_semantics=("arbitrary",), vmem_limit_bytes=VMEM_LIMIT),
        name="inproj",
    )(x2d, *consts)


def _gla_kernel(q_ref, k_ref, v_ref, la_ref, sg_ref, g_ref, s0_ref, o_ref, sfin_ref, st_ref, *, chunk, nsub, nseq):
    t = pl.program_id(1)

    @pl.when(t == 0)
    def _():
        st_ref[...] = s0_ref[...]

    C = chunk
    tl = C * nsub
    r_i = lax.broadcasted_iota(jnp.int32, (C, C), 0)
    c_i = lax.broadcasted_iota(jnp.int32, (C, C), 1)
    causal = c_i <= r_i
    grp = min(tl, MXU_DIM)
    rt = lax.broadcasted_iota(jnp.int32, (grp, grp), 0)
    ct = lax.broadcasted_iota(jnp.int32, (grp, grp), 1)
    chunk_tril = ((ct <= rt) & (ct >= (rt // C) * C)).astype(bf16)
    lane = lax.broadcasted_iota(jnp.int32, (1, LANES), 1)
    first_half = lane < GLA_DK
    mid = C // 2
    pairs = range(GLA_HEADS // 2)
    seqs = range(nseq)
    chunks = [(s, c) for s in seqs for c in range(nsub)]

    def lanes(p):
        return slice(p * LANES, (p + 1) * LANES)

    def one_head(x, hh):
        return jnp.where(first_half if hh == 0 else jnp.logical_not(first_half), x, jnp.zeros_like(x))

    bcum_all = []
    for s in seqs:
        la = la_ref[s]
        la_hi = la.astype(bf16)
        la_lo = (la - la_hi.astype(f32)).astype(bf16)
        bcum_all.append(jnp.concatenate(
            [_dot(chunk_tril, la_hi[r:r + grp]) + _dot(chunk_tril, la_lo[r:r + grp]) for r in range(0, tl, grp)],
            axis=0))

    q_in, dec, amat, upd = {}, {}, {}, {}
    for s, c in chunks:
        rows = slice(c * C, (c + 1) * C)
        bcum = bcum_all[s][rows]
        b_last = bcum[C - 1:C, :]
        ref = bcum[mid:mid + 1, :]
        q = q_ref[s, rows, :].astype(f32)
        k = k_ref[s, rows, :].astype(f32)
        q_in[s, c] = (q * jnp.exp(bcum)).astype(bf16)
        q_a = (q * jnp.exp(jnp.clip(bcum - ref, -80.0, 80.0))).astype(bf16)
        k_a = (k * jnp.exp(jnp.clip(ref - bcum, -80.0, 80.0))).astype(bf16)
        k_d = (k * jnp.exp(b_last - bcum)).astype(bf16)
        dec[s, c] = jnp.exp(b_last)
        for p in pairs:
            for hh in range(2):
                h = 2 * p + hh
                v_h = v_ref[s, rows, h * GLA_DV:(h + 1) * GLA_DV]
                a = _dot_nt(one_head(q_a[:, lanes(p)], hh), k_a[:, lanes(p)])
                amat[s, c, h] = jnp.where(causal, a, 0.0).astype(bf16)
                upd[s, c, h] = _dot_tn(v_h, k_d[:, lanes(p)])

    states = {}
    for s in seqs:
        st = [st_ref[s, p] for p in pairs]
        for c in range(nsub):
            states[s, c] = [x.astype(bf16) for x in st]
            st = [dec[s, c][:, lanes(p)] * st[p] + jnp.where(first_half, upd[s, c, 2 * p], upd[s, c, 2 * p + 1])
                  for p in pairs]
        for p in pairs:
            st_ref[s, p] = st[p]

    for s, c in chunks:
        rows = slice(c * C, (c + 1) * C)
        for h in range(GLA_HEADS):
            p, hh = divmod(h, 2)
            vs = slice(h * GLA_DV, (h + 1) * GLA_DV)
            o = (_dot(amat[s, c, h], v_ref[s, rows, vs])
                 + _dot_nt(one_head(q_in[s, c][:, lanes(p)], hh), states[s, c][p]))
            o = o * lax.rsqrt(jnp.mean(o * o, axis=-1, keepdims=True) + RMS_EPS)
            o = o * g_ref[:, vs] * sg_ref[s, rows, vs].astype(f32)
            o_ref[s, rows, vs] = o.astype(bf16)

    @pl.when(t == pl.num_programs(1) - 1)
    def _():
        sfin_ref[...] = st_ref[...]


def _gla(gq, gk, gv, la, sg, g, s0t):
    B, L, _ = gq.shape
    C = min(GLA_CHUNK, L)
    tl = min(512, L)
    nseq = next(d for d in (4, 2, 1) if B % d == 0 and d * tl <= 1024)
    assert L % tl == 0 and tl % C == 0
    tok = lambda w: pl.BlockSpec((nseq, tl, w), lambda b, t: (b, t, 0))
    st_spec = pl.BlockSpec((nseq, 2, GLA_DV, LANES), lambda b, t: (b, 0, 0, 0))
    return pl.pallas_call(
        functools.partial(_gla_kernel, chunk=C, nsub=tl // C, nseq=nseq),
        grid=(B // nseq, L // tl),
        in_specs=[tok(GLA_QK), tok(GLA_QK), tok(GLA_V), tok(GLA_QK), tok(GLA_V),
                  pl.BlockSpec((1, GLA_V), lambda b, t: (0, 0)), st_spec],
        out_specs=[tok(GLA_V), st_spec],
        out_shape=(jax.ShapeDtypeStruct((B, L, GLA_V), bf16),
                   jax.ShapeDtypeStruct((B, 2, GLA_DV, LANES), f32)),
        scratch_shapes=[pltpu.VMEM((nseq, 2, GLA_DV, LANES), f32)],
        compiler_params=pltpu.CompilerParams(dimension_semantics=("arbitrary", "arbitrary"),
                                             vmem_limit_bytes=VMEM_LIMIT),
        name="gla_scan",
    )(gq, gk, gv, la, sg, g, s0t)


def _state_to_pairs(s):
    B = s.shape[0]
    s = s.reshape(B, 2, 2, GLA_DK, GLA_DV)
    return jnp.transpose(s, (0, 1, 4, 2, 3)).reshape(B, 2, GLA_DV, 2 * GLA_DK)


def _pairs_to_state(st):
    B = st.shape[0]
    st = st.reshape(B, 2, GLA_DV, 2, GLA_DK)
    return jnp.transpose(st, (0, 1, 3, 4, 2)).reshape(B, GLA_HEADS, GLA_DK, GLA_DV)


def _meta_kernel(lf_ref, nk_ref, qm_ref, ct_ref, start_ref, *, nc):
    H = FOX_HEADS
    r_i = lax.broadcasted_iota(jnp.int32, (LANES, LANES), 0)
    c_i = lax.broadcasted_iota(jnp.int32, (LANES, LANES), 1)
    upper = (r_i <= c_i).astype(f32)
    rj = lax.broadcasted_iota(jnp.int32, (nc, nc), 0)
    cj = lax.broadcasted_iota(jnp.int32, (nc, nc), 1)
    strict_lower = (cj < rj).astype(f32)
    eye = rj == cj

    def to_row(col):
        return jnp.sum(jnp.where(eye, jnp.broadcast_to(col, (nc, nc)), 0.0), axis=0, keepdims=True)

    ys = [_dot_hi(lf_ref[0, h], upper) for h in range(H)]
    prev = [_dot_hi(strict_lower, jnp.broadcast_to(y[:, LANES - 1:LANES], (nc, LANES))) for y in ys]
    start = None
    for h in range(H):
        c = (ys[h] + prev[h]) * LOG2E
        ct_ref[0, h] = c
        cmax = jnp.max(c, axis=1, keepdims=True)
        cmin = jnp.min(c, axis=1, keepdims=True)
        qmax = jnp.sqrt(qm_ref[0, :, H + h:H + h + 1])
        kmax = jnp.sqrt(jnp.max(jnp.max(nk_ref[0, h], axis=1, keepdims=True), axis=0, keepdims=True))
        thr = cmax + (2.0 * NORM_SLACK) * qmax * kmax + SKIP_T2
        needed = (to_row(cmin) <= thr) & (cj <= rj)
        first = jnp.min(jnp.where(needed, cj, nc), axis=1, keepdims=True)
        start = first if start is None else jnp.minimum(start, first)
    start_ref[0] = to_row(start.astype(f32)).astype(jnp.int32)


def _fox_meta(lf_t, nk2_t, qmx):
    B, _, nc, _ = lf_t.shape
    per_head = pl.BlockSpec((1, FOX_HEADS, nc, LANES), lambda b: (b, 0, 0, 0))
    return pl.pallas_call(
        functools.partial(_meta_kernel, nc=nc),
        grid=(B,),
        in_specs=[per_head, per_head, pl.BlockSpec((1, nc, LANES), lambda b: (b, 0, 0))],
        out_specs=[pl.BlockSpec((1, FOX_HEADS, nc, LANES), lambda b: (b, 0, 0, 0)),
                   pl.BlockSpec((1, 1, nc), lambda b: (b, 0, 0))],
        out_shape=(jax.ShapeDtypeStruct((B, FOX_HEADS, nc, LANES), f32),
                   jax.ShapeDtypeStruct((B, 1, nc), jnp.int32)),
        compiler_params=pltpu.CompilerParams(dimension_semantics=("arbitrary",), vmem_limit_bytes=VMEM_LIMIT),
        name="fox_meta",
    )(lf_t, nk2_t, qmx)


def _cumsum_kernel(lf_ref, ct_ref, *, nb, nc):
    r_i = lax.broadcasted_iota(jnp.int32, (LANES, LANES), 0)
    c_i = lax.broadcasted_iota(jnp.int32, (LANES, LANES), 1)
    upper = (r_i <= c_i).astype(f32)
    rj = lax.broadcasted_iota(jnp.int32, (nc, nc), 0)
    cj = lax.broadcasted_iota(jnp.int32, (nc, nc), 1)
    strict_lower = (cj < rj).astype(f32)
    seqs = [(s, h) for s in range(nb) for h in range(FOX_HEADS)]
    ys = [_dot_hi(lf_ref[s, h], upper) for s, h in seqs]
    prev = [_dot_hi(strict_lower, jnp.broadcast_to(y[:, LANES - 1:LANES], (nc, LANES))) for y in ys]
    for (s, h), y, p in zip(seqs, ys, prev):
        ct_ref[s, h] = (y + p) * LOG2E


def _fox_cumsum(lf_t):
    B, _, nc, _ = lf_t.shape
    nb = next(d for d in (4, 2, 1) if B % d == 0)
    spec = pl.BlockSpec((nb, FOX_HEADS, nc, LANES), lambda b: (b, 0, 0, 0))
    return pl.pallas_call(
        functools.partial(_cumsum_kernel, nb=nb, nc=nc),
        grid=(B // nb,),
        in_specs=[spec],
        out_specs=spec,
        out_shape=jax.ShapeDtypeStruct(lf_t.shape, f32),
        compiler_params=pltpu.CompilerParams(dimension_semantics=("arbitrary",), vmem_limit_bytes=VMEM_LIMIT),
        name="fox_cumsum",
    )(lf_t)


def _head_rows(h):
    return slice(h * FOX_DH, (h + 1) * FOX_DH)


def _pair_slab(h, width=LANES):
    return slice((h // 2) * width, (h // 2 + 1) * width)


def _v_aug(vt_h, h):
    one = jnp.ones_like(vt_h)
    return jnp.concatenate([vt_h, one] if h % 2 == 0 else [one, vt_h], axis=0)


def _fox_prompt_kernel(start_ref, q_ref, kp_ref, kc_ref, vp_ref, vc_ref,
                       ct_ref, kh_ref, vh_ref, o_ref, m_s, acc_s, kbuf, vbuf, sem, *, parts):
    T = ATT_BLOCK
    TQ = 2 * T
    H = FOX_HEADS
    b = pl.program_id(0)
    i = pl.program_id(1)
    r_i = lax.broadcasted_iota(jnp.int32, (TQ, 2 * TQ), 0)
    c_i = lax.broadcasted_iota(jnp.int32, (TQ, 2 * TQ), 1)
    visible = c_i <= r_i + TQ
    lane = lax.broadcasted_iota(jnp.int32, (1, LANES), 1)
    first_half = lane < FOX_DH
    halves = range(parts)
    bps = 2 * parts

    def q_rows(u):
        return slice(u * TQ, (u + 1) * TQ)

    def first_block(u):
        return bps * i + 2 * u - 2

    def head_q(u, h):
        hm = first_half if h % 2 == 0 else jnp.logical_not(first_half)
        qp = q_ref[0, q_rows(u), _pair_slab(h)]
        return jnp.where(hm, qp, jnp.zeros_like(qp))

    def head_base(u, h):
        return ct_ref[0, h, pl.ds(bps * i + 2 * u, 1), 0:1]

    def c_row(h, j):
        return ct_ref[0, h, pl.ds(j, 1), :]

    def window(u, prev_ref, cur_ref, rows):
        if u == 0:
            return jnp.concatenate([prev_ref[0, rows, :], cur_ref[0, rows, 0:TQ]], axis=1)
        return cur_ref[0, rows, (u - 1) * TQ:(u + 1) * TQ]

    scores = {}
    for u in halves:
        j0 = first_block(u)
        jc = jnp.maximum(j0, 0)
        for h in range(H):
            base = head_base(u, h)
            ck = jnp.concatenate([jnp.where(j0 >= 0, c_row(h, jc) - base, -NEG_BIG),
                                  jnp.where(j0 >= 0, c_row(h, jc + 1) - base, -NEG_BIG),
                                  c_row(h, j0 + 2) - base, c_row(h, j0 + 3) - base], axis=1)
            s = jnp.where(visible, _dot(head_q(u, h), window(u, kp_ref, kc_ref, _pair_slab(h))) - ck, NEG_BIG)
            m = jnp.max(s, axis=1, keepdims=True)
            m_s[h, q_rows(u), :] = m
            scores[u, h] = (s, m)
    for u in halves:
        for h in range(H):
            s, m = scores[u, h]
            acc_s[h, q_rows(u), :] = _dot_nt(jnp.exp2(s - m).astype(bf16),
                                            _v_aug(window(u, vp_ref, vc_ref, _head_rows(h)), h))

    def copies(j, slot):
        cols = pl.ds(pl.multiple_of(j * T, T), T)
        return (pltpu.make_async_copy(kh_ref.at[b, :, cols], kbuf.at[slot], sem.at[0, slot]),
                pltpu.make_async_copy(vh_ref.at[b, :, cols], vbuf.at[slot], sem.at[1, slot]))

    for u in halves:
        start = jnp.minimum(start_ref[b, bps * i + 2 * u], start_ref[b, bps * i + 2 * u + 1])
        n_far = jnp.maximum(first_block(u) - start, 0)

        @pl.when(n_far > 0)
        def _(u=u, start=start, n_far=n_far):
            for cp in copies(start, 0):
                cp.start()

            def body(t, carry):
                j = start + t
                slot = lax.rem(t, 2)
                for cp in copies(j, slot):
                    cp.wait()

                @pl.when(t + 1 < n_far)
                def _():
                    for cp in copies(j + 1, 1 - slot):
                        cp.start()

                for h in range(H):
                    ck = c_row(h, j) - head_base(u, h)
                    s = _dot(head_q(u, h), kbuf[slot, _pair_slab(h), :]) - ck
                    m_prev = m_s[h, q_rows(u), :]
                    m_new = jnp.maximum(m_prev, jnp.max(s, axis=1, keepdims=True))
                    alpha = jnp.exp2(m_prev - m_new)
                    p = jnp.exp2(s - m_new)
                    acc_s[h, q_rows(u), :] = (alpha * acc_s[h, q_rows(u), :]
                                             + _dot_nt(p.astype(bf16), _v_aug(vbuf[slot, _head_rows(h), :], h)))
                    m_s[h, q_rows(u), :] = m_new
                return carry

            lax.fori_loop(0, n_far, body, 0)

    for p in range(H // 2):
        even = acc_s[2 * p]
        odd = acc_s[2 * p + 1]
        numer = jnp.where(first_half, even, odd)
        denom = pltpu.roll(jnp.where(first_half, odd, even), FOX_DH, 1)
        o_ref[0, :, p * LANES:(p + 1) * LANES] = (numer / denom).astype(bf16)


def _fox_prompt(fq, fkbt, fvbt, ct, start):
    B, L, _ = fq.shape
    T = ATT_BLOCK
    TQ = 2 * T
    parts = next(d for d in (FOX_PARTS, 2, 1) if L % (d * TQ) == 0)
    TS = parts * TQ
    prev = pl.BlockSpec((1, FOX_W, TQ), lambda b, i, s: (b, 0, jnp.maximum(parts * i - 1, 0)))
    cur = pl.BlockSpec((1, FOX_W, TS), lambda b, i, s: (b, 0, i))
    grid_spec = pltpu.PrefetchScalarGridSpec(
        num_scalar_prefetch=1,
        grid=(B, L // TS),
        in_specs=[pl.BlockSpec((1, TS, FOX_W), lambda b, i, s: (b, i, 0)),
                  prev, cur, prev, cur,
                  pl.BlockSpec((1, FOX_HEADS, L // T, LANES), lambda b, i, s: (b, 0, 0, 0)),
                  pl.BlockSpec(memory_space=pl.ANY), pl.BlockSpec(memory_space=pl.ANY)],
        out_specs=pl.BlockSpec((1, TS, FOX_W), lambda b, i, s: (b, i, 0)),
        scratch_shapes=[pltpu.VMEM((FOX_HEADS, TS, 1), f32),
                        pltpu.VMEM((FOX_HEADS, TS, LANES), f32),
                        pltpu.VMEM((2, FOX_W, T), bf16), pltpu.VMEM((2, FOX_W, T), bf16),
                        pltpu.SemaphoreType.DMA((2, 2))],
    )
    return pl.pallas_call(
        functools.partial(_fox_prompt_kernel, parts=parts),
        grid_spec=grid_spec,
        out_shape=jax.ShapeDtypeStruct((B, L, FOX_W), bf16),
        compiler_params=pltpu.CompilerParams(dimension_semantics=("arbitrary", "arbitrary"),
                                             vmem_limit_bytes=VMEM_LIMIT),
        name="fox_prompt",
    )(start, fq, fkbt, fkbt, fvbt, fvbt, ct, fkbt, fvbt)


def _fox_sample_kernel(q_ref, kn_ref, vn_ref, kp_ref, vp_ref, ct_ref, o_ref, *, past, lq):
    H = FOX_HEADS
    r_i = lax.broadcasted_iota(jnp.int32, (lq, lq), 0)
    c_i = lax.broadcasted_iota(jnp.int32, (lq, lq), 1)
    causal = c_i <= r_i
    lane = lax.broadcasted_iota(jnp.int32, (1, LANES), 1)
    first_half = lane < FOX_DH
    probs = []
    for h in range(H):
        hm = first_half if h % 2 == 0 else jnp.logical_not(first_half)
        qp = q_ref[0, :, _pair_slab(h)]
        qm = jnp.where(hm, qp, jnp.zeros_like(qp))
        base = ct_ref[0, h:h + 1, past:past + 1]
        s_p = _dot(qm, kp_ref[0, _pair_slab(h), :].astype(bf16)) - (ct_ref[0, h:h + 1, 0:past] - base)
        s_n = _dot(qm, kn_ref[0, _pair_slab(h), :]) - (ct_ref[0, h:h + 1, past:past + lq] - base)
        s_n = jnp.where(causal, s_n, NEG_BIG)
        m = jnp.maximum(jnp.max(s_p, axis=1, keepdims=True), jnp.max(s_n, axis=1, keepdims=True))
        probs.append((jnp.exp2(s_p - m).astype(bf16), jnp.exp2(s_n - m).astype(bf16)))
    accs = []
    for h in range(H):
        p_p, p_n = probs[h]
        accs.append(_dot_nt(p_p, _v_aug(vp_ref[0, _head_rows(h), :].astype(bf16), h))
                    + _dot_nt(p_n, _v_aug(vn_ref[0, _head_rows(h), :], h)))
    for p in range(H // 2):
        numer = jnp.where(first_half, accs[2 * p], accs[2 * p + 1])
        denom = pltpu.roll(jnp.where(first_half, accs[2 * p + 1], accs[2 * p]), FOX_DH, 1)
        o_ref[0, :, p * LANES:(p + 1) * LANES] = (numer / denom).astype(bf16)


def _fox_sample(fq, fkbt, fvbt, kt_past, vt_past, ct_flat):
    B, lq, _ = fq.shape
    past = kt_past.shape[2]
    lpad = ct_flat.shape[2]
    new = pl.BlockSpec((1, FOX_W, lq), lambda b: (b, 0, 0))
    old = pl.BlockSpec((1, FOX_W, past), lambda b: (b, 0, 0))
    qo = pl.BlockSpec((1, lq, FOX_W), lambda b: (b, 0, 0))
    return pl.pallas_call(
        functools.partial(_fox_sample_kernel, past=past, lq=lq),
        grid=(B,),
        in_specs=[qo, new, new, old, old, pl.BlockSpec((1, FOX_HEADS, lpad), lambda b: (b, 0, 0))],
        out_specs=qo,
        out_shape=jax.ShapeDtypeStruct((B, lq, FOX_W), bf16),
        compiler_params=pltpu.CompilerParams(dimension_semantics=("arbitrary",), vmem_limit_bytes=VMEM_LIMIT),
        name="fox_sample",
    )(fq, fkbt, fvbt, kt_past, vt_past, ct_flat)


def _layer_norm(t, g, b):
    mu = jnp.mean(t, axis=-1, keepdims=True)
    d = t - mu
    var = jnp.mean(d * d, axis=-1, keepdims=True)
    return d * lax.rsqrt(var + LN_EPS) * g + b


def _mix_kernel(x_ref, og_ref, of_ref, wz_ref, bz_ref, wpg_ref, wpf_ref, wo_ref, g1_ref, b1_ref, x1_ref, *, alpha):
    groups = _row_groups(x_ref.shape[0])
    merged = []
    for rows in groups:
        xb = x_ref[rows, :].astype(bf16)
        zg = _dot(xb, wz_ref[:, :D_MODEL]) + bz_ref[:, :D_MODEL]
        zf = _dot(xb, wz_ref[:, D_MODEL:]) + bz_ref[:, D_MODEL:]
        m = (jax.nn.sigmoid(zg) * _dot(og_ref[rows, :], wpg_ref[...])
             + jax.nn.sigmoid(zf) * _dot(of_ref[rows, :], wpf_ref[...]))
        merged.append(m.astype(bf16))
    for rows, m in zip(groups, merged):
        t = alpha * x_ref[rows, :] + _dot(m, wo_ref[...])
        x1_ref[rows, :] = _layer_norm(t, g1_ref[...], b1_ref[...])


def _row_groups(tm):
    rows = MXU_DIM if tm % MXU_DIM == 0 else tm
    return tuple(slice(r, r + rows) for r in range(0, tm, rows))


def _ffn_kernel(x1_ref, wg_ref, wu_ref, wd_ref, g2_ref, b2_ref, y_ref, *, alpha, bounds):
    chunks = [slice(lo, hi) for lo, hi in zip(bounds[:-1], bounds[1:])]
    stages = [(rows, cs) for rows in _row_groups(x1_ref.shape[0]) for cs in chunks]
    acc = {}
    pending = None

    def down(rows, cs, hid):
        part = _dot(hid, wd_ref[cs, :])
        key = (rows.start, rows.stop)
        acc[key] = part if key not in acc else acc[key] + part
        if cs is chunks[-1]:
            x1 = x1_ref[rows, :]
            y_ref[rows, :] = _layer_norm(alpha * x1 + acc[key], g2_ref[...], b2_ref[...])

    for rows, cs in stages:
        xb = x1_ref[rows, :].astype(bf16)
        g = _dot(xb, wg_ref[:, cs])
        u = _dot(xb, wu_ref[:, cs])
        hid = (g * jax.nn.sigmoid(g) * u).astype(bf16)
        if pending is not None:
            down(*pending)
        pending = (rows, cs, hid)
    down(*pending)


def _mix(x2d, og, of, wz, bz, wpg, wpf, wo, g1, b1, alpha):
    n = x2d.shape[0]
    tm = min(TOKEN_TILE, n)
    row = lambda w: pl.BlockSpec((tm, w), lambda i: (i, 0))
    return pl.pallas_call(
        functools.partial(_mix_kernel, alpha=alpha),
        grid=(n // tm,),
        in_specs=[row(D_MODEL), row(GLA_V), row(FOX_W)] + [_const_spec(a.shape) for a in (wz, bz, wpg, wpf, wo, g1, b1)],
        out_specs=row(D_MODEL),
        out_shape=jax.ShapeDtypeStruct((n, D_MODEL), f32),
        compiler_params=pltpu.CompilerParams(dimension_semantics=("arbitrary",), vmem_limit_bytes=VMEM_LIMIT),
        name="mix",
    )(x2d, og, of, wz, bz, wpg, wpf, wo, g1, b1)


def _ffn(x1, wg, wu, wd, g2, b2, alpha):
    n = x1.shape[0]
    tm = min(TOKEN_TILE, n)
    row = lambda w: pl.BlockSpec((tm, w), lambda i: (i, 0))
    return pl.pallas_call(
        functools.partial(_ffn_kernel, alpha=alpha, bounds=FFN_BOUNDS),
        grid=(n // tm,),
        in_specs=[row(D_MODEL)] + [_const_spec(a.shape) for a in (wg, wu, wd, g2, b2)],
        out_specs=row(D_MODEL),
        out_shape=jax.ShapeDtypeStruct((n, D_MODEL), f32),
        compiler_params=pltpu.CompilerParams(dimension_semantics=("arbitrary",), vmem_limit_bytes=VMEM_LIMIT),
        name="ffn",
    )(x1, wg, wu, wd, g2, b2)


def _prep_weights(w_in, b_in, w_alpha2, b_alpha2, gla_norm_g, w_proj_gla, w_proj_fox, w_out,
                  ln1_g, ln1_b, w_ffn_gate, w_ffn_up, w_ffn_down, ln2_g, ln2_b):
    def cols(a, idx):
        return a[..., _OFF[idx]:_OFF[idx + 1]]

    order = (0, 1, 2, 4, 5, 3)
    pad = SMALL_W - GLA_RANK
    w1 = jnp.concatenate([cols(w_in, i).astype(bf16) for i in order] + [jnp.zeros((D_MODEL, pad), bf16)], axis=1)
    b1 = jnp.concatenate([cols(b_in, i) for i in order] + [jnp.zeros((pad,), f32)])[None, :]
    t_cols = (slice(_OFF[6], _OFF[9]),)
    ff_pad = FF_ROWS - FOX_HEADS
    wkv = jnp.pad(jnp.transpose(w_in[:, t_cols[0]].astype(bf16)), ((0, ff_pad), (0, 0)))
    bkv = jnp.broadcast_to(jnp.pad(b_in[t_cols[0]], (0, ff_pad))[:, None], (2 * FOX_W + FF_ROWS, LANES)).astype(f32)
    wa = jnp.zeros((SMALL_W, GLA_QK), f32).at[:GLA_RANK].set(w_alpha2).astype(bf16)
    head_of = np.arange(FOX_W) // FOX_DH
    indq = np.zeros((FOX_W, SMALL_W), np.float32)
    indq[np.arange(FOX_W), FOX_HEADS + head_of] = 1.0
    row = lambda a: a[None, :].astype(f32)
    return dict(
        w1=w1, b1=b1, wkv=wkv, bkv=bkv, wa=wa, ba=row(b_alpha2), indq=jnp.asarray(indq, bf16),
        g=row(gla_norm_g),
        wz=w_in[:, _OFF[9]:].astype(bf16), bz=row(b_in[_OFF[9]:]),
        wpg=w_proj_gla.astype(bf16), wpf=w_proj_fox.astype(bf16), wo=w_out.astype(bf16),
        g1=row(ln1_g), b1n=row(ln1_b),
        wg=w_ffn_gate.astype(bf16), wu=w_ffn_up.astype(bf16), wd=w_ffn_down.astype(bf16),
        g2=row(ln2_g), b2n=row(ln2_b),
    )


def _heads_last(t, B, L):
    return jnp.transpose(t.reshape(B, FOX_HEADS, FOX_DH, L), (0, 3, 1, 2))


def _layer(x, s0, k_past, v_past, lf_past, p, alpha):
    B, L, _ = x.shape
    n = B * L
    x2d = x.reshape(n, D_MODEL)
    prompt = k_past is None
    lo = L if prompt else n
    gq, gk, gv, la, sg, fq, fkt, fvt, fkbt, fvbt, lft, nk2t, qmx = _inproj(
        x2d, lo, p["w1"], p["b1"], p["wkv"], p["bkv"], p["wa"], p["ba"], p["indq"])
    r3 = lambda a: a.reshape(B, L, a.shape[-1])

    o_gla, st = _gla(r3(gq), r3(gk), r3(gv), r3(la), r3(sg), p["g"], _state_to_pairs(s0))
    gla_state = _pairs_to_state(st)

    if prompt:
        nc = L // LANES
        chunks = lambda t: t.reshape(B, FOX_HEADS, nc, LANES)
        ct, start = _fox_meta(chunks(lft), chunks(nk2t), qmx.reshape(B, nc, SMALL_W))
        o_fox = _fox_prompt(r3(fq), fkbt, fvbt, ct, start.reshape(B, nc))
        fk, fv = _heads_last(fkt, B, L), _heads_last(fvt, B, L)
        lf = jnp.transpose(lft, (0, 2, 1))
    else:
        past = k_past.shape[1]
        lk = past + L
        lpad = -(-lk // (8 * LANES)) * (8 * LANES)
        nc = lpad // LANES
        per_stream = lambda t: jnp.transpose(t.reshape(t.shape[1], B, L), (1, 0, 2))
        lft_s = per_stream(lft)
        lf_all_t = jnp.concatenate([jnp.transpose(lf_past.astype(f32), (0, 2, 1)), lft_s], axis=2)
        lf_all_t = jnp.pad(lf_all_t, ((0, 0), (0, 0), (0, lpad - lk))).reshape(B, FOX_HEADS, nc, LANES)
        ct = _fox_cumsum(lf_all_t)
        cache_t = lambda c: jnp.transpose(c, (0, 2, 3, 1)).reshape(B, FOX_W, past)
        o_fox = _fox_sample(r3(fq), per_stream(fkbt), per_stream(fvbt), cache_t(k_past), cache_t(v_past),
                            ct.reshape(B, FOX_HEADS, lpad))
        fk, fv = _heads_last(per_stream(fkt), B, L), _heads_last(per_stream(fvt), B, L)
        lf = jnp.transpose(lft_s, (0, 2, 1))

    x1 = _mix(x2d, o_gla.reshape(n, GLA_V), o_fox.reshape(n, FOX_W), p["wz"], p["bz"], p["wpg"], p["wpf"], p["wo"],
              p["g1"], p["b1n"], alpha)
    y = _ffn(x1, p["wg"], p["wu"], p["wd"], p["g2"], p["b2n"], alpha)
    return y.reshape(B, L, D_MODEL), gla_state, fk, fv, lf


def kernel(x_prompt, x_sample, state_gla, cache_fox_k, cache_fox_v, cache_fox_logf, w_in, b_in, w_alpha2, b_alpha2,
           gla_norm_g, w_proj_gla, w_proj_fox, w_out, ln1_g, ln1_b, w_ffn_gate, w_ffn_up, w_ffn_down, ln2_g, ln2_b):
    depth = w_in.shape[0]
    alpha = (2.0 * depth) ** 0.25
    yp, ys = x_prompt, x_sample
    outs_p, outs_s = [], []
    for l in range(depth):
        p = _prep_weights(w_in[l], b_in[l], w_alpha2[l], b_alpha2[l], gla_norm_g[l], w_proj_gla[l], w_proj_fox[l],
                          w_out[l], ln1_g[l], ln1_b[l], w_ffn_gate[l], w_ffn_up[l], w_ffn_down[l], ln2_g[l], ln2_b[l])
        s0 = jnp.zeros((yp.shape[0], GLA_HEADS, GLA_DK, GLA_DV), f32)
        yp, *rest_p = _layer(yp, s0, None, None, None, p, alpha)
        outs_p.append(rest_p)
        ys, *rest_s = _layer(ys, state_gla[l], cache_fox_k[l], cache_fox_v[l], cache_fox_logf[l], p, alpha)
        outs_s.append(rest_s)
    stack = lambda outs, i: jnp.stack([o[i] for o in outs])
    return (yp, ys,
            stack(outs_p, 0), stack(outs_p, 1), stack(outs_p, 2), stack(outs_p, 3),
            stack(outs_s, 0), stack(outs_s, 1), stack(outs_s, 2), stack(outs_s, 3))
```

```python
import functools

import jax
import jax.numpy as jnp
import numpy as np
from jax import lax
from jax.experimental import pallas as pl
from jax.experimental.pallas import tpu as pltpu

f32 = jnp.float32
bf16 = jnp.bfloat16

D_MODEL = 1024
GLA_HEADS = 4
GLA_DK = 64
GLA_DV = 128
GLA_RANK = 16
GLA_TAU = 16.0
GLA_QK = GLA_HEADS * GLA_DK
GLA_V = GLA_HEADS * GLA_DV
GLA_CHUNK = 64
FOX_HEADS = 8
FOX_DH = 64
FOX_W = FOX_HEADS * FOX_DH
D_FF = 2816
LN_EPS = 1e-5
RMS_EPS = 1e-5
SPLIT_SIZES = (GLA_QK, GLA_QK, GLA_V, GLA_RANK, GLA_V, FOX_W, FOX_W, FOX_W, FOX_HEADS, D_MODEL, D_MODEL)
_OFF = np.concatenate([[0], np.cumsum(SPLIT_SIZES)]).astype(int)

LANES = 128
MXU_DIM = 256
FFN_BOUNDS = (0, 4 * MXU_DIM, 8 * MXU_DIM, D_FF)
TOKEN_TILE = 1024
SMALL_W = LANES
FF_ROWS = 16
ATT_BLOCK = 128
FOX_PARTS = 2
LOG2E = 1.4426950408889634
SKIP_T2 = 110.0 * LOG2E
NORM_SLACK = 1.02
NEG_BIG = -1e30
GLA_EXP2_CAP = 80.0 * LOG2E
VMEM_LIMIT = 56 * 1024 * 1024


def _dot(a, b):
    return jnp.dot(a, b, preferred_element_type=f32)


def _dot_nt(a, b):
    return lax.dot_general(a, b, (((1,), (1,)), ((), ())), preferred_element_type=f32)


def _dot_tn(a, b):
    return lax.dot_general(a, b, (((0,), (0,)), ((), ())), preferred_element_type=f32)


def _dot_hi(a, b):
    return jnp.dot(a, b, preferred_element_type=f32, precision=lax.Precision.HIGHEST)


def _log_sigmoid(x):
    return jnp.minimum(x, 0.0) - jnp.log1p(jnp.exp(-jnp.abs(x)))


def _const_spec(shape):
    return pl.BlockSpec(shape, lambda *_: (0,) * len(shape), pipeline_mode=pl.Buffered(1))


def _inproj_kernel(x_ref, w_ref, b_ref, wkv_ref, bkv_ref, wa_ref, ba_ref, indq_ref,
                   gq_ref, gk_ref, gv_ref, la_ref, sg_ref, fq_ref, fkt_ref, fvt_ref,
                   fkbt_ref, fvbt_ref, lft_ref, nk2t_ref, qmx_ref):
    xb = x_ref[...].astype(bf16)
    tm = x_ref.shape[0]

    def group(lo, width):
        return _dot(xb, w_ref[:, lo:lo + width]) + b_ref[:, lo:lo + width]

    o_gq, o_gk, o_gv = 0, GLA_QK, 2 * GLA_QK
    o_gr = o_gv + GLA_V
    o_fq = o_gr + GLA_V
    o_small = o_fq + FOX_W
    small_b = group(o_small, SMALL_W).astype(bf16)
    fq = (group(o_fq, FOX_W) * (FOX_DH ** -0.5 * LOG2E)).astype(bf16)
    fq_ref[...] = fq
    fq32 = fq.astype(f32)
    fq2 = (fq32 * fq32).astype(bf16)
    gq_ref[...] = (group(o_gq, GLA_QK) * (GLA_DK ** -0.5)).astype(bf16)
    gk_ref[...] = group(o_gk, GLA_QK).astype(bf16)
    gv_ref[...] = group(o_gv, GLA_V).astype(bf16)
    la_pre = _dot(small_b, wa_ref[...]) + ba_ref[...]
    la_ref[...] = _log_sigmoid(la_pre) * (LOG2E / GLA_TAU)
    nq2 = _dot(fq2, indq_ref[...])
    qmx_ref[...] = jnp.max(nq2.reshape(tm // ATT_BLOCK, ATT_BLOCK, SMALL_W), axis=1)
    gr = group(o_gr, GLA_V)
    sg_ref[...] = (gr * jax.nn.sigmoid(gr)).astype(bf16)

    bias_t = jnp.concatenate([bkv_ref[...]] * (tm // LANES), axis=1)
    k_t = _dot_nt(wkv_ref[:FOX_W, :], xb) + bias_t[:FOX_W]
    fkt_ref[0] = k_t
    kb = k_t.astype(bf16)
    fkbt_ref[0] = kb
    k32 = kb.astype(f32)
    k2 = k32 * k32
    nk2t_ref[0] = jnp.concatenate(
        [jnp.sum(k2[h * FOX_DH:(h + 1) * FOX_DH], axis=0, keepdims=True) for h in range(FOX_HEADS)], axis=0)
    vf_t = _dot_nt(wkv_ref[FOX_W:, :], xb) + bias_t[FOX_W:]
    fvt_ref[0] = vf_t[:FOX_W]
    fvbt_ref[0] = vf_t[:FOX_W].astype(bf16)
    lft_ref[0] = _log_sigmoid(vf_t[FOX_W:FOX_W + FOX_HEADS])


def _inproj(x2d, lo, w1, b1, wkv, bkv, wa, ba, indq):
    n = x2d.shape[0]
    bo = n // lo
    tm = min(TOKEN_TILE, lo)
    assert lo % tm == 0 and tm % LANES == 0
    tpb = lo // tm
    row = lambda w: pl.BlockSpec((tm, w), lambda i: (i, 0))
    col = lambda w: pl.BlockSpec((1, w, tm), lambda i: (i // tpb, 0, i % tpb))
    rows_out = lambda w, dt: (jax.ShapeDtypeStruct((n, w), dt), row(w))
    cols_out = lambda w, dt: (jax.ShapeDtypeStruct((bo, w, lo), dt), col(w))
    outs = (
        rows_out(GLA_QK, bf16),
        rows_out(GLA_QK, bf16),
        rows_out(GLA_V, bf16),
        rows_out(GLA_QK, f32),
        rows_out(GLA_V, bf16),
        rows_out(FOX_W, bf16),
        cols_out(FOX_W, f32),
        cols_out(FOX_W, f32),
        cols_out(FOX_W, bf16),
        cols_out(FOX_W, bf16),
        cols_out(FOX_HEADS, f32),
        cols_out(FOX_HEADS, f32),
        (jax.ShapeDtypeStruct((n // ATT_BLOCK, SMALL_W), f32),
         pl.BlockSpec((tm // ATT_BLOCK, SMALL_W), lambda i: (i, 0))),
    )
    consts = (w1, b1, wkv, bkv, wa, ba, indq)
    return pl.pallas_call(
        _inproj_kernel,
        grid=(n // tm,),
        in_specs=[row(D_MODEL)] + [_const_spec(a.shape) for a in consts],
        out_specs=[o[1] for o in outs],
        out_shape=[o[0] for o in outs],
        compiler_params=pltpu.CompilerParams(dimension_semantics=("arbitrary",), vmem_limit_bytes=VMEM_LIMIT),
        name="inproj",
    )(x2d, *consts)


def _gla_kernel(q_ref, k_ref, v_ref, la_ref, sg_ref, g_ref, s0_ref, o_ref, sfin_ref, st_ref, *, chunk, nsub, nseq):
    t = pl.program_id(1)

    @pl.when(t == 0)
    def _():
        st_ref[...] = s0_ref[...]

    C = chunk
    tl = C * nsub
    r_i = lax.broadcasted_iota(jnp.int32, (C, C), 0)
    c_i = lax.broadcasted_iota(jnp.int32, (C, C), 1)
    causal = c_i <= r_i
    grp = min(tl, MXU_DIM)
    rt = lax.broadcasted_iota(jnp.int32, (grp, grp), 0)
    ct = lax.broadcasted_iota(jnp.int32, (grp, grp), 1)
    chunk_tril = ((ct <= rt) & (ct >= (rt // C) * C)).astype(bf16)
    lane = lax.broadcasted_iota(jnp.int32, (1, LANES), 1)
    first_half = lane < GLA_DK
    mid = C // 2
    pairs = range(GLA_HEADS // 2)
    seqs = range(nseq)
    chunks = [(s, c) for s in seqs for c in range(nsub)]

    def lanes(p):
        return slice(p * LANES, (p + 1) * LANES)

    def one_head(x, hh):
        return jnp.where(first_half if hh == 0 else jnp.logical_not(first_half), x, jnp.zeros_like(x))

    bcum_all = []
    for s in seqs:
        la = la_ref[s]
        la_hi = la.astype(bf16)
        la_lo = (la - la_hi.astype(f32)).astype(bf16)
        bcum_all.append(jnp.concatenate(
            [_dot(chunk_tril, la_hi[r:r + grp]) + _dot(chunk_tril, la_lo[r:r + grp]) for r in range(0, tl, grp)],
            axis=0))

    q_in, dec, amat, upd = {}, {}, {}, {}
    for s, c in chunks:
        rows = slice(c * C, (c + 1) * C)
        bcum = bcum_all[s][rows]
        b_last = bcum[C - 1:C, :]
        ref = bcum[mid:mid + 1, :]
        q = q_ref[s, rows, :].astype(f32)
        k = k_ref[s, rows, :].astype(f32)
        q_in[s, c] = (q * jnp.exp2(bcum)).astype(bf16)
        q_a = (q * jnp.exp2(jnp.minimum(bcum - ref, GLA_EXP2_CAP))).astype(bf16)
        k_a = (k * jnp.exp2(jnp.minimum(ref - bcum, GLA_EXP2_CAP))).astype(bf16)
        k_d = (k * jnp.exp2(b_last - bcum)).astype(bf16)
        dec[s, c] = jnp.exp2(b_last)
        for p in pairs:
            for hh in range(2):
                h = 2 * p + hh
                v_h = v_ref[s, rows, h * GLA_DV:(h + 1) * GLA_DV]
                a = _dot_nt(one_head(q_a[:, lanes(p)], hh), k_a[:, lanes(p)])
                amat[s, c, h] = jnp.where(causal, a, 0.0).astype(bf16)
                upd[s, c, h] = _dot_tn(v_h, k_d[:, lanes(p)])

    states = {}
    for s in seqs:
        st = [st_ref[s, p] for p in pairs]
        for c in range(nsub):
            states[s, c] = [x.astype(bf16) for x in st]
            st = [dec[s, c][:, lanes(p)] * st[p] + jnp.where(first_half, upd[s, c, 2 * p], upd[s, c, 2 * p + 1])
                  for p in pairs]
        for p in pairs:
            st_ref[s, p] = st[p]

    for s, c in chunks:
        rows = slice(c * C, (c + 1) * C)
        for h in range(GLA_HEADS):
            p, hh = divmod(h, 2)
            vs = slice(h * GLA_DV, (h + 1) * GLA_DV)
            o = (_dot(amat[s, c, h], v_ref[s, rows, vs])
                 + _dot_nt(one_head(q_in[s, c][:, lanes(p)], hh), states[s, c][p]))
            o = o * lax.rsqrt(jnp.mean(o * o, axis=-1, keepdims=True) + RMS_EPS)
            o = o * g_ref[:, vs] * sg_ref[s, rows, vs].astype(f32)
            o_ref[s, rows, vs] = o.astype(bf16)

    @pl.when(t == pl.num_programs(1) - 1)
    def _():
        sfin_ref[...] = st_ref[...]


def _gla(gq, gk, gv, la, sg, g, s0t):
    B, L, _ = gq.shape
    C = min(GLA_CHUNK, L)
    tl = min(512, L)
    nseq = next(d for d in (4, 2, 1) if B % d == 0 and d * tl <= 1024)
    assert L % tl == 0 and tl % C == 0
    tok = lambda w: pl.BlockSpec((nseq, tl, w), lambda b, t: (b, t, 0))
    st_spec = pl.BlockSpec((nseq, 2, GLA_DV, LANES), lambda b, t: (b, 0, 0, 0))
    return pl.pallas_call(
        functools.partial(_gla_kernel, chunk=C, nsub=tl // C, nseq=nseq),
        grid=(B // nseq, L // tl),
        in_specs=[tok(GLA_QK), tok(GLA_QK), tok(GLA_V), tok(GLA_QK), tok(GLA_V),
                  pl.BlockSpec((1, GLA_V), lambda b, t: (0, 0)), st_spec],
        out_specs=[tok(GLA_V), st_spec],
        out_shape=(jax.ShapeDtypeStruct((B, L, GLA_V), bf16),
                   jax.ShapeDtypeStruct((B, 2, GLA_DV, LANES), f32)),
        scratch_shapes=[pltpu.VMEM((nseq, 2, GLA_DV, LANES), f32)],
        compiler_params=pltpu.CompilerParams(dimension_semantics=("arbitrary", "arbitrary"),
                                             vmem_limit_bytes=VMEM_LIMIT),
        name="gla_scan",
    )(gq, gk, gv, la, sg, g, s0t)


def _state_to_pairs(s):
    B = s.shape[0]
    s = s.reshape(B, 2, 2, GLA_DK, GLA_DV)
    return jnp.transpose(s, (0, 1, 4, 2, 3)).reshape(B, 2, GLA_DV, 2 * GLA_DK)


def _pairs_to_state(st):
    B = st.shape[0]
    st = st.reshape(B, 2, GLA_DV, 2, GLA_DK)
    return jnp.transpose(st, (0, 1, 3, 4, 2)).reshape(B, GLA_HEADS, GLA_DK, GLA_DV)


def _meta_kernel(lf_ref, nk_ref, qm_ref, ct_ref, start_ref, *, nc):
    H = FOX_HEADS
    r_i = lax.broadcasted_iota(jnp.int32, (LANES, LANES), 0)
    c_i = lax.broadcasted_iota(jnp.int32, (LANES, LANES), 1)
    upper = (r_i <= c_i).astype(f32)
    rj = lax.broadcasted_iota(jnp.int32, (nc, nc), 0)
    cj = lax.broadcasted_iota(jnp.int32, (nc, nc), 1)
    strict_lower = (cj < rj).astype(f32)
    eye = rj == cj

    def to_row(col):
        return jnp.sum(jnp.where(eye, jnp.broadcast_to(col, (nc, nc)), 0.0), axis=0, keepdims=True)

    ys = [_dot_hi(lf_ref[0, h], upper) for h in range(H)]
    prev = [_dot_hi(strict_lower, jnp.broadcast_to(y[:, LANES - 1:LANES], (nc, LANES))) for y in ys]
    start = None
    for h in range(H):
        c = (ys[h] + prev[h]) * LOG2E
        ct_ref[0, h] = c
        cmax = jnp.max(c, axis=1, keepdims=True)
        cmin = jnp.min(c, axis=1, keepdims=True)
        qmax = jnp.sqrt(qm_ref[0, :, H + h:H + h + 1])
        kmax = jnp.sqrt(jnp.max(jnp.max(nk_ref[0, h], axis=1, keepdims=True), axis=0, keepdims=True))
        thr = cmax + (2.0 * NORM_SLACK) * qmax * kmax + SKIP_T2
        needed = (to_row(cmin) <= thr) & (cj <= rj)
        first = jnp.min(jnp.where(needed, cj, nc), axis=1, keepdims=True)
        start = first if start is None else jnp.minimum(start, first)
    start_ref[0] = to_row(start.astype(f32)).astype(jnp.int32)


def _fox_meta(lf_t, nk2_t, qmx):
    B, _, nc, _ = lf_t.shape
    per_head = pl.BlockSpec((1, FOX_HEADS, nc, LANES), lambda b: (b, 0, 0, 0))
    return pl.pallas_call(
        functools.partial(_meta_kernel, nc=nc),
        grid=(B,),
        in_specs=[per_head, per_head, pl.BlockSpec((1, nc, LANES), lambda b: (b, 0, 0))],
        out_specs=[pl.BlockSpec((1, FOX_HEADS, nc, LANES), lambda b: (b, 0, 0, 0)),
                   pl.BlockSpec((1, 1, nc), lambda b: (b, 0, 0))],
        out_shape=(jax.ShapeDtypeStruct((B, FOX_HEADS, nc, LANES), f32),
                   jax.ShapeDtypeStruct((B, 1, nc), jnp.int32)),
        compiler_params=pltpu.CompilerParams(dimension_semantics=("arbitrary",), vmem_limit_bytes=VMEM_LIMIT),
        name="fox_meta",
    )(lf_t, nk2_t, qmx)


def _cumsum_kernel(lf_ref, ct_ref, *, nb, nc):
    r_i = lax.broadcasted_iota(jnp.int32, (LANES, LANES), 0)
    c_i = lax.broadcasted_iota(jnp.int32, (LANES, LANES), 1)
    upper = (r_i <= c_i).astype(f32)
    rj = lax.broadcasted_iota(jnp.int32, (nc, nc), 0)
    cj = lax.broadcasted_iota(jnp.int32, (nc, nc), 1)
    strict_lower = (cj < rj).astype(f32)
    seqs = [(s, h) for s in range(nb) for h in range(FOX_HEADS)]
    ys = [_dot_hi(lf_ref[s, h], upper) for s, h in seqs]
    prev = [_dot_hi(strict_lower, jnp.broadcast_to(y[:, LANES - 1:LANES], (nc, LANES))) for y in ys]
    for (s, h), y, p in zip(seqs, ys, prev):
        ct_ref[s, h] = (y + p) * LOG2E


def _fox_cumsum(lf_t):
    B, _, nc, _ = lf_t.shape
    nb = next(d for d in (4, 2, 1) if B % d == 0)
    spec = pl.BlockSpec((nb, FOX_HEADS, nc, LANES), lambda b: (b, 0, 0, 0))
    return pl.pallas_call(
        functools.partial(_cumsum_kernel, nb=nb, nc=nc),
        grid=(B // nb,),
        in_specs=[spec],
        out_specs=spec,
        out_shape=jax.ShapeDtypeStruct(lf_t.shape, f32),
        compiler_params=pltpu.CompilerParams(dimension_semantics=("arbitrary",), vmem_limit_bytes=VMEM_LIMIT),
        name="fox_cumsum",
    )(lf_t)


def _head_rows(h):
    return slice(h * FOX_DH, (h + 1) * FOX_DH)


def _pair_slab(h, width=LANES):
    return slice((h // 2) * width, (h // 2 + 1) * width)


def _v_aug(vt_h, h):
    one = jnp.ones_like(vt_h)
    return jnp.concatenate([vt_h, one] if h % 2 == 0 else [one, vt_h], axis=0)


def _fox_prompt_kernel(start_ref, q_ref, kp_ref, kc_ref, vp_ref, vc_ref,
                       ct_ref, kh_ref, vh_ref, o_ref, m_s, acc_s, kbuf, vbuf, sem, *, parts):
    T = ATT_BLOCK
    TQ = 2 * T
    H = FOX_HEADS
    b = pl.program_id(0)
    i = pl.program_id(1)
    r_i = lax.broadcasted_iota(jnp.int32, (TQ, 2 * TQ), 0)
    c_i = lax.broadcasted_iota(jnp.int32, (TQ, 2 * TQ), 1)
    visible = c_i <= r_i + TQ
    lane = lax.broadcasted_iota(jnp.int32, (1, LANES), 1)
    first_half = lane < FOX_DH
    halves = range(parts)
    bps = 2 * parts

    def q_rows(u):
        return slice(u * TQ, (u + 1) * TQ)

    def first_block(u):
        return bps * i + 2 * u - 2

    def head_q(u, h):
        hm = first_half if h % 2 == 0 else jnp.logical_not(first_half)
        qp = q_ref[0, q_rows(u), _pair_slab(h)]
        return jnp.where(hm, qp, jnp.zeros_like(qp))

    def head_base(u, h):
        return ct_ref[0, h, pl.ds(bps * i + 2 * u, 1), 0:1]

    def c_row(h, j):
        return ct_ref[0, h, pl.ds(j, 1), :]

    def window(u, prev_ref, cur_ref, rows):
        if u == 0:
            return jnp.concatenate([prev_ref[0, rows, :], cur_ref[0, rows, 0:TQ]], axis=1)
        return cur_ref[0, rows, (u - 1) * TQ:(u + 1) * TQ]

    scores = {}
    for u in halves:
        j0 = first_block(u)
        jc = jnp.maximum(j0, 0)
        for h in range(H):
            base = head_base(u, h)
            ck = jnp.concatenate([jnp.where(j0 >= 0, c_row(h, jc) - base, -NEG_BIG),
                                  jnp.where(j0 >= 0, c_row(h, jc + 1) - base, -NEG_BIG),
                                  c_row(h, j0 + 2) - base, c_row(h, j0 + 3) - base], axis=1)
            s = jnp.where(visible, _dot(head_q(u, h), window(u, kp_ref, kc_ref, _pair_slab(h))) - ck, NEG_BIG)
            m = jnp.max(s, axis=1, keepdims=True)
            m_s[h, q_rows(u), :] = m
            scores[u, h] = (s, m)
    for u in halves:
        for h in range(H):
            s, m = scores[u, h]
            acc_s[h, q_rows(u), :] = _dot_nt(jnp.exp2(s - m).astype(bf16),
                                            _v_aug(window(u, vp_ref, vc_ref, _head_rows(h)), h))

    def copies(j, slot):
        cols = pl.ds(pl.multiple_of(j * T, T), T)
        return (pltpu.make_async_copy(kh_ref.at[b, :, cols], kbuf.at[slot], sem.at[0, slot]),
                pltpu.make_async_copy(vh_ref.at[b, :, cols], vbuf.at[slot], sem.at[1, slot]))

    for u in halves:
        start = jnp.minimum(start_ref[b, bps * i + 2 * u], start_ref[b, bps * i + 2 * u + 1])
        n_far = jnp.maximum(first_block(u) - start, 0)

        @pl.when(n_far > 0)
        def _(u=u, start=start, n_far=n_far):
            for cp in copies(start, 0):
                cp.start()

            def body(t, carry):
                j = start + t
                slot = lax.rem(t, 2)
                for cp in copies(j, slot):
                    cp.wait()

                @pl.when(t + 1 < n_far)
                def _():
                    for cp in copies(j + 1, 1 - slot):
                        cp.start()

                for h in range(H):
                    ck = c_row(h, j) - head_base(u, h)
                    s = _dot(head_q(u, h), kbuf[slot, _pair_slab(h), :]) - ck
                    m_prev = m_s[h, q_rows(u), :]
                    m_new = jnp.maximum(m_prev, jnp.max(s, axis=1, keepdims=True))
                    alpha = jnp.exp2(m_prev - m_new)
                    p = jnp.exp2(s - m_new)
                    acc_s[h, q_rows(u), :] = (alpha * acc_s[h, q_rows(u), :]
                                             + _dot_nt(p.astype(bf16), _v_aug(vbuf[slot, _head_rows(h), :], h)))
                    m_s[h, q_rows(u), :] = m_new
                return carry

            lax.fori_loop(0, n_far, body, 0)

    for p in range(H // 2):
        even = acc_s[2 * p]
        odd = acc_s[2 * p + 1]
        numer = jnp.where(first_half, even, odd)
        denom = pltpu.roll(jnp.where(first_half, odd, even), FOX_DH, 1)
        o_ref[0, :, p * LANES:(p + 1) * LANES] = (numer / denom).astype(bf16)


def _fox_prompt(fq, fkbt, fvbt, ct, start):
    B, L, _ = fq.shape
    T = ATT_BLOCK
    TQ = 2 * T
    parts = next(d for d in (FOX_PARTS, 2, 1) if L % (d * TQ) == 0)
    TS = parts * TQ
    prev = pl.BlockSpec((1, FOX_W, TQ), lambda b, i, s: (b, 0, jnp.maximum(parts * i - 1, 0)))
    cur = pl.BlockSpec((1, FOX_W, TS), lambda b, i, s: (b, 0, i))
    grid_spec = pltpu.PrefetchScalarGridSpec(
        num_scalar_prefetch=1,
        grid=(B, L // TS),
        in_specs=[pl.BlockSpec((1, TS, FOX_W), lambda b, i, s: (b, i, 0)),
                  prev, cur, prev, cur,
                  pl.BlockSpec((1, FOX_HEADS, L // T, LANES), lambda b, i, s: (b, 0, 0, 0)),
                  pl.BlockSpec(memory_space=pl.ANY), pl.BlockSpec(memory_space=pl.ANY)],
        out_specs=pl.BlockSpec((1, TS, FOX_W), lambda b, i, s: (b, i, 0)),
        scratch_shapes=[pltpu.VMEM((FOX_HEADS, TS, 1), f32),
                        pltpu.VMEM((FOX_HEADS, TS, LANES), f32),
                        pltpu.VMEM((2, FOX_W, T), bf16), pltpu.VMEM((2, FOX_W, T), bf16),
                        pltpu.SemaphoreType.DMA((2, 2))],
    )
    return pl.pallas_call(
        functools.partial(_fox_prompt_kernel, parts=parts),
        grid_spec=grid_spec,
        out_shape=jax.ShapeDtypeStruct((B, L, FOX_W), bf16),
        compiler_params=pltpu.CompilerParams(dimension_semantics=("arbitrary", "arbitrary"),
                                             vmem_limit_bytes=VMEM_LIMIT),
        name="fox_prompt",
    )(start, fq, fkbt, fkbt, fvbt, fvbt, ct, fkbt, fvbt)


def _fox_sample_kernel(q_ref, kn_ref, vn_ref, kp_ref, vp_ref, ct_ref, o_ref, *, past, lq):
    H = FOX_HEADS
    r_i = lax.broadcasted_iota(jnp.int32, (lq, lq), 0)
    c_i = lax.broadcasted_iota(jnp.int32, (lq, lq), 1)
    causal = c_i <= r_i
    lane = lax.broadcasted_iota(jnp.int32, (1, LANES), 1)
    first_half = lane < FOX_DH
    probs = []
    for h in range(H):
        hm = first_half if h % 2 == 0 else jnp.logical_not(first_half)
        qp = q_ref[0, :, _pair_slab(h)]
        qm = jnp.where(hm, qp, jnp.zeros_like(qp))
        base = ct_ref[0, h:h + 1, past:past + 1]
        s_p = _dot(qm, kp_ref[0, _pair_slab(h), :].astype(bf16)) - (ct_ref[0, h:h + 1, 0:past] - base)
        s_n = _dot(qm, kn_ref[0, _pair_slab(h), :]) - (ct_ref[0, h:h + 1, past:past + lq] - base)
        s_n = jnp.where(causal, s_n, NEG_BIG)
        m = jnp.maximum(jnp.max(s_p, axis=1, keepdims=True), jnp.max(s_n, axis=1, keepdims=True))
        probs.append((jnp.exp2(s_p - m).astype(bf16), jnp.exp2(s_n - m).astype(bf16)))
    accs = []
    for h in range(H):
        p_p, p_n = probs[h]
        accs.append(_dot_nt(p_p, _v_aug(vp_ref[0, _head_rows(h), :].astype(bf16), h))
                    + _dot_nt(p_n, _v_aug(vn_ref[0, _head_rows(h), :], h)))
    for p in range(H // 2):
        numer = jnp.where(first_half, accs[2 * p], accs[2 * p + 1])
        denom = pltpu.roll(jnp.where(first_half, accs[2 * p + 1], accs[2 * p]), FOX_DH, 1)
        o_ref[0, :, p * LANES:(p + 1) * LANES] = (numer / denom).astype(bf16)


def _fox_sample(fq, fkbt, fvbt, kt_past, vt_past, ct_flat):
    B, lq, _ = fq.shape
    past = kt_past.shape[2]
    lpad = ct_flat.shape[2]
    new = pl.BlockSpec((1, FOX_W, lq), lambda b: (b, 0, 0))
    old = pl.BlockSpec((1, FOX_W, past), lambda b: (b, 0, 0))
    qo = pl.BlockSpec((1, lq, FOX_W), lambda b: (b, 0, 0))
    return pl.pallas_call(
        functools.partial(_fox_sample_kernel, past=past, lq=lq),
        grid=(B,),
        in_specs=[qo, new, new, old, old, pl.BlockSpec((1, FOX_HEADS, lpad), lambda b: (b, 0, 0))],
        out_specs=qo,
        out_shape=jax.ShapeDtypeStruct((B, lq, FOX_W), bf16),
        compiler_params=pltpu.CompilerParams(dimension_semantics=("arbitrary",), vmem_limit_bytes=VMEM_LIMIT),
        name="fox_sample",
    )(fq, fkbt, fvbt, kt_past, vt_past, ct_flat)


def _layer_norm(t, g, b):
    mu = jnp.mean(t, axis=-1, keepdims=True)
    d = t - mu
    var = jnp.mean(d * d, axis=-1, keepdims=True)
    return d * lax.rsqrt(var + LN_EPS) * g + b


def _mix_kernel(x_ref, og_ref, of_ref, wz_ref, bz_ref, wpg_ref, wpf_ref, wo_ref, g1_ref, b1_ref, x1_ref, *, alpha):
    groups = _row_groups(x_ref.shape[0])
    merged = []
    for rows in groups:
        xb = x_ref[rows, :].astype(bf16)
        zg = _dot(xb, wz_ref[:, :D_MODEL]) + bz_ref[:, :D_MODEL]
        zf = _dot(xb, wz_ref[:, D_MODEL:]) + bz_ref[:, D_MODEL:]
        m = (jax.nn.sigmoid(zg) * _dot(og_ref[rows, :], wpg_ref[...])
             + jax.nn.sigmoid(zf) * _dot(of_ref[rows, :], wpf_ref[...]))
        merged.append(m.astype(bf16))
    for rows, m in zip(groups, merged):
        t = alpha * x_ref[rows, :] + _dot(m, wo_ref[...])
        x1_ref[rows, :] = _layer_norm(t, g1_ref[...], b1_ref[...])


def _row_groups(tm):
    rows = MXU_DIM if tm % MXU_DIM == 0 else tm
    return tuple(slice(r, r + rows) for r in range(0, tm, rows))


def _ffn_kernel(x1_ref, wg_ref, wu_ref, wd_ref, g2_ref, b2_ref, y_ref, *, alpha, bounds):
    chunks = [slice(lo, hi) for lo, hi in zip(bounds[:-1], bounds[1:])]
    stages = [(rows, cs) for rows in _row_groups(x1_ref.shape[0]) for cs in chunks]
    acc = {}
    pending = None

    def down(rows, cs, hid):
        part = _dot(hid, wd_ref[cs, :])
        key = (rows.start, rows.stop)
        acc[key] = part if key not in acc else acc[key] + part
        if cs is chunks[-1]:
            x1 = x1_ref[rows, :]
            y_ref[rows, :] = _layer_norm(alpha * x1 + acc[key], g2_ref[...], b2_ref[...])

    for rows, cs in stages:
        xb = x1_ref[rows, :].astype(bf16)
        g = _dot(xb, wg_ref[:, cs])
        u = _dot(xb, wu_ref[:, cs])
        hid = (g * jax.nn.sigmoid(g) * u).astype(bf16)
        if pending is not None:
            down(*pending)
        pending = (rows, cs, hid)
    down(*pending)


def _mix(x2d, og, of, wz, bz, wpg, wpf, wo, g1, b1, alpha):
    n = x2d.shape[0]
    tm = min(TOKEN_TILE, n)
    row = lambda w: pl.BlockSpec((tm, w), lambda i: (i, 0))
    return pl.pallas_call(
        functools.partial(_mix_kernel, alpha=alpha),
        grid=(n // tm,),
        in_specs=[row(D_MODEL), row(GLA_V), row(FOX_W)] + [_const_spec(a.shape) for a in (wz, bz, wpg, wpf, wo, g1, b1)],
        out_specs=row(D_MODEL),
        out_shape=jax.ShapeDtypeStruct((n, D_MODEL), f32),
        compiler_params=pltpu.CompilerParams(dimension_semantics=("arbitrary",), vmem_limit_bytes=VMEM_LIMIT),
        name="mix",
    )(x2d, og, of, wz, bz, wpg, wpf, wo, g1, b1)


def _ffn(x1, wg, wu, wd, g2, b2, alpha):
    n = x1.shape[0]
    tm = min(TOKEN_TILE, n)
    row = lambda w: pl.BlockSpec((tm, w), lambda i: (i, 0))
    return pl.pallas_call(
        functools.partial(_ffn_kernel, alpha=alpha, bounds=FFN_BOUNDS),
        grid=(n // tm,),
        in_specs=[row(D_MODEL)] + [_const_spec(a.shape) for a in (wg, wu, wd, g2, b2)],
        out_specs=row(D_MODEL),
        out_shape=jax.ShapeDtypeStruct((n, D_MODEL), f32),
        compiler_params=pltpu.CompilerParams(dimension_semantics=("arbitrary",), vmem_limit_bytes=VMEM_LIMIT),
        name="ffn",
    )(x1, wg, wu, wd, g2, b2)


def _prep_weights(w_in, b_in, w_alpha2, b_alpha2, gla_norm_g, w_proj_gla, w_proj_fox, w_out,
                  ln1_g, ln1_b, w_ffn_gate, w_ffn_up, w_ffn_down, ln2_g, ln2_b):
    def cols(a, idx):
        return a[..., _OFF[idx]:_OFF[idx + 1]]

    order = (0, 1, 2, 4, 5, 3)
    pad = SMALL_W - GLA_RANK
    w1 = jnp.concatenate([cols(w_in, i).astype(bf16) for i in order] + [jnp.zeros((D_MODEL, pad), bf16)], axis=1)
    b1 = jnp.concatenate([cols(b_in, i) for i in order] + [jnp.zeros((pad,), f32)])[None, :]
    t_cols = (slice(_OFF[6], _OFF[9]),)
    ff_pad = FF_ROWS - FOX_HEADS
    wkv = jnp.pad(jnp.transpose(w_in[:, t_cols[0]].astype(bf16)), ((0, ff_pad), (0, 0)))
    bkv = jnp.broadcast_to(jnp.pad(b_in[t_cols[0]], (0, ff_pad))[:, None], (2 * FOX_W + FF_ROWS, LANES)).astype(f32)
    wa = jnp.zeros((SMALL_W, GLA_QK), f32).at[:GLA_RANK].set(w_alpha2).astype(bf16)
    head_of = np.arange(FOX_W) // FOX_DH
    indq = np.zeros((FOX_W, SMALL_W), np.float32)
    indq[np.arange(FOX_W), FOX_HEADS + head_of] = 1.0
    row = lambda a: a[None, :].astype(f32)
    return dict(
        w1=w1, b1=b1, wkv=wkv, bkv=bkv, wa=wa, ba=row(b_alpha2), indq=jnp.asarray(indq, bf16),
        g=row(gla_norm_g),
        wz=w_in[:, _OFF[9]:].astype(bf16), bz=row(b_in[_OFF[9]:]),
        wpg=w_proj_gla.astype(bf16), wpf=w_proj_fox.astype(bf16), wo=w_out.astype(bf16),
        g1=row(ln1_g), b1n=row(ln1_b),
        wg=w_ffn_gate.astype(bf16), wu=w_ffn_up.astype(bf16), wd=w_ffn_down.astype(bf16),
        g2=row(ln2_g), b2n=row(ln2_b),
    )


def _heads_last(t, B, L):
    return jnp.transpose(t.reshape(B, FOX_HEADS, FOX_DH, L), (0, 3, 1, 2))


def _layer(x, s0, k_past, v_past, lf_past, p, alpha):
    B, L, _ = x.shape
    n = B * L
    x2d = x.reshape(n, D_MODEL)
    prompt = k_past is None
    lo = L if prompt else n
    gq, gk, gv, la, sg, fq, fkt, fvt, fkbt, fvbt, lft, nk2t, qmx = _inproj(
        x2d, lo, p["w1"], p["b1"], p["wkv"], p["bkv"], p["wa"], p["ba"], p["indq"])
    r3 = lambda a: a.reshape(B, L, a.shape[-1])

    o_gla, st = _gla(r3(gq), r3(gk), r3(gv), r3(la), r3(sg), p["g"], _state_to_pairs(s0))
    gla_state = _pairs_to_state(st)

    if prompt:
        nc = L // LANES
        chunks = lambda t: t.reshape(B, FOX_HEADS, nc, LANES)
        ct, start = _fox_meta(chunks(lft), chunks(nk2t), qmx.reshape(B, nc, SMALL_W))
        o_fox = _fox_prompt(r3(fq), fkbt, fvbt, ct, start.reshape(B, nc))
        fk, fv = _heads_last(fkt, B, L), _heads_last(fvt, B, L)
        lf = jnp.transpose(lft, (0, 2, 1))
    else:
        past = k_past.shape[1]
        lk = past + L
        lpad = -(-lk // (8 * LANES)) * (8 * LANES)
        nc = lpad // LANES
        per_stream = lambda t: jnp.transpose(t.reshape(t.shape[1], B, L), (1, 0, 2))
        lft_s = per_stream(lft)
        lf_all_t = jnp.concatenate([jnp.transpose(lf_past.astype(f32), (0, 2, 1)), lft_s], axis=2)
        lf_all_t = jnp.pad(lf_all_t, ((0, 0), (0, 0), (0, lpad - lk))).reshape(B, FOX_HEADS, nc, LANES)
        ct = _fox_cumsum(lf_all_t)
        cache_t = lambda c: jnp.transpose(c, (0, 2, 3, 1)).reshape(B, FOX_W, past)
        o_fox = _fox_sample(r3(fq), per_stream(fkbt), per_stream(fvbt), cache_t(k_past), cache_t(v_past),
                            ct.reshape(B, FOX_HEADS, lpad))
        fk, fv = _heads_last(per_stream(fkt), B, L), _heads_last(per_stream(fvt), B, L)
        lf = jnp.transpose(lft_s, (0, 2, 1))

    x1 = _mix(x2d, o_gla.reshape(n, GLA_V), o_fox.reshape(n, FOX_W), p["wz"], p["bz"], p["wpg"], p["wpf"], p["wo"],
              p["g1"], p["b1n"], alpha)
    y = _ffn(x1, p["wg"], p["wu"], p["wd"], p["g2"], p["b2n"], alpha)
    return y.reshape(B, L, D_MODEL), gla_state, fk, fv, lf


def kernel(x_prompt, x_sample, state_gla, cache_fox_k, cache_fox_v, cache_fox_logf, w_in, b_in, w_alpha2, b_alpha2,
           gla_norm_g, w_proj_gla, w_proj_fox, w_out, ln1_g, ln1_b, w_ffn_gate, w_ffn_up, w_ffn_down, ln2_g, ln2_b):
    depth = w_in.shape[0]
    alpha = (2.0 * depth) ** 0.25
    yp, ys = x_prompt, x_sample
    outs_p, outs_s = [], []
    for l in range(depth):
        p = _prep_weights(w_in[l], b_in[l], w_alpha2[l], b_alpha2[l], gla_norm_g[l], w_proj_gla[l], w_proj_fox[l],
                          w_out[l], ln1_g[l], ln1_b[l], w_ffn_gate[l], w_ffn_up[l], w_ffn_down[l], ln2_g[l], ln2_b[l])
        s0 = jnp.zeros((yp.shape[0], GLA_HEADS, GLA_DK, GLA_DV), f32)
        yp, *rest_p = _layer(yp, s0, None, None, None, p, alpha)
        outs_p.append(rest_p)
        ys, *rest_s = _layer(ys, state_gla[l], cache_fox_k[l], cache_fox_v[l], cache_fox_logf[l], p, alpha)
        outs_s.append(rest_s)
    stack = lambda outs, i: jnp.stack([o[i] for o in outs])
    return (yp, ys,
            stack(outs_p, 0), stack(outs_p, 1), stack(outs_p, 2), stack(outs_p, 3),
            stack(outs_s, 0), stack(outs_s, 1), stack(outs_s, 2), stack(outs_s, 3))
```

```python
import functools

import jax
import jax.numpy as jnp
import numpy as np
from jax import lax
from jax.experimental import pallas as pl
from jax.experimental.pallas import tpu as pltpu

f32 = jnp.float32
bf16 = jnp.bfloat16

D_MODEL = 1024
GLA_HEADS = 4
GLA_DK = 64
GLA_DV = 128
GLA_RANK = 16
GLA_TAU = 16.0
GLA_QK = GLA_HEADS * GLA_DK
GLA_V = GLA_HEADS * GLA_DV
GLA_CHUNK = 64
FOX_HEADS = 8
FOX_DH = 64
FOX_W = FOX_HEADS * FOX_DH
D_FF = 2816
LN_EPS = 1e-5
RMS_EPS = 1e-5
SPLIT_SIZES = (GLA_QK, GLA_QK, GLA_V, GLA_RANK, GLA_V, FOX_W, FOX_W, FOX_W, FOX_HEADS, D_MODEL, D_MODEL)
_OFF = np.concatenate([[0], np.cumsum(SPLIT_SIZES)]).astype(int)

LANES = 128
MXU_DIM = 256
FFN_BOUNDS = (0, 4 * MXU_DIM, 8 * MXU_DIM, D_FF)
TOKEN_TILE = 1024
SMALL_W = LANES
FF_ROWS = 16
ATT_BLOCK = 128
FOX_PARTS = 2
LOG2E = 1.4426950408889634
SKIP_T2 = 110.0 * LOG2E
NORM_SLACK = 1.02
NEG_BIG = -1e30
VMEM_LIMIT = 56 * 1024 * 1024


def _dot(a, b):
    return jnp.dot(a, b, preferred_element_type=f32)


def _dot_nt(a, b):
    return lax.dot_general(a, b, (((1,), (1,)), ((), ())), preferred_element_type=f32)


def _dot_tn(a, b):
    return lax.dot_general(a, b, (((0,), (0,)), ((), ())), preferred_element_type=f32)


def _dot_hi(a, b):
    return jnp.dot(a, b, preferred_element_type=f32, precision=lax.Precision.HIGHEST)


def _log_sigmoid(x):
    return jnp.minimum(x, 0.0) - jnp.log1p(jnp.exp(-jnp.abs(x)))


def _const_spec(shape):
    return pl.BlockSpec(shape, lambda *_: (0,) * len(shape), pipeline_mode=pl.Buffered(1))


def _inproj_kernel(x_ref, w_ref, b_ref, wkv_ref, bkv_ref, wa_ref, ba_ref, indq_ref,
                   gq_ref, gk_ref, gv_ref, la_ref, sg_ref, fq_ref, fkt_ref, fvt_ref,
                   fkbt_ref, fvbt_ref, lft_ref, nk2t_ref, qmx_ref):
    xb = x_ref[...].astype(bf16)
    tm = x_ref.shape[0]

    def group(lo, width):
        return _dot(xb, w_ref[:, lo:lo + width]) + b_ref[:, lo:lo + width]

    o_gq, o_gk, o_gv = 0, GLA_QK, 2 * GLA_QK
    o_gr = o_gv + GLA_V
    o_fq = o_gr + GLA_V
    o_small = o_fq + FOX_W
    small_b = group(o_small, SMALL_W).astype(bf16)
    fq = (group(o_fq, FOX_W) * (FOX_DH ** -0.5 * LOG2E)).astype(bf16)
    fq_ref[...] = fq
    fq32 = fq.astype(f32)
    fq2 = (fq32 * fq32).astype(bf16)
    gq_ref[...] = (group(o_gq, GLA_QK) * (GLA_DK ** -0.5)).astype(bf16)
    gk_ref[...] = group(o_gk, GLA_QK).astype(bf16)
    gv_ref[...] = group(o_gv, GLA_V).astype(bf16)
    la_pre = _dot(small_b, wa_ref[...]) + ba_ref[...]
    la_ref[...] = _log_sigmoid(la_pre) * (1.0 / GLA_TAU)
    nq2 = _dot(fq2, indq_ref[...])
    qmx_ref[...] = jnp.max(nq2.reshape(tm // ATT_BLOCK, ATT_BLOCK, SMALL_W), axis=1)
    gr = group(o_gr, GLA_V)
    sg_ref[...] = (gr * jax.nn.sigmoid(gr)).astype(bf16)

    bias_t = jnp.concatenate([bkv_ref[...]] * (tm // LANES), axis=1)
    k_t = _dot_nt(wkv_ref[:FOX_W, :], xb) + bias_t[:FOX_W]
    fkt_ref[0] = k_t
    kb = k_t.astype(bf16)
    fkbt_ref[0] = kb
    k32 = kb.astype(f32)
    k2 = k32 * k32
    nk2t_ref[0] = jnp.concatenate(
        [jnp.sum(k2[h * FOX_DH:(h + 1) * FOX_DH], axis=0, keepdims=True) for h in range(FOX_HEADS)], axis=0)
    vf_t = _dot_nt(wkv_ref[FOX_W:, :], xb) + bias_t[FOX_W:]
    fvt_ref[0] = vf_t[:FOX_W]
    fvbt_ref[0] = vf_t[:FOX_W].astype(bf16)
    lft_ref[0] = _log_sigmoid(vf_t[FOX_W:FOX_W + FOX_HEADS])


def _inproj(x2d, lo, w1, b1, wkv, bkv, wa, ba, indq):
    n = x2d.shape[0]
    bo = n // lo
    tm = min(TOKEN_TILE, lo)
    assert lo % tm == 0 and tm % LANES == 0
    tpb = lo // tm
    row = lambda w: pl.BlockSpec((tm, w), lambda i: (i, 0))
    col = lambda w: pl.BlockSpec((1, w, tm), lambda i: (i // tpb, 0, i % tpb))
    rows_out = lambda w, dt: (jax.ShapeDtypeStruct((n, w), dt), row(w))
    cols_out = lambda w, dt: (jax.ShapeDtypeStruct((bo, w, lo), dt), col(w))
    outs = (
        rows_out(GLA_QK, bf16),
        rows_out(GLA_QK, bf16),
        rows_out(GLA_V, bf16),
        rows_out(GLA_QK, f32),
        rows_out(GLA_V, bf16),
        rows_out(FOX_W, bf16),
        cols_out(FOX_W, f32),
        cols_out(FOX_W, f32),
        cols_out(FOX_W, bf16),
        cols_out(FOX_W, bf16),
        cols_out(FOX_HEADS, f32),
        cols_out(FOX_HEADS, f32),
        (jax.ShapeDtypeStruct((n // ATT_BLOCK, SMALL_W), f32),
         pl.BlockSpec((tm // ATT_BLOCK, SMALL_W), lambda i: (i, 0))),
    )
    consts = (w1, b1, wkv, bkv, wa, ba, indq)
    return pl.pallas_call(
        _inproj_kernel,
        grid=(n // tm,),
        in_specs=[row(D_MODEL)] + [_const_spec(a.shape) for a in consts],
        out_specs=[o[1] for o in outs],
        out_shape=[o[0] for o in outs],
        compiler_params=pltpu.CompilerParams(dimension_semantics=("arbitrary",), vmem_limit_bytes=VMEM_LIMIT),
        name="inproj",
    )(x2d, *consts)


def _gla_kernel(q_ref, k_ref, v_ref, la_ref, sg_ref, g_ref, s0_ref, o_ref, sfin_ref, st_ref, *, chunk, nsub, nseq):
    t = pl.program_id(1)

    @pl.when(t == 0)
    def _():
        st_ref[...] = s0_ref[...]

    C = chunk
    tl = C * nsub
    r_i = lax.broadcasted_iota(jnp.int32, (C, C), 0)
    c_i = lax.broadcasted_iota(jnp.int32, (C, C), 1)
    causal = c_i <= r_i
    grp = min(tl, MXU_DIM)
    rt = lax.broadcasted_iota(jnp.int32, (grp, grp), 0)
    ct = lax.broadcasted_iota(jnp.int32, (grp, grp), 1)
    chunk_tril = ((ct <= rt) & (ct >= (rt // C) * C)).astype(bf16)
    lane = lax.broadcasted_iota(jnp.int32, (1, LANES), 1)
    first_half = lane < GLA_DK
    mid = C // 2
    pairs = range(GLA_HEADS // 2)
    seqs = range(nseq)
    chunks = [(s, c) for s in seqs for c in range(nsub)]

    def lanes(p):
        return slice(p * LANES, (p + 1) * LANES)

    def one_head(x, hh):
        return jnp.where(first_half if hh == 0 else jnp.logical_not(first_half), x, jnp.zeros_like(x))

    bcum_all = []
    for s in seqs:
        la = la_ref[s]
        la_hi = la.astype(bf16)
        la_lo = (la - la_hi.astype(f32)).astype(bf16)
        bcum_all.append(jnp.concatenate(
            [_dot(chunk_tril, la_hi[r:r + grp]) + _dot(chunk_tril, la_lo[r:r + grp]) for r in range(0, tl, grp)],
            axis=0))

    q_in, dec, amat, upd = {}, {}, {}, {}
    for s, c in chunks:
        rows = slice(c * C, (c + 1) * C)
        bcum = bcum_all[s][rows]
        b_last = bcum[C - 1:C, :]
        ref = bcum[mid:mid + 1, :]
        q = q_ref[s, rows, :].astype(f32)
        k = k_ref[s, rows, :].astype(f32)
        q_in[s, c] = (q * jnp.exp(bcum)).astype(bf16)
        q_a = (q * jnp.exp(jnp.clip(bcum - ref, -80.0, 80.0))).astype(bf16)
        k_a = (k * jnp.exp(jnp.clip(ref - bcum, -80.0, 80.0))).astype(bf16)
        k_d = (k * jnp.exp(b_last - bcum)).astype(bf16)
        dec[s, c] = jnp.exp(b_last)
        for p in pairs:
            for hh in range(2):
                h = 2 * p + hh
                v_h = v_ref[s, rows, h * GLA_DV:(h + 1) * GLA_DV]
                a = _dot_nt(one_head(q_a[:, lanes(p)], hh), k_a[:, lanes(p)])
                amat[s, c, h] = jnp.where(causal, a, 0.0).astype(bf16)
                upd[s, c, h] = _dot_tn(v_h, k_d[:, lanes(p)])

    states = {}
    for s in seqs:
        st = [st_ref[s, p] for p in pairs]
        for c in range(nsub):
            states[s, c] = [x.astype(bf16) for x in st]
            st = [dec[s, c][:, lanes(p)] * st[p] + jnp.where(first_half, upd[s, c, 2 * p], upd[s, c, 2 * p + 1])
                  for p in pairs]
        for p in pairs:
            st_ref[s, p] = st[p]

    for s, c in chunks:
        rows = slice(c * C, (c + 1) * C)
        for h in range(GLA_HEADS):
            p, hh = divmod(h, 2)
            vs = slice(h * GLA_DV, (h + 1) * GLA_DV)
            o = (_dot(amat[s, c, h], v_ref[s, rows, vs])
                 + _dot_nt(one_head(q_in[s, c][:, lanes(p)], hh), states[s, c][p]))
            o = o * lax.rsqrt(jnp.mean(o * o, axis=-1, keepdims=True) + RMS_EPS)
            o = o * g_ref[:, vs] * sg_ref[s, rows, vs].astype(f32)
            o_ref[s, rows, vs] = o.astype(bf16)

    @pl.when(t == pl.num_programs(1) - 1)
    def _():
        sfin_ref[...] = st_ref[...]


def _gla(gq, gk, gv, la, sg, g, s0t):
    B, L, _ = gq.shape
    C = min(GLA_CHUNK, L)
    tl = min(512, L)
    nseq = next(d for d in (4, 2, 1) if B % d == 0 and d * tl <= 1024)
    assert L % tl == 0 and tl % C == 0
    tok = lambda w: pl.BlockSpec((nseq, tl, w), lambda b, t: (b, t, 0))
    st_spec = pl.BlockSpec((nseq, 2, GLA_DV, LANES), lambda b, t: (b, 0, 0, 0))
    return pl.pallas_call(
        functools.partial(_gla_kernel, chunk=C, nsub=tl // C, nseq=nseq),
        grid=(B // nseq, L // tl),
        in_specs=[tok(GLA_QK), tok(GLA_QK), tok(GLA_V), tok(GLA_QK), tok(GLA_V),
                  pl.BlockSpec((1, GLA_V), lambda b, t: (0, 0)), st_spec],
        out_specs=[tok(GLA_V), st_spec],
        out_shape=(jax.ShapeDtypeStruct((B, L, GLA_V), bf16),
                   jax.ShapeDtypeStruct((B, 2, GLA_DV, LANES), f32)),
        scratch_shapes=[pltpu.VMEM((nseq, 2, GLA_DV, LANES), f32)],
        compiler_params=pltpu.CompilerParams(dimension_semantics=("arbitrary", "arbitrary"),
                                             vmem_limit_bytes=VMEM_LIMIT),
        name="gla_scan",
    )(gq, gk, gv, la, sg, g, s0t)


def _state_to_pairs(s):
    B = s.shape[0]
    s = s.reshape(B, 2, 2, GLA_DK, GLA_DV)
    return jnp.transpose(s, (0, 1, 4, 2, 3)).reshape(B, 2, GLA_DV, 2 * GLA_DK)


def _pairs_to_state(st):
    B = st.shape[0]
    st = st.reshape(B, 2, GLA_DV, 2, GLA_DK)
    return jnp.transpose(st, (0, 1, 3, 4, 2)).reshape(B, GLA_HEADS, GLA_DK, GLA_DV)


def _meta_kernel(lf_ref, nk_ref, qm_ref, ct_ref, start_ref, *, nc):
    H = FOX_HEADS
    r_i = lax.broadcasted_iota(jnp.int32, (LANES, LANES), 0)
    c_i = lax.broadcasted_iota(jnp.int32, (LANES, LANES), 1)
    upper = (r_i <= c_i).astype(f32)
    rj = lax.broadcasted_iota(jnp.int32, (nc, nc), 0)
    cj = lax.broadcasted_iota(jnp.int32, (nc, nc), 1)
    strict_lower = (cj < rj).astype(f32)
    eye = rj == cj

    def to_row(col):
        return jnp.sum(jnp.where(eye, jnp.broadcast_to(col, (nc, nc)), 0.0), axis=0, keepdims=True)

    ys = [_dot_hi(lf_ref[0, h], upper) for h in range(H)]
    prev = [_dot_hi(strict_lower, jnp.broadcast_to(y[:, LANES - 1:LANES], (nc, LANES))) for y in ys]
    start = None
    for h in range(H):
        c = (ys[h] + prev[h]) * LOG2E
        ct_ref[0, h] = c
        cmax = jnp.max(c, axis=1, keepdims=True)
        cmin = jnp.min(c, axis=1, keepdims=True)
        qmax = jnp.sqrt(qm_ref[0, :, H + h:H + h + 1])
        kmax = jnp.sqrt(jnp.max(jnp.max(nk_ref[0, h], axis=1, keepdims=True), axis=0, keepdims=True))
        thr = cmax + (2.0 * NORM_SLACK) * qmax * kmax + SKIP_T2
        needed = (to_row(cmin) <= thr) & (cj <= rj)
        first = jnp.min(jnp.where(needed, cj, nc), axis=1, keepdims=True)
        start = first if start is None else jnp.minimum(start, first)
    start_ref[0] = to_row(start.astype(f32)).astype(jnp.int32)


def _fox_meta(lf_t, nk2_t, qmx):
    B, _, nc, _ = lf_t.shape
    per_head = pl.BlockSpec((1, FOX_HEADS, nc, LANES), lambda b: (b, 0, 0, 0))
    return pl.pallas_call(
        functools.partial(_meta_kernel, nc=nc),
        grid=(B,),
        in_specs=[per_head, per_head, pl.BlockSpec((1, nc, LANES), lambda b: (b, 0, 0))],
        out_specs=[pl.BlockSpec((1, FOX_HEADS, nc, LANES), lambda b: (b, 0, 0, 0)),
                   pl.BlockSpec((1, 1, nc), lambda b: (b, 0, 0))],
        out_shape=(jax.ShapeDtypeStruct((B, FOX_HEADS, nc, LANES), f32),
                   jax.ShapeDtypeStruct((B, 1, nc), jnp.int32)),
        compiler_params=pltpu.CompilerParams(dimension_semantics=("arbitrary",), vmem_limit_bytes=VMEM_LIMIT),
        name="fox_meta",
    )(lf_t, nk2_t, qmx)


def _cumsum_kernel(lf_ref, ct_ref, *, nb, nc):
    r_i = lax.broadcasted_iota(jnp.int32, (LANES, LANES), 0)
    c_i = lax.broadcasted_iota(jnp.int32, (LANES, LANES), 1)
    upper = (r_i <= c_i).astype(f32)
    rj = lax.broadcasted_iota(jnp.int32, (nc, nc), 0)
    cj = lax.broadcasted_iota(jnp.int32, (nc, nc), 1)
    strict_lower = (cj < rj).astype(f32)
    seqs = [(s, h) for s in range(nb) for h in range(FOX_HEADS)]
    ys = [_dot_hi(lf_ref[s, h], upper) for s, h in seqs]
    prev = [_dot_hi(strict_lower, jnp.broadcast_to(y[:, LANES - 1:LANES], (nc, LANES))) for y in ys]
    for (s, h), y, p in zip(seqs, ys, prev):
        ct_ref[s, h] = (y + p) * LOG2E


def _fox_cumsum(lf_t):
    B, _, nc, _ = lf_t.shape
    nb = next(d for d in (4, 2, 1) if B % d == 0)
    spec = pl.BlockSpec((nb, FOX_HEADS, nc, LANES), lambda b: (b, 0, 0, 0))
    return pl.pallas_call(
        functools.partial(_cumsum_kernel, nb=nb, nc=nc),
        grid=(B // nb,),
        in_specs=[spec],
        out_specs=spec,
        out_shape=jax.ShapeDtypeStruct(lf_t.shape, f32),
        compiler_params=pltpu.CompilerParams(dimension_semantics=("arbitrary",), vmem_limit_bytes=VMEM_LIMIT),
        name="fox_cumsum",
    )(lf_t)


def _head_rows(h):
    return slice(h * FOX_DH, (h + 1) * FOX_DH)


def _pair_slab(h, width=LANES):
    return slice((h // 2) * width, (h // 2 + 1) * width)


def _v_aug(vt_h, h):
    one = jnp.ones_like(vt_h)
    return jnp.concatenate([vt_h, one] if h % 2 == 0 else [one, vt_h], axis=0)


def _fox_prompt_kernel(start_ref, q_ref, kp_ref, kc_ref, vp_ref, vc_ref,
                       ct_ref, kh_ref, vh_ref, o_ref, m_s, acc_s, kbuf, vbuf, sem, *, parts):
    T = ATT_BLOCK
    TQ = 2 * T
    H = FOX_HEADS
    b = pl.program_id(0)
    i = pl.program_id(1)
    r_i = lax.broadcasted_iota(jnp.int32, (TQ, 2 * TQ), 0)
    c_i = lax.broadcasted_iota(jnp.int32, (TQ, 2 * TQ), 1)
    visible = c_i <= r_i + TQ
    lane = lax.broadcasted_iota(jnp.int32, (1, LANES), 1)
    first_half = lane < FOX_DH
    halves = range(parts)
    bps = 2 * parts

    def q_rows(u):
        return slice(u * TQ, (u + 1) * TQ)

    def first_block(u):
        return bps * i + 2 * u - 2

    def head_q(u, h):
        hm = first_half if h % 2 == 0 else jnp.logical_not(first_half)
        qp = q_ref[0, q_rows(u), _pair_slab(h)]
        return jnp.where(hm, qp, jnp.zeros_like(qp))

    def head_base(u, h):
        return ct_ref[0, h, pl.ds(bps * i + 2 * u, 1), 0:1]

    def c_row(h, j):
        return ct_ref[0, h, pl.ds(j, 1), :]

    def window(u, prev_ref, cur_ref, rows):
        if u == 0:
            return jnp.concatenate([prev_ref[0, rows, :], cur_ref[0, rows, 0:TQ]], axis=1)
        return cur_ref[0, rows, (u - 1) * TQ:(u + 1) * TQ]

    scores = {}
    for u in halves:
        j0 = first_block(u)
        jc = jnp.maximum(j0, 0)
        for h in range(H):
            base = head_base(u, h)
            ck = jnp.concatenate([jnp.where(j0 >= 0, c_row(h, jc) - base, -NEG_BIG),
                                  jnp.where(j0 >= 0, c_row(h, jc + 1) - base, -NEG_BIG),
                                  c_row(h, j0 + 2) - base, c_row(h, j0 + 3) - base], axis=1)
            s = jnp.where(visible, _dot(head_q(u, h), window(u, kp_ref, kc_ref, _pair_slab(h))) - ck, NEG_BIG)
            m = jnp.max(s, axis=1, keepdims=True)
            m_s[h, q_rows(u), :] = m
            scores[u, h] = (s, m)
    for u in halves:
        for h in range(H):
            s, m = scores[u, h]
            acc_s[h, q_rows(u), :] = _dot_nt(jnp.exp2(s - m).astype(bf16),
                                            _v_aug(window(u, vp_ref, vc_ref, _head_rows(h)), h))

    def copies(j, slot):
        cols = pl.ds(pl.multiple_of(j * T, T), T)
        return (pltpu.make_async_copy(kh_ref.at[b, :, cols], kbuf.at[slot], sem.at[0, slot]),
                pltpu.make_async_copy(vh_ref.at[b, :, cols], vbuf.at[slot], sem.at[1, slot]))

    for u in halves:
        start = jnp.minimum(start_ref[b, bps * i + 2 * u], start_ref[b, bps * i + 2 * u + 1])
        n_far = jnp.maximum(first_block(u) - start, 0)

        @pl.when(n_far > 0)
        def _(u=u, start=start, n_far=n_far):
            for cp in copies(start, 0):
                cp.start()

            def body(t, carry):
                j = start + t
                slot = lax.rem(t, 2)
                for cp in copies(j, slot):
                    cp.wait()

                @pl.when(t + 1 < n_far)
                def _():
                    for cp in copies(j + 1, 1 - slot):
                        cp.start()

                for h in range(H):
                    ck = c_row(h, j) - head_base(u, h)
                    s = _dot(head_q(u, h), kbuf[slot, _pair_slab(h), :]) - ck
                    m_prev = m_s[h, q_rows(u), :]
                    m_new = jnp.maximum(m_prev, jnp.max(s, axis=1, keepdims=True))
                    alpha = jnp.exp2(m_prev - m_new)
                    p = jnp.exp2(s - m_new)
                    acc_s[h, q_rows(u), :] = (alpha * acc_s[h, q_rows(u), :]
                                             + _dot_nt(p.astype(bf16), _v_aug(vbuf[slot, _head_rows(h), :], h)))
                    m_s[h, q_rows(u), :] = m_new
                return carry

            lax.fori_loop(0, n_far, body, 0)

    for p in range(H // 2):
        even = acc_s[2 * p]
        odd = acc_s[2 * p + 1]
        numer = jnp.where(first_half, even, odd)
        denom = pltpu.roll(jnp.where(first_half, odd, even), FOX_DH, 1)
        o_ref[0, :, p * LANES:(p + 1) * LANES] = (numer / denom).astype(bf16)


def _fox_prompt(fq, fkbt, fvbt, ct, start):
    B, L, _ = fq.shape
    T = ATT_BLOCK
    TQ = 2 * T
    parts = next(d for d in (FOX_PARTS, 2, 1) if L % (d * TQ) == 0)
    TS = parts * TQ
    prev = pl.BlockSpec((1, FOX_W, TQ), lambda b, i, s: (b, 0, jnp.maximum(parts * i - 1, 0)))
    cur = pl.BlockSpec((1, FOX_W, TS), lambda b, i, s: (b, 0, i))
    grid_spec = pltpu.PrefetchScalarGridSpec(
        num_scalar_prefetch=1,
        grid=(B, L // TS),
        in_specs=[pl.BlockSpec((1, TS, FOX_W), lambda b, i, s: (b, i, 0)),
                  prev, cur, prev, cur,
                  pl.BlockSpec((1, FOX_HEADS, L // T, LANES), lambda b, i, s: (b, 0, 0, 0)),
                  pl.BlockSpec(memory_space=pl.ANY), pl.BlockSpec(memory_space=pl.ANY)],
        out_specs=pl.BlockSpec((1, TS, FOX_W), lambda b, i, s: (b, i, 0)),
        scratch_shapes=[pltpu.VMEM((FOX_HEADS, TS, 1), f32),
                        pltpu.VMEM((FOX_HEADS, TS, LANES), f32),
                        pltpu.VMEM((2, FOX_W, T), bf16), pltpu.VMEM((2, FOX_W, T), bf16),
                        pltpu.SemaphoreType.DMA((2, 2))],
    )
    return pl.pallas_call(
        functools.partial(_fox_prompt_kernel, parts=parts),
        grid_spec=grid_spec,
        out_shape=jax.ShapeDtypeStruct((B, L, FOX_W), bf16),
        compiler_params=pltpu.CompilerParams(dimension_semantics=("arbitrary", "arbitrary"),
                                             vmem_limit_bytes=VMEM_LIMIT),
        name="fox_prompt",
    )(start, fq, fkbt, fkbt, fvbt, fvbt, ct, fkbt, fvbt)


def _fox_sample_kernel(q_ref, kn_ref, vn_ref, kp_ref, vp_ref, ct_ref, o_ref, *, past, lq):
    H = FOX_HEADS
    r_i = lax.broadcasted_iota(jnp.int32, (lq, lq), 0)
    c_i = lax.broadcasted_iota(jnp.int32, (lq, lq), 1)
    causal = c_i <= r_i
    lane = lax.broadcasted_iota(jnp.int32, (1, LANES), 1)
    first_half = lane < FOX_DH
    probs = []
    for h in range(H):
        hm = first_half if h % 2 == 0 else jnp.logical_not(first_half)
        qp = q_ref[0, :, _pair_slab(h)]
        qm = jnp.where(hm, qp, jnp.zeros_like(qp))
        base = ct_ref[0, h:h + 1, past:past + 1]
        s_p = _dot(qm, kp_ref[0, _pair_slab(h), :].astype(bf16)) - (ct_ref[0, h:h + 1, 0:past] - base)
        s_n = _dot(qm, kn_ref[0, _pair_slab(h), :]) - (ct_ref[0, h:h + 1, past:past + lq] - base)
        s_n = jnp.where(causal, s_n, NEG_BIG)
        m = jnp.maximum(jnp.max(s_p, axis=1, keepdims=True), jnp.max(s_n, axis=1, keepdims=True))
        probs.append((jnp.exp2(s_p - m).astype(bf16), jnp.exp2(s_n - m).astype(bf16)))
    accs = []
    for h in range(H):
        p_p, p_n = probs[h]
        accs.append(_dot_nt(p_p, _v_aug(vp_ref[0, _head_rows(h), :].astype(bf16), h))
                    + _dot_nt(p_n, _v_aug(vn_ref[0, _head_rows(h), :], h)))
    for p in range(H // 2):
        numer = jnp.where(first_half, accs[2 * p], accs[2 * p + 1])
        denom = pltpu.roll(jnp.where(first_half, accs[2 * p + 1], accs[2 * p]), FOX_DH, 1)
        o_ref[0, :, p * LANES:(p + 1) * LANES] = (numer / denom).astype(bf16)


def _fox_sample(fq, fkbt, fvbt, kt_past, vt_past, ct_flat):
    B, lq, _ = fq.shape
    past = kt_past.shape[2]
    lpad = ct_flat.shape[2]
    new = pl.BlockSpec((1, FOX_W, lq), lambda b: (b, 0, 0))
    old = pl.BlockSpec((1, FOX_W, past), lambda b: (b, 0, 0))
    qo = pl.BlockSpec((1, lq, FOX_W), lambda b: (b, 0, 0))
    return pl.pallas_call(
        functools.partial(_fox_sample_kernel, past=past, lq=lq),
        grid=(B,),
        in_specs=[qo, new, new, old, old, pl.BlockSpec((1, FOX_HEADS, lpad), lambda b: (b, 0, 0))],
        out_specs=qo,
        out_shape=jax.ShapeDtypeStruct((B, lq, FOX_W), bf16),
        compiler_params=pltpu.CompilerParams(dimension_semantics=("arbitrary",), vmem_limit_bytes=VMEM_LIMIT),
        name="fox_sample",
    )(fq, fkbt, fvbt, kt_past, vt_past, ct_flat)


def _layer_norm(t, g, b):
    mu = jnp.mean(t, axis=-1, keepdims=True)
    d = t - mu
    var = jnp.mean(d * d, axis=-1, keepdims=True)
    return d * lax.rsqrt(var + LN_EPS) * g + b


def _mix_kernel(x_ref, og_ref, of_ref, wz_ref, bz_ref, wpg_ref, wpf_ref, wo_ref, g1_ref, b1_ref, x1_ref, *, alpha):
    groups = _row_groups(x_ref.shape[0])
    merged = []
    for rows in groups:
        xb = x_ref[rows, :].astype(bf16)
        zg = _dot(xb, wz_ref[:, :D_MODEL]) + bz_ref[:, :D_MODEL]
        zf = _dot(xb, wz_ref[:, D_MODEL:]) + bz_ref[:, D_MODEL:]
        m = (jax.nn.sigmoid(zg) * _dot(og_ref[rows, :], wpg_ref[...])
             + jax.nn.sigmoid(zf) * _dot(of_ref[rows, :], wpf_ref[...]))
        merged.append(m.astype(bf16))
    for rows, m in zip(groups, merged):
        t = alpha * x_ref[rows, :] + _dot(m, wo_ref[...])
        x1_ref[rows, :] = _layer_norm(t, g1_ref[...], b1_ref[...])


def _row_groups(tm):
    rows = MXU_DIM if tm % MXU_DIM == 0 else tm
    return tuple(slice(r, r + rows) for r in range(0, tm, rows))


def _ffn_kernel(x1_ref, wg_ref, wu_ref, wd_ref, g2_ref, b2_ref, y_ref, *, alpha, bounds):
    chunks = [slice(lo, hi) for lo, hi in zip(bounds[:-1], bounds[1:])]
    stages = [(rows, cs) for rows in _row_groups(x1_ref.shape[0]) for cs in chunks]
    acc = {}
    pending = None

    def down(rows, cs, hid):
        part = _dot(hid, wd_ref[cs, :])
        key = (rows.start, rows.stop)
        acc[key] = part if key not in acc else acc[key] + part
        if cs is chunks[-1]:
            x1 = x1_ref[rows, :]
            y_ref[rows, :] = _layer_norm(alpha * x1 + acc[key], g2_ref[...], b2_ref[...])

    for rows, cs in stages:
        xb = x1_ref[rows, :].astype(bf16)
        g = _dot(xb, wg_ref[:, cs])
        u = _dot(xb, wu_ref[:, cs])
        hid = (g * jax.nn.sigmoid(g) * u).astype(bf16)
        if pending is not None:
            down(*pending)
        pending = (rows, cs, hid)
    down(*pending)


def _mix(x2d, og, of, wz, bz, wpg, wpf, wo, g1, b1, alpha):
    n = x2d.shape[0]
    tm = min(TOKEN_TILE, n)
    row = lambda w: pl.BlockSpec((tm, w), lambda i: (i, 0))
    return pl.pallas_call(
        functools.partial(_mix_kernel, alpha=alpha),
        grid=(n // tm,),
        in_specs=[row(D_MODEL), row(GLA_V), row(FOX_W)] + [_const_spec(a.shape) for a in (wz, bz, wpg, wpf, wo, g1, b1)],
        out_specs=row(D_MODEL),
        out_shape=jax.ShapeDtypeStruct((n, D_MODEL), f32),
        compiler_params=pltpu.CompilerParams(dimension_semantics=("arbitrary",), vmem_limit_bytes=VMEM_LIMIT),
        name="mix",
    )(x2d, og, of, wz, bz, wpg, wpf, wo, g1, b1)


def _mixffn_kernel(x_ref, og_ref, of_ref, wz_ref, bz_ref, wpg_ref, wpf_ref, wo_ref, g1_ref, b1_ref,
                   wg_ref, wu_ref, wd_ref, g2_ref, b2_ref, y_ref, *, alpha, bounds):
    chunks = [slice(lo, hi) for lo, hi in zip(bounds[:-1], bounds[1:])]
    groups = _row_groups(x_ref.shape[0])

    def swiglu(rows, x1):
        xb = x1.astype(bf16)
        acc, pending = None, None
        for cs in chunks:
            g = _dot(xb, wg_ref[:, cs])
            u = _dot(xb, wu_ref[:, cs])
            hid = (g * jax.nn.sigmoid(g) * u).astype(bf16)
            if pending is not None:
                part = _dot(pending[1], wd_ref[pending[0], :])
                acc = part if acc is None else acc + part
            pending = (cs, hid)
        part = _dot(pending[1], wd_ref[pending[0], :])
        acc = part if acc is None else acc + part
        y_ref[rows, :] = _layer_norm(alpha * x1 + acc, g2_ref[...], b2_ref[...])

    prev = None
    for rows in groups:
        xb = x_ref[rows, :].astype(bf16)
        zg = _dot(xb, wz_ref[:, :D_MODEL]) + bz_ref[:, :D_MODEL]
        zf = _dot(xb, wz_ref[:, D_MODEL:]) + bz_ref[:, D_MODEL:]
        m = (jax.nn.sigmoid(zg) * _dot(og_ref[rows, :], wpg_ref[...])
             + jax.nn.sigmoid(zf) * _dot(of_ref[rows, :], wpf_ref[...])).astype(bf16)
        if prev is not None:
            swiglu(*prev)
        t = alpha * x_ref[rows, :] + _dot(m, wo_ref[...])
        prev = (rows, _layer_norm(t, g1_ref[...], b1_ref[...]))
    swiglu(*prev)


def _mixffn(x2d, og, of, wz, bz, wpg, wpf, wo, g1, b1, wg, wu, wd, g2, b2, alpha):
    n = x2d.shape[0]
    tm = min(TOKEN_TILE // 2, n)
    row = lambda w: pl.BlockSpec((tm, w), lambda i: (i, 0))
    consts = (wz, bz, wpg, wpf, wo, g1, b1, wg, wu, wd, g2, b2)
    return pl.pallas_call(
        functools.partial(_mixffn_kernel, alpha=alpha, bounds=FFN_BOUNDS),
        grid=(n // tm,),
        in_specs=[row(D_MODEL), row(GLA_V), row(FOX_W)] + [_const_spec(a.shape) for a in consts],
        out_specs=row(D_MODEL),
        out_shape=jax.ShapeDtypeStruct((n, D_MODEL), f32),
        compiler_params=pltpu.CompilerParams(dimension_semantics=("arbitrary",), vmem_limit_bytes=VMEM_LIMIT),
        name="mixffn",
    )(x2d, og, of, *consts)


def _ffn(x1, wg, wu, wd, g2, b2, alpha):
    n = x1.shape[0]
    tm = min(TOKEN_TILE, n)
    row = lambda w: pl.BlockSpec((tm, w), lambda i: (i, 0))
    return pl.pallas_call(
        functools.partial(_ffn_kernel, alpha=alpha, bounds=FFN_BOUNDS),
        grid=(n // tm,),
        in_specs=[row(D_MODEL)] + [_const_spec(a.shape) for a in (wg, wu, wd, g2, b2)],
        out_specs=row(D_MODEL),
        out_shape=jax.ShapeDtypeStruct((n, D_MODEL), f32),
        compiler_params=pltpu.CompilerParams(dimension_semantics=("arbitrary",), vmem_limit_bytes=VMEM_LIMIT),
        name="ffn",
    )(x1, wg, wu, wd, g2, b2)


def _prep_weights(w_in, b_in, w_alpha2, b_alpha2, gla_norm_g, w_proj_gla, w_proj_fox, w_out,
                  ln1_g, ln1_b, w_ffn_gate, w_ffn_up, w_ffn_down, ln2_g, ln2_b):
    def cols(a, idx):
        return a[..., _OFF[idx]:_OFF[idx + 1]]

    order = (0, 1, 2, 4, 5, 3)
    pad = SMALL_W - GLA_RANK
    w1 = jnp.concatenate([cols(w_in, i).astype(bf16) for i in order] + [jnp.zeros((D_MODEL, pad), bf16)], axis=1)
    b1 = jnp.concatenate([cols(b_in, i) for i in order] + [jnp.zeros((pad,), f32)])[None, :]
    t_cols = (slice(_OFF[6], _OFF[9]),)
    ff_pad = FF_ROWS - FOX_HEADS
    wkv = jnp.pad(jnp.transpose(w_in[:, t_cols[0]].astype(bf16)), ((0, ff_pad), (0, 0)))
    bkv = jnp.broadcast_to(jnp.pad(b_in[t_cols[0]], (0, ff_pad))[:, None], (2 * FOX_W + FF_ROWS, LANES)).astype(f32)
    wa = jnp.zeros((SMALL_W, GLA_QK), f32).at[:GLA_RANK].set(w_alpha2).astype(bf16)
    head_of = np.arange(FOX_W) // FOX_DH
    indq = np.zeros((FOX_W, SMALL_W), np.float32)
    indq[np.arange(FOX_W), FOX_HEADS + head_of] = 1.0
    row = lambda a: a[None, :].astype(f32)
    return dict(
        w1=w1, b1=b1, wkv=wkv, bkv=bkv, wa=wa, ba=row(b_alpha2), indq=jnp.asarray(indq, bf16),
        g=row(gla_norm_g),
        wz=w_in[:, _OFF[9]:].astype(bf16), bz=row(b_in[_OFF[9]:]),
        wpg=w_proj_gla.astype(bf16), wpf=w_proj_fox.astype(bf16), wo=w_out.astype(bf16),
        g1=row(ln1_g), b1n=row(ln1_b),
        wg=w_ffn_gate.astype(bf16), wu=w_ffn_up.astype(bf16), wd=w_ffn_down.astype(bf16),
        g2=row(ln2_g), b2n=row(ln2_b),
    )


def _heads_last(t, B, L):
    return jnp.transpose(t.reshape(B, FOX_HEADS, FOX_DH, L), (0, 3, 1, 2))


def _layer(x, s0, k_past, v_past, lf_past, p, alpha):
    B, L, _ = x.shape
    n = B * L
    x2d = x.reshape(n, D_MODEL)
    prompt = k_past is None
    lo = L if prompt else n
    gq, gk, gv, la, sg, fq, fkt, fvt, fkbt, fvbt, lft, nk2t, qmx = _inproj(
        x2d, lo, p["w1"], p["b1"], p["wkv"], p["bkv"], p["wa"], p["ba"], p["indq"])
    r3 = lambda a: a.reshape(B, L, a.shape[-1])

    o_gla, st = _gla(r3(gq), r3(gk), r3(gv), r3(la), r3(sg), p["g"], _state_to_pairs(s0))
    gla_state = _pairs_to_state(st)

    if prompt:
        nc = L // LANES
        chunks = lambda t: t.reshape(B, FOX_HEADS, nc, LANES)
        ct, start = _fox_meta(chunks(lft), chunks(nk2t), qmx.reshape(B, nc, SMALL_W))
        o_fox = _fox_prompt(r3(fq), fkbt, fvbt, ct, start.reshape(B, nc))
        fk, fv = _heads_last(fkt, B, L), _heads_last(fvt, B, L)
        lf = jnp.transpose(lft, (0, 2, 1))
    else:
        past = k_past.shape[1]
        lk = past + L
        lpad = -(-lk // (8 * LANES)) * (8 * LANES)
        nc = lpad // LANES
        per_stream = lambda t: jnp.transpose(t.reshape(t.shape[1], B, L), (1, 0, 2))
        lft_s = per_stream(lft)
        lf_all_t = jnp.concatenate([jnp.transpose(lf_past.astype(f32), (0, 2, 1)), lft_s], axis=2)
        lf_all_t = jnp.pad(lf_all_t, ((0, 0), (0, 0), (0, lpad - lk))).reshape(B, FOX_HEADS, nc, LANES)
        ct = _fox_cumsum(lf_all_t)
        cache_t = lambda c: jnp.transpose(c, (0, 2, 3, 1)).reshape(B, FOX_W, past)
        o_fox = _fox_sample(r3(fq), per_stream(fkbt), per_stream(fvbt), cache_t(k_past), cache_t(v_past),
                            ct.reshape(B, FOX_HEADS, lpad))
        fk, fv = _heads_last(per_stream(fkt), B, L), _heads_last(per_stream(fvt), B, L)
        lf = jnp.transpose(lft_s, (0, 2, 1))

    y = _mixffn(x2d, o_gla.reshape(n, GLA_V), o_fox.reshape(n, FOX_W), p["wz"], p["bz"], p["wpg"], p["wpf"], p["wo"],
                p["g1"], p["b1n"], p["wg"], p["wu"], p["wd"], p["g2"], p["b2n"], alpha)
    return y.reshape(B, L, D_MODEL), gla_state, fk, fv, lf


def kernel(x_prompt, x_sample, state_gla, cache_fox_k, cache_fox_v, cache_fox_logf, w_in, b_in, w_alpha2, b_alpha2,
           gla_norm_g, w_proj_gla, w_proj_fox, w_out, ln1_g, ln1_b, w_ffn_gate, w_ffn_up, w_ffn_down, ln2_g, ln2_b):
    depth = w_in.shape[0]
    alpha = (2.0 * depth) ** 0.25
    yp, ys = x_prompt, x_sample
    outs_p, outs_s = [], []
    for l in range(depth):
        p = _prep_weights(w_in[l], b_in[l], w_alpha2[l], b_alpha2[l], gla_norm_g[l], w_proj_gla[l], w_proj_fox[l],
                          w_out[l], ln1_g[l], ln1_b[l], w_ffn_gate[l], w_ffn_up[l], w_ffn_down[l], ln2_g[l], ln2_b[l])
        s0 = jnp.zeros((yp.shape[0], GLA_HEADS, GLA_DK, GLA_DV), f32)
        yp, *rest_p = _layer(yp, s0, None, None, None, p, alpha)
        outs_p.append(rest_p)
        ys, *rest_s = _layer(ys, state_gla[l], cache_fox_k[l], cache_fox_v[l], cache_fox_logf[l], p, alpha)
        outs_s.append(rest_s)
    stack = lambda outs, i: jnp.stack([o[i] for o in outs])
    return (yp, ys,
            stack(outs_p, 0), stack(outs_p, 1), stack(outs_p, 2), stack(outs_p, 3),
            stack(outs_s, 0), stack(outs_s, 1), stack(outs_s, 2), stack(outs_s, 3))
```
